```python
import math
import jax
import jax.numpy as jnp
from jax import lax
import numpy as np

D_MODEL = 1024
BATCH = 8
SEQ = 4096
DEPTH = 2

CTX_LEN = 256
GRID_W = 64
ROPE_THETA = 10000.0
Q_BLOCK = 128
EPS = 1e-6
N_MOD = 6

D_RNN = 512
RNN_BLOCKS = 8
RNN_BW = D_RNN // RNN_BLOCKS
CONV_W = 4
LRU_C = 8.0

DIFF_HEADS = 4
DIFF_DK = 64
DIFF_DV = 2 * DIFF_DK

GQA_HEADS = 8
GQA_KV_HEADS = 2
GQA_DH = 64

MLA_HEADS = 8
MLA_NOPE = 64
MLA_ROPE = 32
MLA_DV = 64
MLA_Q_RANK = 384
MLA_KV_RANK = 256

N_BRANCH = 4
BRANCH_W = 512

PEER_HEADS = 8
PEER_NKEYS = 128
PEER_EXPERTS = PEER_NKEYS * PEER_NKEYS
PEER_DK = 128
PEER_TOPK = 16
PEER_CHUNK = 128

IN_SPLIT = (D_RNN, D_RNN,
            DIFF_HEADS * 2 * DIFF_DK, DIFF_HEADS * 2 * DIFF_DK, DIFF_HEADS * DIFF_DV,
            GQA_HEADS * GQA_DH, GQA_KV_HEADS * GQA_DH, GQA_KV_HEADS * GQA_DH,
            MLA_Q_RANK, MLA_KV_RANK, MLA_ROPE,
            N_BRANCH * D_MODEL)
IN_COLS = sum(IN_SPLIT)

kernel_name = "hybrid_lru_diffattn_gqa_mla_peer_dit"


def rmsnorm(x, g):
    xf = x.astype(jnp.float32)
    y = xf * lax.rsqrt(jnp.mean(xf * xf, axis=-1, keepdims=True) + EPS)
    return (y * g.astype(jnp.float32)).astype(x.dtype)


def adaln(x, g, shift, scale):
    return rmsnorm(x, g) * (1.0 + scale) + shift


def split_cols(z):
    out, o = [], 0
    for w in IN_SPLIT:
        out.append(z[..., o:o + w])
        o += w
    return out


def heads(z, n, dh):
    b, s, _ = z.shape
    return z.reshape(b, s, n, dh).transpose(0, 2, 1, 3)


def merge_heads(o):
    b, h, s, d = o.shape
    return o.transpose(0, 2, 1, 3).reshape(b, s, h * d)


def axial_rope(rows, rope_dim):
    n_freq = rope_dim // 4
    inv_freq = ROPE_THETA ** (-jnp.arange(n_freq, dtype=jnp.float32) / n_freq)
    r = jnp.repeat(jnp.arange(rows, dtype=jnp.float32), GRID_W)
    col = jnp.tile(jnp.arange(GRID_W, dtype=jnp.float32), rows)
    ang = jnp.concatenate([r[:, None] * inv_freq, col[:, None] * inv_freq], axis=-1)
    return jnp.cos(ang), jnp.sin(ang)


def apply_rope(x, cos, sin):
    half = x.shape[-1] // 2
    xf = x.astype(jnp.float32)
    x1, x2 = xf[..., :half], xf[..., half:]
    return jnp.concatenate([x1 * cos - x2 * sin, x1 * sin + x2 * cos], axis=-1).astype(x.dtype)


def _probs(q, k, scale):
    s = jnp.einsum("bgrqd,bgkd->bgrqk", q.astype(jnp.float32), k.astype(jnp.float32)) * scale
    return jax.nn.softmax(s, axis=-1)


def attend(q, k, v, scale):
    p = _probs(q, k, scale)
    return jnp.einsum("bgrqk,bgkd->bgrqd", p, v.astype(jnp.float32)).astype(v.dtype)


def diff_attend(q1, q2, k1, k2, v, lam, scale):
    p = _probs(q1, k1, scale) - lam * _probs(q2, k2, scale)
    return jnp.einsum("bgrqk,bgkd->bgrqd", p, v.astype(jnp.float32)).astype(v.dtype)


def sweep_query_blocks(fn, *qs):
    n = qs[0].shape[-2]
    nb = n // Q_BLOCK

    def to_blocks(a):
        a = a.reshape(a.shape[:-2] + (nb, Q_BLOCK, a.shape[-1]))
        return jnp.moveaxis(a, -3, 0)

    out = lax.map(lambda blk: fn(*blk), tuple(to_blocks(a) for a in qs))
    out = jnp.moveaxis(out, 0, -3)
    return out.reshape(out.shape[:-3] + (n, out.shape[-1]))


def centred_dwconv(x, w, b):
    pad_l = (CONV_W - 1) // 2
    pad_r = CONV_W - 1 - pad_l
    y = lax.conv_general_dilated(x, w[:, None, :].astype(x.dtype), (1,), [(pad_l, pad_r)],
                                 dimension_numbers=("NWC", "WIO", "NWC"),
                                 feature_group_count=x.shape[-1])
    return y + b


def rglru_coeffs(u, w_a, b_a, w_i, b_i, lam):
    uf = u.astype(jnp.float32)
    ub = uf.reshape(uf.shape[:-1] + (RNN_BLOCKS, RNN_BW))

    def gate(w, b):
        z = jnp.einsum("bsnc,ncd->bsnd", ub, w.astype(jnp.float32)).reshape(uf.shape)
        return jax.nn.sigmoid(z + b.astype(jnp.float32))

    r = gate(w_a, b_a)
    i = gate(w_i, b_i)
    log_a = -LRU_C * r * jax.nn.softplus(-lam.astype(jnp.float32))
    return jnp.exp(log_a), jnp.sqrt(-jnp.expm1(2.0 * log_a)) * (i * uf)


def linear_scan(a, b, h0, reverse):
    edge = -1 if reverse else 0
    b = b.at[:, edge].add(a[:, edge] * h0)

    def combine(e1, e2):
        a1, b1 = e1
        a2, b2 = e2
        return a1 * a2, a2 * b1 + b2

    _, h = lax.associative_scan(combine, (a, b), reverse=reverse, axis=1)
    return h


def rglru_mixer(xa, ya, xac, yac, p, need_ctx):
    u = centred_dwconv(xa, p["conv_w"], p["conv_b"])
    uc = centred_dwconv(xac, p["conv_w"], p["conv_b"])
    hs, hcs = [], []
    for d, rev in enumerate((False, True)):
        args = (p["lru_wa"][d], p["lru_ba"][d], p["lru_wi"][d], p["lru_bi"][d], p["lru_lambda"][d])
        ac, bc = rglru_coeffs(uc, *args)
        hc = linear_scan(ac, bc, jnp.zeros_like(bc[:, 0]), rev)
        h_ctx_final = hc[:, 0] if rev else hc[:, -1]
        a, b = rglru_coeffs(u, *args)
        hs.append(linear_scan(a, b, h_ctx_final, rev))
        hcs.append(hc)
    out = (jax.nn.gelu(ya.astype(jnp.float32)) * (hs[0] + hs[1])).astype(xa.dtype)
    out_c = None
    if need_ctx:
        out_c = (jax.nn.gelu(yac.astype(jnp.float32)) * (hcs[0] + hcs[1])).astype(xac.dtype)
    return out, out_c


def diff_mixer(q, k, v, qc, kc, vc, p, rope, lam_init, need_ctx):
    lv = p["diff_lam"].astype(jnp.float32)
    lam = jnp.exp(jnp.sum(lv[0] * lv[1])) - jnp.exp(jnp.sum(lv[2] * lv[3])) + lam_init
    scale = DIFF_DK ** -0.5

    def split_qk(z):
        zh = heads(z, DIFF_HEADS, 2 * DIFF_DK)
        return zh[..., :DIFF_DK], zh[..., DIFF_DK:]

    def finish(o):
        return merge_heads(rmsnorm(o, p["diff_subln_g"]) * (1.0 - lam_init))

    k1c, k2c = split_qk(kc)
    vhc = heads(vc, DIFF_HEADS, DIFF_DV)
    q1, q2 = (apply_rope(t, *rope) for t in split_qk(q))
    k1, k2 = (apply_rope(t, *rope) for t in split_qk(k))
    kk1 = jnp.concatenate([k1c, k1], axis=2)
    kk2 = jnp.concatenate([k2c, k2], axis=2)
    vv = jnp.concatenate([vhc, heads(v, DIFF_HEADS, DIFF_DV)], axis=2)
    o = sweep_query_blocks(lambda a, b: diff_attend(a, b, kk1, kk2, vv, lam, scale),
                           q1[:, :, None], q2[:, :, None])
    out = finish(o[:, :, 0])
    out_c = None
    if need_ctx:
        q1c, q2c = split_qk(qc)
        out_c = finish(diff_attend(q1c[:, :, None], q2c[:, :, None], k1c, k2c, vhc, lam, scale)[:, :, 0])
    return out, out_c


def gqa_mixer(q, k, v, qc, kc, vc, p, rope, need_ctx):
    rep = GQA_HEADS // GQA_KV_HEADS
    scale = GQA_DH ** -0.5

    def q_heads(z):
        t = rmsnorm(heads(z, GQA_HEADS, GQA_DH), p["gqa_qnorm_g"])
        return t.reshape(t.shape[0], GQA_KV_HEADS, rep, t.shape[2], GQA_DH)

    def k_heads(z):
        return rmsnorm(heads(z, GQA_KV_HEADS, GQA_DH), p["gqa_knorm_g"])

    def finish(o):
        return merge_heads(o.reshape(o.shape[0], GQA_HEADS, o.shape[3], GQA_DH))

    khc = k_heads(kc)
    vhc = heads(vc, GQA_KV_HEADS, GQA_DH)
    kk = jnp.concatenate([khc, apply_rope(k_heads(k), *rope)], axis=2)
    vv = jnp.concatenate([vhc, heads(v, GQA_KV_HEADS, GQA_DH)], axis=2)
    ql = apply_rope(q_heads(q), *rope)
    out = finish(sweep_query_blocks(lambda a: attend(a, kk, vv, scale), ql))
    out_c = finish(attend(q_heads(qc), khc, vhc, scale)) if need_ctx else None
    return out, out_c


def mla_mixer(cq, ckv, kr, cqc, ckvc, krc, p, rope, need_ctx):
    scale = (MLA_NOPE + MLA_ROPE) ** -0.5

    def q_of(z, rot):
        qf = heads(rmsnorm(z, p["mla_qnorm_g"]) @ p["mla_w_uq"], MLA_HEADS, MLA_NOPE + MLA_ROPE)
        q_nope, q_rope = qf[..., :MLA_NOPE], qf[..., MLA_NOPE:]
        if rot:
            q_rope = apply_rope(q_rope, *rope)
        return jnp.concatenate([q_nope, q_rope], axis=-1)[:, :, None]

    def kv_of(zc, zr, rot):
        kvf = heads(rmsnorm(zc, p["mla_kvnorm_g"]) @ p["mla_w_ukv"], MLA_HEADS, MLA_NOPE + MLA_DV)
        k_nope, vh = kvf[..., :MLA_NOPE], kvf[..., MLA_NOPE:]
        k_rope = zr[:, None]
        if rot:
            k_rope = apply_rope(k_rope, *rope)
        k_rope = jnp.broadcast_to(k_rope, k_nope.shape[:-1] + (MLA_ROPE,))
        return jnp.concatenate([k_nope, k_rope], axis=-1), vh

    kc_, vc_ = kv_of(ckvc, krc, False)
    kl, vl = kv_of(ckv, kr, True)
    kk = jnp.concatenate([kc_, kl], axis=2)
    vv = jnp.concatenate([vc_, vl], axis=2)
    out = merge_heads(sweep_query_blocks(lambda a: attend(a, kk, vv, scale), q_of(cq, True))[:, :, 0])
    out_c = merge_heads(attend(q_of(cqc, False), kc_, vc_, scale)[:, :, 0]) if need_ctx else None
    return out, out_c


def merge_branches(outs, zg, w_branch, w_out):
    o = jnp.stack(outs, axis=-2)
    gates = jax.nn.sigmoid(zg.reshape(zg.shape[:-1] + (N_BRANCH, D_MODEL)))
    merged = jnp.sum(jnp.einsum("bskc,kcd->bskd", o, w_branch) * gates, axis=-2)
    return merged @ w_out


def token_mixers(h, hc, p, rope_diff, rope_gqa, rope_mla, lam_init, need_ctx):
    (xa, ya, qb, kb, vb, qg, kg, vg, cq, ckv, kr, zg) = split_cols(h @ p["w_in"])
    (xac, yac, qbc, kbc, vbc, qgc, kgc, vgc, cqc, ckvc, krc, zgc) = split_cols(hc @ p["w_in"])
    oa, oa_c = rglru_mixer(xa, ya, xac, yac, p, need_ctx)
    ob, ob_c = diff_mixer(qb, kb, vb, qbc, kbc, vbc, p, rope_diff, lam_init, need_ctx)
    og, og_c = gqa_mixer(qg, kg, vg, qgc, kgc, vgc, p, rope_gqa, need_ctx)
    om, om_c = mla_mixer(cq, ckv, kr, cqc, ckvc, krc, p, rope_mla, need_ctx)
    out = merge_branches((oa, ob, og, om), zg, p["w_branch"], p["w_out"])
    out_c = None
    if need_ctx:
        out_c = merge_branches((oa_c, ob_c, og_c, om_c), zgc, p["w_branch"], p["w_out"])
    return out, out_c


def peer_ffn(x, w_q, keys, u_tab, v_tab):
    shape = x.shape
    xt = x.reshape(-1, PEER_CHUNK, shape[-1])
    kf = keys.astype(jnp.float32)

    def chunk(t):
        q = (t @ w_q).astype(jnp.float32).reshape(PEER_CHUNK, PEER_HEADS, 2, PEER_DK // 2)
        s = jnp.einsum("thpc,hpnc->thpn", q, kf)
        sv, si = lax.top_k(s, PEER_TOPK)
        cand = (sv[:, :, 0, :, None] + sv[:, :, 1, None, :]).reshape(PEER_CHUNK, PEER_HEADS, -1)
        cidx = (si[:, :, 0, :, None] * PEER_NKEYS + si[:, :, 1, None, :]).reshape(PEER_CHUNK, PEER_HEADS, -1)
        best, pos = lax.top_k(cand, PEER_TOPK)
        eidx = jnp.take_along_axis(cidx, pos, axis=-1)
        g = jax.nn.softmax(best, axis=-1)
        act = jax.nn.gelu(jnp.einsum("thkd,td->thk", jnp.take(u_tab, eidx, axis=0), t).astype(jnp.float32))
        w = (g * act).astype(t.dtype)
        return jnp.einsum("thk,thkd->td", w, jnp.take(v_tab, eidx, axis=0))

    return lax.map(chunk, xt).reshape(shape)


def setup_inputs(seed: int = 0) -> dict:
    key = jax.random.key(seed)
    ks = jax.random.split(key, 40)
    L, D = DEPTH, D_MODEL

    def nrm(i, shape, scale):
        return jax.random.normal(ks[i], shape, jnp.float32) * scale

    def gain(i, shape):
        return 1.0 + nrm(i, shape, 0.02)

    u = jax.random.uniform(ks[39], (L, 2, D_RNN), jnp.float32, 0.9, 0.999)
    a_base = u ** (1.0 / LRU_C)
    lru_lambda = jnp.log(a_base) - jnp.log1p(-a_base)
    return {
        "x": nrm(0, (BATCH, SEQ, D), 1.0),
        "c": nrm(1, (BATCH, D), 1.0),
        "ctx": nrm(2, (BATCH, CTX_LEN, D), 1.0),
        "c_ctx": nrm(3, (D,), 1.0),
        "w_mod": nrm(4, (L, D, N_MOD * D), 0.5 * D ** -0.5),
        "b_mod": nrm(5, (L, N_MOD * D), 0.02),
        "norm1_g": gain(6, (L, D)),
        "norm2_g": gain(7, (L, D)),
        "w_in": nrm(8, (L, D, IN_COLS), D ** -0.5),
        "conv_w": nrm(9, (L, CONV_W, D_RNN), CONV_W ** -0.5),
        "conv_b": nrm(10, (L, D_RNN), 0.02),
        "lru_wa": nrm(11, (L, 2, RNN_BLOCKS, RNN_BW, RNN_BW), RNN_BW ** -0.5),
        "lru_ba": nrm(12, (L, 2, D_RNN), 0.02),
        "lru_wi": nrm(13, (L, 2, RNN_BLOCKS, RNN_BW, RNN_BW), RNN_BW ** -0.5),
        "lru_bi": nrm(14, (L, 2, D_RNN), 0.02),
        "lru_lambda": lru_lambda,
        "diff_lam": nrm(15, (L, 4, DIFF_DK), 0.1),
        "diff_subln_g": gain(16, (L, DIFF_DV)),
        "gqa_qnorm_g": gain(17, (L, GQA_DH)),
        "gqa_knorm_g": gain(18, (L, GQA_DH)),
        "mla_qnorm_g": gain(19, (L, MLA_Q_RANK)),
        "mla_w_uq": nrm(20, (L, MLA_Q_RANK, MLA_HEADS * (MLA_NOPE + MLA_ROPE)), MLA_Q_RANK ** -0.5),
        "mla_kvnorm_g": gain(21, (L, MLA_KV_RANK)),
        "mla_w_ukv": nrm(22, (L, MLA_KV_RANK, MLA_HEADS * (MLA_NOPE + MLA_DV)), MLA_KV_RANK ** -0.5),
        "w_branch": nrm(23, (L, N_BRANCH, BRANCH_W, D), BRANCH_W ** -0.5),
        "w_out": nrm(24, (L, D, D), D ** -0.5),
        "peer_wq": nrm(25, (L, D, PEER_HEADS * PEER_DK), D ** -0.5),
        "peer_keys": nrm(26, (L, PEER_HEADS, 2, PEER_NKEYS, PEER_DK // 2), (PEER_DK // 2) ** -0.5),
        "peer_u": nrm(27, (L, PEER_EXPERTS, D), D ** -0.5),
        "peer_v": nrm(28, (L, PEER_EXPERTS, D), PEER_HEADS ** -0.5),
        "final_norm_g": gain(29, (D,)),
    }


def reference(x, c, ctx, c_ctx, w_mod, b_mod, norm1_g, norm2_g, w_in, conv_w, conv_b,
              lru_wa, lru_ba, lru_wi, lru_bi, lru_lambda, diff_lam, diff_subln_g,
              gqa_qnorm_g, gqa_knorm_g, mla_qnorm_g, mla_w_uq, mla_kvnorm_g, mla_w_ukv,
              w_branch, w_out, peer_wq, peer_keys, peer_u, peer_v, final_norm_g):
    rows = x.shape[1] // GRID_W
    rope_diff = axial_rope(rows, DIFF_DK)
    rope_gqa = axial_rope(rows, GQA_DH)
    rope_mla = axial_rope(rows, MLA_ROPE)
    sc = jax.nn.silu(c)
    scc = jax.nn.silu(c_ctx)
    xc = ctx
    for l in range(DEPTH):
        need_ctx = l < DEPTH - 1
        lam_init = 0.8 - 0.6 * math.exp(-0.3 * l)
        p = {
            "w_in": w_in[l], "conv_w": conv_w[l], "conv_b": conv_b[l],
            "lru_wa": lru_wa[l], "lru_ba": lru_ba[l], "lru_wi": lru_wi[l], "lru_bi": lru_bi[l],
            "lru_lambda": lru_lambda[l], "diff_lam": diff_lam[l], "diff_subln_g": diff_subln_g[l],
            "gqa_qnorm_g": gqa_qnorm_g[l], "gqa_knorm_g": gqa_knorm_g[l],
            "mla_qnorm_g": mla_qnorm_g[l], "mla_w_uq": mla_w_uq[l],
            "mla_kvnorm_g": mla_kvnorm_g[l], "mla_w_ukv": mla_w_ukv[l],
            "w_branch": w_branch[l], "w_out": w_out[l],
        }
        mod = (sc @ w_mod[l] + b_mod[l]).reshape(-1, N_MOD, 1, D_MODEL)
        modc = (scc @ w_mod[l] + b_mod[l]).reshape(N_MOD, D_MODEL)
        h = adaln(x, norm1_g[l], mod[:, 0], mod[:, 1])
        hc = adaln(xc, norm1_g[l], modc[0], modc[1])
        mix, mix_c = token_mixers(h, hc, p, rope_diff, rope_gqa, rope_mla, lam_init, need_ctx)
        x = x + mod[:, 2] * mix
        h = adaln(x, norm2_g[l], mod[:, 3], mod[:, 4])
        x = x + mod[:, 5] * peer_ffn(h, peer_wq[l], peer_keys[l], peer_u[l], peer_v[l])
        if need_ctx:
            xc = xc + modc[2] * mix_c
            hc = adaln(xc, norm2_g[l], modc[3], modc[4])
            xc = xc + modc[5] * peer_ffn(hc, peer_wq[l], peer_keys[l], peer_u[l], peer_v[l])
    return rmsnorm(x, final_norm_g)
```

```python
import functools
import math

import jax
import jax.numpy as jnp
from jax import lax
from jax.experimental import pallas as pl
from jax.experimental.pallas import tpu as pltpu

F32 = jnp.float32
BF16 = jnp.bfloat16

EPS = 1e-6
GRID_W = 64
ROPE_THETA = 10000.0
N_MOD = 6

D_RNN = 512
RNN_BLOCKS = 8
CONV_W = 4
LRU_C = 8.0

DIFF_HEADS = 4
DIFF_DK = 64
DIFF_DV = 128
GQA_HEADS = 8
GQA_KV_HEADS = 2
GQA_DH = 64
MLA_HEADS = 8
MLA_NOPE = 64
MLA_ROPE = 32
MLA_DV = 64
MLA_Q_RANK = 384
MLA_KV_RANK = 256
N_BRANCH = 4
BRANCH_W = 512

PEER_HEADS = 8
PEER_NKEYS = 128
PEER_DK = 128
PEER_TOPK = 16

V7X_LANES = 128
V7X_SUBLANES = 8
V7X_VMEM_BYTES = 64 * 1024 * 1024
VMEM_LIMIT = 48 * 1024 * 1024

ROW_TILE = 256


def _cparams(sem):
    return pltpu.CompilerParams(dimension_semantics=sem, vmem_limit_bytes=VMEM_LIMIT)


def _adaln_body(x_ref, g_ref, sh_ref, sc_ref, o_ref):
    x = x_ref[0]
    ms = jnp.mean(x * x, axis=-1, keepdims=True)
    y = x * lax.rsqrt(ms + EPS) * g_ref[...]
    o_ref[0] = (y * (1.0 + sc_ref[0, 0]) + sh_ref[0, 0]).astype(o_ref.dtype)


def adaln(x, g, shift, scale, t0):
    b, s, d = x.shape
    nt = s // ROW_TILE - t0
    seg = lambda bi, ti: (bi, jnp.minimum(ti + t0, 1), 0, 0)
    return pl.pallas_call(
        _adaln_body,
        grid=(b, nt),
        in_specs=[
            pl.BlockSpec((1, ROW_TILE, d), lambda bi, ti: (bi, ti + t0, 0)),
            pl.BlockSpec((1, d), lambda bi, ti: (0, 0)),
            pl.BlockSpec((1, 1, 1, d), seg),
            pl.BlockSpec((1, 1, 1, d), seg),
        ],
        out_specs=pl.BlockSpec((1, ROW_TILE, d), lambda bi, ti: (bi, ti, 0)),
        out_shape=jax.ShapeDtypeStruct((b, nt * ROW_TILE, d), BF16),
        compiler_params=_cparams(("parallel", "parallel")),
        name="adaln",
    )(x, g, shift, scale)


def _mm_body(a_ref, w_ref, o_ref):
    o_ref[...] = jnp.dot(a_ref[...], w_ref[...], preferred_element_type=F32).astype(o_ref.dtype)


def matmul(a, w, out_dtype=F32, tm=512, tn=512):
    m, k = a.shape
    _, n = w.shape
    tm = min(tm, m)
    tn = min(tn, n)
    assert m % tm == 0 and n % tn == 0
    return pl.pallas_call(
        _mm_body,
        grid=(n // tn, m // tm),
        in_specs=[
            pl.BlockSpec((tm, k), lambda j, i: (i, 0)),
            pl.BlockSpec((k, tn), lambda j, i: (0, j)),
        ],
        out_specs=pl.BlockSpec((tm, tn), lambda j, i: (i, j)),
        out_shape=jax.ShapeDtypeStruct((m, n), out_dtype),
        compiler_params=_cparams(("parallel", "parallel")),
        name="matmul",
    )(a, w)


def _gelu(x):
    c = math.sqrt(2.0 / math.pi)
    return x * (0.5 * (1.0 + jnp.tanh(c * (x + 0.044715 * (x * x * x)))))


def _sigmoid(x):
    return 1.0 / (1.0 + jnp.exp(-x))


TILE_ROWS = V7X_SUBLANES


def _lru_body(x_ref, xp_ref, xn_ref, cw_ref, cb_ref, wg_ref, bg_ref, nc_ref, *rest,
              reverse, n_chunks):
    if reverse:
        ya_ref, hf_ref, o_ref, carry_ref = rest
    else:
        o_ref, carry_ref = rest
    step = pl.program_id(1)
    if reverse:
        chunk = jnp.where(step == 0, 0, n_chunks - step)
    else:
        chunk = step
    rows = ROW_TILE
    c = x_ref.shape[-1]

    @pl.when(step == 0)
    def _():
        carry_ref[...] = jnp.zeros_like(carry_ref)

    x = x_ref[0]
    xx = jnp.concatenate([xp_ref[0], x, xn_ref[0]], axis=0)
    n_xx = rows + 2 * TILE_ROWS
    row = lax.broadcasted_iota(jnp.int32, (rows, c), 0)
    seg_start = jnp.logical_or(chunk == 0, chunk == 1)
    seg_end = jnp.logical_or(chunk == 0, chunk == n_chunks - 1)

    def shifted(d):
        return pltpu.roll(xx, (n_xx - d) % n_xx, axis=0)[TILE_ROWS:TILE_ROWS + rows]

    x_m1 = jnp.where(jnp.logical_and(seg_start, row == 0), 0.0, shifted(-1))
    x_p1 = jnp.where(jnp.logical_and(seg_end, row >= rows - 1), 0.0, shifted(1))
    x_p2 = jnp.where(jnp.logical_and(seg_end, row >= rows - 2), 0.0, shifted(2))
    cw = cw_ref[...]
    u = cw[0:1] * x_m1 + cw[1:2] * x + cw[2:3] * x_p1 + cw[3:4] * x_p2 + cb_ref[...]

    z = jnp.dot(u.astype(BF16), wg_ref[0], preferred_element_type=F32) + bg_ref[0]
    r = _sigmoid(z[:, :c])
    gi = _sigmoid(z[:, c:])
    a = jnp.exp(nc_ref[0] * r)
    bv = jnp.sqrt(1.0 - a * a) * (gi * u)

    sub = row % TILE_ROWS
    for dstep in (1, 2, 4):
        if reverse:
            sh = (rows - dstep) % rows
            keep = sub <= TILE_ROWS - 1 - dstep
        else:
            sh = dstep
            keep = sub >= dstep
        a_sh = jnp.where(keep, pltpu.roll(a, sh, axis=0), 1.0)
        b_sh = jnp.where(keep, pltpu.roll(bv, sh, axis=0), 0.0)
        bv = a * b_sh + bv
        a = a * a_sh

    carry = carry_ref[...]
    n_tiles = rows // TILE_ROWS
    order = range(n_tiles - 1, -1, -1) if reverse else range(n_tiles)
    hs = [None] * n_tiles
    for t in order:
        sl = slice(t * TILE_ROWS, (t + 1) * TILE_ROWS)
        h = bv[sl] + a[sl] * carry
        hs[t] = h
        edge = h[0:1] if reverse else h[TILE_ROWS - 1:TILE_ROWS]
        carry = jnp.broadcast_to(edge, (TILE_ROWS, c))
    carry_ref[...] = carry
    h_all = jnp.concatenate(hs, axis=0)
    if reverse:
        o_ref[0] = (_gelu(ya_ref[0]) * (hf_ref[0] + h_all)).astype(o_ref.dtype)
    else:
        o_ref[0] = h_all


def lru_scan(z, xa_col, ya_col, conv_w, conv_b, wg, bg, negc, hf, *, reverse):
    b, s, _ = z.shape
    c = D_RNN
    n_chunks = s // ROW_TILE
    per = ROW_TILE // TILE_ROWS
    n8 = s // TILE_ROWS
    if reverse:
        cidx = lambda st: jnp.where(st == 0, 0, n_chunks - st)
    else:
        cidx = lambda st: st
    in_specs = [
        pl.BlockSpec((1, ROW_TILE, c), lambda bi, st: (bi, cidx(st), xa_col)),
        pl.BlockSpec((1, TILE_ROWS, c), lambda bi, st: (bi, jnp.maximum(cidx(st) * per - 1, 0), xa_col)),
        pl.BlockSpec((1, TILE_ROWS, c), lambda bi, st: (bi, jnp.minimum((cidx(st) + 1) * per, n8 - 1), xa_col)),
        pl.BlockSpec((CONV_W, c), lambda bi, st: (0, 0)),
        pl.BlockSpec((1, c), lambda bi, st: (0, 0)),
        pl.BlockSpec((1, c, 2 * c), lambda bi, st: (0, 0, 0)),
        pl.BlockSpec((1, 1, 2 * c), lambda bi, st: (0, 0, 0)),
        pl.BlockSpec((1, 1, c), lambda bi, st: (0, 0, 0)),
    ]
    args = [z, z, z, conv_w, conv_b, wg, bg, negc]
    if reverse:
        in_specs += [
            pl.BlockSpec((1, ROW_TILE, c), lambda bi, st: (bi, cidx(st), ya_col)),
            pl.BlockSpec((1, ROW_TILE, c), lambda bi, st: (bi, cidx(st), 0)),
        ]
        args += [z, hf]
    return pl.pallas_call(
        functools.partial(_lru_body, reverse=reverse, n_chunks=n_chunks),
        grid=(b, n_chunks),
        in_specs=in_specs,
        out_specs=pl.BlockSpec((1, ROW_TILE, c), lambda bi, st: (bi, cidx(st), 0)),
        out_shape=jax.ShapeDtypeStruct((b, s, c), BF16 if reverse else F32),
        scratch_shapes=[pltpu.VMEM((TILE_ROWS, c), F32)],
        compiler_params=_cparams(("parallel", "arbitrary")),
        name="lru_rev" if reverse else "lru_fwd",
    )(*args)


def _tile_lanes(t, n):
    return jnp.concatenate([t] * n, axis=1) if n > 1 else t


def _rope(x, c, s1, s2, r):
    w = x.shape[-1]
    n = w // V7X_LANES
    return (x * _tile_lanes(c, n) + pltpu.roll(x, w - r, axis=1) * _tile_lanes(s1, n)
            + pltpu.roll(x, r, axis=1) * _tile_lanes(s2, n))


def _split_dot(x, m_ref):
    hi = x.astype(BF16)
    lo = (x - hi.astype(F32)).astype(BF16)
    m = m_ref[...]
    return jnp.dot(hi, m, preferred_element_type=F32) + jnp.dot(lo, m, preferred_element_type=F32)


def _rms(x, g):
    return x * lax.rsqrt(jnp.mean(x * x, axis=-1, keepdims=True) + EPS) * g


def _prep_body(qb_ref, kb_ref, vb_ref, qg_ref, tail_ref,
               c64_ref, s164_ref, s264_ref, cm_ref, s1m_ref, s2m_ref,
               avg_ref, gq_ref, gk_ref, gcq_ref, gckv_ref, wuq_ref, wuk_ref, wuv_ref,
               qd_ref, kd_ref, vd_ref, qq_ref, kq_ref, vq_ref, qm_ref, km_ref, vm_ref):
    c64, s164, s264 = c64_ref[...], s164_ref[...], s264_ref[...]
    cm, s1m, s2m = cm_ref[...], s1m_ref[...], s2m_ref[...]
    half64 = DIFF_DK // 2
    qd_ref[0] = (_rope(qb_ref[0], c64, s164, s264, half64) * (DIFF_DK ** -0.5)).astype(BF16)
    kd_ref[0] = _rope(kb_ref[0], c64, s164, s264, half64).astype(BF16)
    vd_ref[0] = vb_ref[0].astype(BF16)
    qg = qg_ref[0]
    ms = _split_dot(qg * qg, avg_ref)
    qn = qg * lax.rsqrt(ms + EPS) * gq_ref[...]
    qq_ref[0] = (_rope(qn, c64, s164, s264, GQA_DH // 2) * (GQA_DH ** -0.5)).astype(BF16)
    tail = tail_ref[0]
    kg = tail[:, 0:128]
    msk = _split_dot(kg * kg, avg_ref.at[0:128, 0:128])
    kn = _rope(kg * lax.rsqrt(msk + EPS) * gk_ref[...], c64, s164, s264, GQA_DH // 2).astype(BF16)
    vg = tail[:, 128:256].astype(BF16)
    kq_ref[0] = jnp.concatenate([kn[:, 0:64], kn[:, 0:64], kn[:, 64:128], kn[:, 64:128]], axis=1)
    vq_ref[0] = jnp.concatenate([vg[:, 0:64], vg[:, 0:64], vg[:, 64:128], vg[:, 64:128]], axis=1)
    cq = tail[:, 256:256 + MLA_Q_RANK]
    ckv = tail[:, 640:640 + MLA_KV_RANK]
    kr = tail[:, 896:1024]
    qf = jnp.dot(_rms(cq, gcq_ref[...]).astype(BF16), wuq_ref[...], preferred_element_type=F32)
    scale = (MLA_NOPE + MLA_ROPE) ** -0.5
    qm_ref[0] = (_rope(qf, cm, s1m, s2m, MLA_ROPE // 2) * scale).astype(BF16)
    ckvn = _rms(ckv, gckv_ref[...]).astype(BF16)
    kf = jnp.dot(ckvn, wuk_ref[...], preferred_element_type=F32)
    krr = _rope(kr, cm, s1m, s2m, MLA_ROPE // 2)
    km_ref[0] = (kf + _tile_lanes(krr, MLA_HEADS)).astype(BF16)
    vm_ref[0] = jnp.dot(ckvn, wuv_ref[...], preferred_element_type=F32).astype(BF16)


def attn_prep(z, tabs, avg, gq, gk, gcq, gckv, wuq, wuk, wuv):
    b, s, _ = z.shape
    nt = s // ROW_TILE
    zspec = lambda w, idx: pl.BlockSpec((1, ROW_TILE, w), lambda ti, bi: (bi, ti, idx))
    tab = pl.BlockSpec((ROW_TILE, V7X_LANES), lambda ti, bi: (ti, 0))
    full = lambda a: pl.BlockSpec(a.shape, lambda ti, bi: (0,) * a.ndim)
    ospec = lambda w: pl.BlockSpec((1, ROW_TILE, w), lambda ti, bi: (bi, ti, 0))
    oshape = lambda w: jax.ShapeDtypeStruct((b, s, w), BF16)
    widths = (512, 512, 512, 512, 256, 256, 1024, 1024, 512)
    return pl.pallas_call(
        _prep_body,
        grid=(nt, b),
        in_specs=[zspec(512, 10), zspec(512, 11), zspec(512, 12), zspec(512, 13), zspec(1024, 7)]
        + [tab] * 6 + [full(a) for a in (avg, gq, gk, gcq, gckv, wuq, wuk, wuv)],
        out_specs=[ospec(w) for w in widths],
        out_shape=[oshape(w) for w in widths],
        compiler_params=_cparams(("parallel", "parallel")),
        name="attn_prep",
    )(z, z, z, z, z, *tabs, avg, gq, gk, gcq, gckv, wuq, wuk, wuv)


def _attn_body(lam_ref, q_ref, k_ref, v_ref, g_ref, o_ref, *, mode, out_scale):
    ti = pl.program_id(2)
    lane = lax.broadcasted_iota(jnp.int32, (ROW_TILE, V7X_LANES), 1)
    lo_half = lane < (V7X_LANES // 2)

    def run(n_keys):
        q = q_ref[0]
        outs = []
        for j in range(2):
            if mode == "mla":
                qj = q[:, j * V7X_LANES:(j + 1) * V7X_LANES]
                kj = k_ref[0, 0:n_keys, j * V7X_LANES:(j + 1) * V7X_LANES]
            else:
                keep = lo_half if j == 0 else jnp.logical_not(lo_half)
                qj = jnp.where(keep, q.astype(F32), 0.0).astype(BF16)
                kj = k_ref[0, 0:n_keys, :]
            s = lax.dot_general(qj, kj, (((1,), (1,)), ((), ())), preferred_element_type=F32)
            m = jnp.max(s, axis=-1, keepdims=True)
            p = jnp.exp(s - m)
            l = jnp.sum(p, axis=-1, keepdims=True)
            o = jnp.dot(p.astype(BF16), v_ref[0, 0:n_keys, :], preferred_element_type=F32)
            outs.append(o / l)
        if mode == "diff":
            o = outs[0] - lam_ref[0] * outs[1]
            o = o * lax.rsqrt(jnp.mean(o * o, axis=-1, keepdims=True) + EPS) * g_ref[...] * out_scale
        else:
            o = jnp.where(lo_half, outs[0], outs[1])
        o_ref[0] = o.astype(o_ref.dtype)

    @pl.when(ti == 0)
    def _():
        run(ROW_TILE)

    @pl.when(ti > 0)
    def _():
        run(k_ref.shape[1])


def attention(q, k, v, lam, g, *, mode, out_scale=1.0):
    b, s, wq = q.shape
    groups = 4
    qw = wq // groups
    kw = k.shape[-1] // (2 if mode == "pair" else groups)
    kidx = (lambda gi: gi // 2) if mode == "pair" else (lambda gi: gi)
    nt = s // ROW_TILE
    return pl.pallas_call(
        functools.partial(_attn_body, mode=mode, out_scale=out_scale),
        grid=(b, groups, nt),
        in_specs=[
            pl.BlockSpec(memory_space=pltpu.SMEM),
            pl.BlockSpec((1, ROW_TILE, qw), lambda bi, gi, ti: (bi, ti, gi)),
            pl.BlockSpec((1, s, kw), lambda bi, gi, ti: (bi, 0, kidx(gi))),
            pl.BlockSpec((1, s, V7X_LANES), lambda bi, gi, ti: (bi, 0, kidx(gi))),
            pl.BlockSpec((1, V7X_LANES), lambda bi, gi, ti: (0, 0)),
        ],
        out_specs=pl.BlockSpec((1, ROW_TILE, V7X_LANES), lambda bi, gi, ti: (bi, ti, gi)),
        out_shape=jax.ShapeDtypeStruct((b, s, groups * V7X_LANES), BF16),
        compiler_params=_cparams(("parallel", "parallel", "arbitrary")),
        name="attn_" + mode,
    )(lam, q, k, v, g)


def _merge_body(oa_ref, ob_ref, og_ref, om_ref, zg_ref, wb_ref, wo_ref, x_ref, gate_ref, o_ref):
    d = x_ref.shape[-1]
    merged = None
    for k, o_k in enumerate((oa_ref, ob_ref, og_ref, om_ref)):
        t = jnp.dot(o_k[0], wb_ref[k], preferred_element_type=F32)
        t = t * _sigmoid(zg_ref[0, :, k * d:(k + 1) * d])
        merged = t if merged is None else merged + t
    y = jnp.dot(merged.astype(BF16), wo_ref[...], preferred_element_type=F32)
    o_ref[0] = x_ref[0] + gate_ref[0, 0] * y


def merge(oa, ob, og, om, z, wb, wo, x, gate):
    b, s, d = x.shape
    nt = s // ROW_TILE
    bspec = pl.BlockSpec((1, ROW_TILE, BRANCH_W), lambda bi, ti: (bi, ti, 0))
    return pl.pallas_call(
        _merge_body,
        grid=(b, nt),
        in_specs=[bspec, bspec, bspec, bspec,
                  pl.BlockSpec((1, ROW_TILE, N_BRANCH * d), lambda bi, ti: (bi, ti, 0)),
                  pl.BlockSpec(wb.shape, lambda bi, ti: (0, 0, 0)),
                  pl.BlockSpec(wo.shape, lambda bi, ti: (0, 0)),
                  pl.BlockSpec((1, ROW_TILE, d), lambda bi, ti: (bi, ti, 0)),
                  pl.BlockSpec((1, 1, 1, d), lambda bi, ti: (bi, jnp.minimum(ti, 1), 0, 0))],
        out_specs=pl.BlockSpec((1, ROW_TILE, d), lambda bi, ti: (bi, ti, 0)),
        out_shape=jax.ShapeDtypeStruct((b, s, d), F32),
        compiler_params=_cparams(("parallel", "parallel")),
        name="merge",
    )(oa, ob, og, om, z, wb, wo, x, gate)


NEG_BIG = float(jnp.finfo(jnp.float32).min)


def _route_body(x_ref, g_ref, sh_ref, sc_ref, wq_ref, kb_ref,
                h_ref, a_ref, b_ref, e1_ref, e2_ref, st_ref, ta_ref, tb_ref):
    x = x_ref[0]
    ms = jnp.mean(x * x, axis=-1, keepdims=True)
    h = (x * lax.rsqrt(ms + EPS) * g_ref[...] * (1.0 + sc_ref[0, 0]) + sh_ref[0, 0]).astype(BF16)
    h_ref[...] = h
    q = jnp.dot(h, wq_ref[...], preferred_element_type=F32).astype(BF16)
    st_ref[...] = lax.dot_general(kb_ref[...], q, (((1,), (1,)), ((), ())), preferred_element_type=F32)
    nk = PEER_NKEYS

    def top_rows(v, dst_ref):
        for r in range(PEER_TOPK):
            m = jnp.max(v, axis=0, keepdims=True)
            dst_ref[r:r + 1, :] = m
            v = jnp.where(v >= m, NEG_BIG, v)

    def head(hd, carry):
        base = pl.multiple_of(hd * (2 * nk), 2 * nk)
        s1 = st_ref[pl.ds(base, nk), :]
        s2 = st_ref[pl.ds(base + nk, nk), :]
        top_rows(s1, ta_ref)
        top_rows(s2, tb_ref)
        tb = tb_ref[...]
        cand = jnp.concatenate([ta_ref[p:p + 1, :] + tb for p in range(PEER_TOPK)], axis=0)
        top = ta_ref[0:1, :] + tb_ref[0:1, :]
        zsum = jnp.zeros_like(top)
        m = top
        for r in range(PEER_TOPK):
            m = jnp.max(cand, axis=0, keepdims=True)
            zsum = zsum + jnp.exp(m - top)
            cand = jnp.where(cand >= m, NEG_BIG, cand)
        a_ref[hd] = s1 - m
        b_ref[hd] = s2
        e1_ref[hd] = jnp.exp(s1 - ta_ref[0:1, :]) / zsum
        e2_ref[hd] = jnp.exp(s2 - tb_ref[0:1, :])
        return carry

    lax.fori_loop(0, PEER_HEADS, head, 0)


def peer_route(x, g, shift, scale, wq, kbig):
    b, s, d = x.shape
    nt = s // ROW_TILE
    m = b * s
    seg = lambda bi, ti: (bi, jnp.minimum(ti, 1), 0, 0)
    flat = lambda bi, ti: (0, 0, bi * nt + ti)
    kspec = pl.BlockSpec((PEER_HEADS, PEER_NKEYS, ROW_TILE), flat)
    kshape = jax.ShapeDtypeStruct((PEER_HEADS, PEER_NKEYS, m), F32)
    return pl.pallas_call(
        _route_body,
        grid=(b, nt),
        in_specs=[
            pl.BlockSpec((1, ROW_TILE, d), lambda bi, ti: (bi, ti, 0)),
            pl.BlockSpec((1, d), lambda bi, ti: (0, 0)),
            pl.BlockSpec((1, 1, 1, d), seg),
            pl.BlockSpec((1, 1, 1, d), seg),
            pl.BlockSpec(wq.shape, lambda bi, ti: (0, 0)),
            pl.BlockSpec(kbig.shape, lambda bi, ti: (0, 0)),
        ],
        out_specs=[pl.BlockSpec((ROW_TILE, d), lambda bi, ti: (bi * nt + ti, 0)), kspec, kspec, kspec, kspec],
        out_shape=[jax.ShapeDtypeStruct((m, d), BF16), kshape, kshape, kshape, kshape],
        scratch_shapes=[pltpu.VMEM((2 * PEER_HEADS * PEER_NKEYS, ROW_TILE), F32),
                        pltpu.VMEM((PEER_TOPK, ROW_TILE), F32),
                        pltpu.VMEM((PEER_TOPK, ROW_TILE), F32)],
        compiler_params=_cparams(("parallel", "parallel")),
        name="peer_route",
    )(x, g, shift, scale, wq, kbig)


PEER_TOK = 2 * ROW_TILE
PEER_ECH = 1024


def _expert_body(h_ref, u_ref, vt_ref, a_ref, e1_ref, b_ref, e2_ref, x_ref, g0_ref, g1_ref, fg_ref,
                 o_ref, acc_ref, w_ref, *, final):
    e = pl.program_id(1)
    nk = PEER_NKEYS

    @pl.when(e == 0)
    def _():
        acc_ref[...] = jnp.zeros_like(acc_ref)

    u = lax.dot_general(u_ref[...], h_ref[...], (((1,), (1,)), ((), ())), preferred_element_type=F32)
    for il in range(PEER_ECH // nk):
        gate = None
        for hd in range(PEER_HEADS):
            sel = (a_ref[hd, il:il + 1, :] + b_ref[hd]) >= 0.0
            t = jnp.where(sel, e1_ref[hd, il:il + 1, :] * e2_ref[hd], 0.0)
            gate = t if gate is None else gate + t
        w_ref[il * nk:(il + 1) * nk, :] = (_gelu(u[il * nk:(il + 1) * nk, :]) * gate).astype(BF16)
    acc_ref[...] += jnp.dot(vt_ref[...], w_ref[...], preferred_element_type=F32)

    @pl.when(e == pl.num_programs(1) - 1)
    def _():
        y = acc_ref[...].T
        for half, gref in enumerate((g0_ref, g1_ref)):
            sl = slice(half * ROW_TILE, (half + 1) * ROW_TILE)
            xn = x_ref[sl, :] + gref[0, 0] * y[sl, :]
            if final:
                xn = xn * lax.rsqrt(jnp.mean(xn * xn, axis=-1, keepdims=True) + EPS) * fg_ref[...]
            o_ref[sl, :] = xn


def peer_experts(h, u_tab, vt_tab, a, e1, bm, e2, x, gate, fg, *, final):
    b, s, d = x.shape
    nt = s // ROW_TILE
    m = b * s
    n_exp = u_tab.shape[0]
    i_per = PEER_ECH // PEER_NKEYS

    def gidx(half):
        def f(i, e):
            t = 2 * i + half
            return (t // nt, jnp.minimum(t % nt, 1), 0, 0)
        return f

    out = pl.pallas_call(
        functools.partial(_expert_body, final=final),
        grid=(m // PEER_TOK, n_exp // PEER_ECH),
        in_specs=[
            pl.BlockSpec((PEER_TOK, d), lambda i, e: (i, 0)),
            pl.BlockSpec((PEER_ECH, d), lambda i, e: (e, 0)),
            pl.BlockSpec((d, PEER_ECH), lambda i, e: (0, e)),
            pl.BlockSpec((PEER_HEADS, i_per, PEER_TOK), lambda i, e: (0, e, i)),
            pl.BlockSpec((PEER_HEADS, i_per, PEER_TOK), lambda i, e: (0, e, i)),
            pl.BlockSpec((PEER_HEADS, PEER_NKEYS, PEER_TOK), lambda i, e: (0, 0, i)),
            pl.BlockSpec((PEER_HEADS, PEER_NKEYS, PEER_TOK), lambda i, e: (0, 0, i)),
            pl.BlockSpec((PEER_TOK, d), lambda i, e: (i, 0)),
            pl.BlockSpec((1, 1, 1, d), gidx(0)),
            pl.BlockSpec((1, 1, 1, d), gidx(1)),
            pl.BlockSpec((1, d), lambda i, e: (0, 0)),
        ],
        out_specs=pl.BlockSpec((PEER_TOK, d), lambda i, e: (i, 0)),
        out_shape=jax.ShapeDtypeStruct((m, d), F32),
        scratch_shapes=[pltpu.VMEM((d, PEER_TOK), F32), pltpu.VMEM((PEER_ECH, PEER_TOK), BF16)],
        compiler_params=_cparams(("parallel", "arbitrary")),
        name="peer_experts",
    )(h, u_tab, vt_tab, a, e1, bm, e2, x.reshape(m, d), gate, gate, fg)
    return out.reshape(b, s, d)


def _axial_tables(rows, rope_dim, ctx_len, lane0):
    n_freq = rope_dim // 4
    half = rope_dim // 2
    inv_freq = ROPE_THETA ** (-jnp.arange(n_freq, dtype=F32) / n_freq)
    r = jnp.repeat(jnp.arange(rows, dtype=F32), GRID_W)
    col = jnp.tile(jnp.arange(GRID_W, dtype=F32), rows)
    ang = jnp.concatenate([r[:, None] * inv_freq, col[:, None] * inv_freq], axis=-1)
    cos, sin = jnp.cos(ang), jnp.sin(ang)
    n = cos.shape[0]
    period = 64 if rope_dim == 64 else V7X_LANES
    c_blk = jnp.ones((n, period), F32).at[:, lane0:lane0 + rope_dim].set(jnp.concatenate([cos, cos], -1))
    s1_blk = jnp.zeros((n, period), F32).at[:, lane0:lane0 + half].set(-sin)
    s2_blk = jnp.zeros((n, period), F32).at[:, lane0 + half:lane0 + rope_dim].set(sin)
    reps = V7X_LANES // period
    out = []
    for blk, fill in ((c_blk, 1.0), (s1_blk, 0.0), (s2_blk, 0.0)):
        t = jnp.tile(blk, (1, reps))
        out.append(jnp.concatenate([jnp.full((ctx_len, V7X_LANES), fill, F32), t], axis=0))
    return out


def _block_diag(w):
    n, a, b = w.shape
    eye = jnp.eye(n, dtype=w.dtype)
    return (eye[:, None, :, None] * w[:, :, None, :]).reshape(n * a, n * b)


def kernel(x, c, ctx, c_ctx, w_mod, b_mod, norm1_g, norm2_g, w_in, conv_w, conv_b, lru_wa, lru_ba,
           lru_wi, lru_bi, lru_lambda, diff_lam, diff_subln_g, gqa_qnorm_g, gqa_knorm_g, mla_qnorm_g,
           mla_w_uq, mla_kvnorm_g, mla_w_ukv, w_branch, w_out, peer_wq, peer_keys, peer_u, peer_v,
           final_norm_g):
    bsz, seq, d = x.shape
    ctx_len = ctx.shape[1]
    depth = w_in.shape[0]
    assert ctx_len == ROW_TILE and seq % PEER_TOK == 0 and seq % GRID_W == 0
    rows = seq // GRID_W
    s_all = ctx_len + seq
    xs = jnp.concatenate([ctx, x], axis=1)

    tabs = _axial_tables(rows, DIFF_DK, ctx_len, 0) + _axial_tables(rows, MLA_ROPE, ctx_len, MLA_NOPE)
    grp = jnp.arange(BRANCH_W) // GQA_DH
    avg = (grp[:, None] == grp[None, :]).astype(BF16) * (1.0 / GQA_DH)
    sc_in = jnp.zeros((16, d), F32).at[:bsz].set(jax.nn.silu(c)).at[bsz].set(jax.nn.silu(c_ctx)).astype(BF16)
    zero1 = jnp.zeros((1,), F32)
    ones_g = jnp.ones((1, V7X_LANES), F32)

    for l in range(depth):
        lam_init = 0.8 - 0.6 * math.exp(-0.3 * l)
        mod_all = matmul(sc_in, w_mod[l].astype(BF16)) + b_mod[l]
        mod_b = mod_all[:bsz].reshape(bsz, N_MOD, d)
        mod_c = jnp.broadcast_to(mod_all[bsz].reshape(1, N_MOD, d), (bsz, N_MOD, d))
        mods = [jnp.stack([mod_c[:, k], mod_b[:, k]], axis=1)[:, :, None, :] for k in range(N_MOD)]

        wl = w_in[l]
        o = 0
        parts = []
        for w in (512, 512, 512, 512, 512, 512, 128, 128, 384, 256, 32, 4096):
            parts.append(wl[:, o:o + w])
            o += w
        xa, ya, qb, kb, vb, qg, kg, vg, cq, ckv, kr, zg = parts
        kr_blk = jnp.zeros((d, V7X_LANES), F32).at[:, MLA_NOPE:MLA_NOPE + MLA_ROPE].set(kr)
        w_cat = jnp.concatenate([zg, xa, ya, qb, kb, vb, qg, kg, vg, cq, ckv, kr_blk], axis=1).astype(BF16)

        h = adaln(xs, norm1_g[l][None], mods[0], mods[1], 0)
        z = matmul(h.reshape(bsz * s_all, d), w_cat).reshape(bsz, s_all, w_cat.shape[1])

        negc = (-LRU_C * jax.nn.softplus(-lru_lambda[l]))[:, None, None, :]
        wg = jnp.stack([jnp.concatenate([_block_diag(lru_wa[l, dd]), _block_diag(lru_wi[l, dd])], axis=1)
                        for dd in range(2)]).astype(BF16)[:, None]
        bg = jnp.stack([jnp.concatenate([lru_ba[l, dd], lru_bi[l, dd]]) for dd in range(2)])[:, None, None, :]
        cb = conv_b[l][None]
        hf = lru_scan(z, 8, 9, conv_w[l], cb, wg[0], bg[0], negc[0], None, reverse=False)
        oa = lru_scan(z, 8, 9, conv_w[l], cb, wg[1], bg[1], negc[1], hf, reverse=True)

        uq = mla_w_uq[l].reshape(MLA_Q_RANK, MLA_HEADS, MLA_NOPE + MLA_ROPE)
        wuq = jnp.pad(uq, ((0, 0), (0, 0), (0, V7X_LANES - MLA_NOPE - MLA_ROPE))).reshape(MLA_Q_RANK, -1)
        ukv = mla_w_ukv[l].reshape(MLA_KV_RANK, MLA_HEADS, MLA_NOPE + MLA_DV)
        wuk = jnp.pad(ukv[:, :, :MLA_NOPE], ((0, 0), (0, 0), (0, V7X_LANES - MLA_NOPE))).reshape(MLA_KV_RANK, -1)
        wuv = ukv[:, :, MLA_NOPE:].reshape(MLA_KV_RANK, -1)
        qd, kd, vd, qq, kq, vq, qm, km, vm = attn_prep(
            z, tabs, avg, jnp.tile(gqa_qnorm_g[l], GQA_HEADS)[None], jnp.tile(gqa_knorm_g[l], GQA_KV_HEADS)[None],
            mla_qnorm_g[l][None], mla_kvnorm_g[l][None], wuq.astype(BF16), wuk.astype(BF16), wuv.astype(BF16))
        lv = diff_lam[l]
        lam = (jnp.exp(jnp.sum(lv[0] * lv[1])) - jnp.exp(jnp.sum(lv[2] * lv[3])) + lam_init).reshape(1)
        ob = attention(qd, kd, vd, lam, diff_subln_g[l][None], mode="diff", out_scale=1.0 - lam_init)
        og = attention(qq, kq, vq, zero1, ones_g, mode="pair")
        om = attention(qm, km, vm, zero1, ones_g, mode="mla")

        xs = merge(oa, ob, og, om, z, w_branch[l].astype(BF16), w_out[l].astype(BF16), xs, mods[2])

        kbig = _block_diag(peer_keys[l].reshape(2 * PEER_HEADS, PEER_NKEYS, PEER_DK // 2)).astype(BF16)
        h2, pa, pb, pe1, pe2 = peer_route(xs, norm2_g[l][None], mods[3], mods[4], peer_wq[l].astype(BF16), kbig)
        xs = peer_experts(h2, peer_u[l].astype(BF16), peer_v[l].T.astype(BF16), pa, pe1, pb, pe2, xs, mods[5],
                          final_norm_g[None], final=(l == depth - 1))
    return xs[:, ctx_len:]
```

```python
import functools
import math

import jax
import jax.numpy as jnp
from jax import lax
from jax.experimental import pallas as pl
from jax.experimental.pallas import tpu as pltpu

F32 = jnp.float32
BF16 = jnp.bfloat16

EPS = 1e-6
GRID_W = 64
ROPE_THETA = 10000.0
N_MOD = 6

D_RNN = 512
RNN_BLOCKS = 8
CONV_W = 4
LRU_C = 8.0

DIFF_HEADS = 4
DIFF_DK = 64
DIFF_DV = 128
GQA_HEADS = 8
GQA_KV_HEADS = 2
GQA_DH = 64
MLA_HEADS = 8
MLA_NOPE = 64
MLA_ROPE = 32
MLA_DV = 64
MLA_Q_RANK = 384
MLA_KV_RANK = 256
N_BRANCH = 4
BRANCH_W = 512

PEER_HEADS = 8
PEER_NKEYS = 128
PEER_DK = 128
PEER_TOPK = 16

V7X_LANES = 128
V7X_SUBLANES = 8
V7X_VMEM_BYTES = 64 * 1024 * 1024
V7X_MXU_DIM = 256
VMEM_LIMIT = 48 * 1024 * 1024

ROW_TILE = 256


def _cparams(sem):
    return pltpu.CompilerParams(dimension_semantics=sem, vmem_limit_bytes=VMEM_LIMIT)


def _adaln_body(x_ref, g_ref, sh_ref, sc_ref, o_ref):
    x = x_ref[0]
    ms = jnp.mean(x * x, axis=-1, keepdims=True)
    y = x * lax.rsqrt(ms + EPS) * g_ref[...]
    o_ref[0] = (y * (1.0 + sc_ref[0, 0]) + sh_ref[0, 0]).astype(o_ref.dtype)


def adaln(x, g, shift, scale, t0):
    b, s, d = x.shape
    nt = s // ROW_TILE - t0
    seg = lambda bi, ti: (bi, jnp.minimum(ti + t0, 1), 0, 0)
    return pl.pallas_call(
        _adaln_body,
        grid=(b, nt),
        in_specs=[
            pl.BlockSpec((1, ROW_TILE, d), lambda bi, ti: (bi, ti + t0, 0)),
            pl.BlockSpec((1, d), lambda bi, ti: (0, 0)),
            pl.BlockSpec((1, 1, 1, d), seg),
            pl.BlockSpec((1, 1, 1, d), seg),
        ],
        out_specs=pl.BlockSpec((1, ROW_TILE, d), lambda bi, ti: (bi, ti, 0)),
        out_shape=jax.ShapeDtypeStruct((b, nt * ROW_TILE, d), BF16),
        compiler_params=_cparams(("parallel", "parallel")),
        name="adaln",
    )(x, g, shift, scale)


def _mm_body(a_ref, w_ref, o_ref):
    o_ref[...] = jnp.dot(a_ref[...], w_ref[...], preferred_element_type=F32).astype(o_ref.dtype)


def matmul(a, w, out_dtype=F32, tm=512, tn=512):
    m, k = a.shape
    _, n = w.shape
    tm = math.gcd(tm, m)
    tn = math.gcd(tn, n)
    return pl.pallas_call(
        _mm_body,
        grid=(n // tn, m // tm),
        in_specs=[
            pl.BlockSpec((tm, k), lambda j, i: (i, 0)),
            pl.BlockSpec((k, tn), lambda j, i: (0, j)),
        ],
        out_specs=pl.BlockSpec((tm, tn), lambda j, i: (i, j)),
        out_shape=jax.ShapeDtypeStruct((m, n), out_dtype),
        compiler_params=_cparams(("parallel", "parallel")),
        name="matmul",
    )(a, w)


def _mult(x, m):
    return x if isinstance(x, int) else pl.multiple_of(x, m)


def _gelu(x):
    c = math.sqrt(2.0 / math.pi)
    return x * (0.5 * (1.0 + jnp.tanh(c * (x + 0.044715 * (x * x * x)))))


def _sigmoid(x):
    return 1.0 / (1.0 + jnp.exp(-x))


TILE_ROWS = V7X_SUBLANES
HALO_ROWS = 2 * V7X_SUBLANES


def _lru_body(x_ref, xp_ref, xn_ref, cw_ref, cb_ref, wg_ref, bg_ref, nc_ref, *rest,
              reverse, n_chunks):
    if reverse:
        ya_ref, hf_ref, o_ref, carry_ref = rest
    else:
        o_ref, carry_ref = rest
    step = pl.program_id(1)
    if reverse:
        chunk = jnp.where(step == 0, 0, n_chunks - step)
    else:
        chunk = step
    rows = ROW_TILE
    c = x_ref.shape[-1]

    @pl.when(step == 0)
    def _():
        carry_ref[...] = jnp.zeros_like(carry_ref)

    x = x_ref[0].astype(F32)
    xx = jnp.concatenate([xp_ref[0].astype(F32), x, xn_ref[0].astype(F32)], axis=0)
    n_xx = rows + 2 * HALO_ROWS
    row = lax.broadcasted_iota(jnp.int32, (rows, c), 0)
    seg_start = jnp.logical_or(chunk == 0, chunk == 1)
    seg_end = jnp.logical_or(chunk == 0, chunk == n_chunks - 1)

    def shifted(d):
        return pltpu.roll(xx, (n_xx - d) % n_xx, axis=0)[HALO_ROWS:HALO_ROWS + rows]

    x_m1 = jnp.where(jnp.logical_and(seg_start, row == 0), 0.0, shifted(-1))
    x_p1 = jnp.where(jnp.logical_and(seg_end, row >= rows - 1), 0.0, shifted(1))
    x_p2 = jnp.where(jnp.logical_and(seg_end, row >= rows - 2), 0.0, shifted(2))
    cw = cw_ref[...]
    u = cw[0:1] * x_m1 + cw[1:2] * x + cw[2:3] * x_p1 + cw[3:4] * x_p2 + cb_ref[...]

    z = jnp.dot(u.astype(BF16), wg_ref[0], preferred_element_type=F32) + bg_ref[0]
    r = _sigmoid(z[:, :c])
    gi = _sigmoid(z[:, c:])
    a = jnp.exp(nc_ref[0] * r)
    bv = jnp.sqrt(1.0 - a * a) * (gi * u)

    sub = row % TILE_ROWS
    for dstep in (1, 2, 4):
        if reverse:
            sh = (rows - dstep) % rows
            keep = sub <= TILE_ROWS - 1 - dstep
        else:
            sh = dstep
            keep = sub >= dstep
        a_sh = jnp.where(keep, pltpu.roll(a, sh, axis=0), 1.0)
        b_sh = jnp.where(keep, pltpu.roll(bv, sh, axis=0), 0.0)
        bv = a * b_sh + bv
        a = a * a_sh

    carry = carry_ref[...]
    n_tiles = rows // TILE_ROWS
    order = range(n_tiles - 1, -1, -1) if reverse else range(n_tiles)
    hs = [None] * n_tiles
    for t in order:
        sl = slice(t * TILE_ROWS, (t + 1) * TILE_ROWS)
        h = bv[sl] + a[sl] * carry
        hs[t] = h
        edge = h[0:1] if reverse else h[TILE_ROWS - 1:TILE_ROWS]
        carry = jnp.broadcast_to(edge, (TILE_ROWS, c))
    carry_ref[...] = carry
    h_all = jnp.concatenate(hs, axis=0)
    if reverse:
        o_ref[0] = (_gelu(ya_ref[0].astype(F32)) * (hf_ref[0] + h_all)).astype(o_ref.dtype)
    else:
        o_ref[0] = h_all


def lru_scan(z, xa_col, ya_col, conv_w, conv_b, wg, bg, negc, hf, *, reverse):
    b, s, _ = z.shape
    c = D_RNN
    n_chunks = s // ROW_TILE
    per = ROW_TILE // HALO_ROWS
    n8 = s // HALO_ROWS
    if reverse:
        cidx = lambda st: jnp.where(st == 0, 0, n_chunks - st)
    else:
        cidx = lambda st: st
    in_specs = [
        pl.BlockSpec((1, ROW_TILE, c), lambda bi, st: (bi, cidx(st), xa_col)),
        pl.BlockSpec((1, HALO_ROWS, c), lambda bi, st: (bi, jnp.maximum(cidx(st) * per - 1, 0), xa_col)),
        pl.BlockSpec((1, HALO_ROWS, c), lambda bi, st: (bi, jnp.minimum((cidx(st) + 1) * per, n8 - 1), xa_col)),
        pl.BlockSpec((CONV_W, c), lambda bi, st: (0, 0)),
        pl.BlockSpec((1, c), lambda bi, st: (0, 0)),
        pl.BlockSpec((1, c, 2 * c), lambda bi, st: (0, 0, 0)),
        pl.BlockSpec((1, 1, 2 * c), lambda bi, st: (0, 0, 0)),
        pl.BlockSpec((1, 1, c), lambda bi, st: (0, 0, 0)),
    ]
    args = [z, z, z, conv_w, conv_b, wg, bg, negc]
    if reverse:
        in_specs += [
            pl.BlockSpec((1, ROW_TILE, c), lambda bi, st: (bi, cidx(st), ya_col)),
            pl.BlockSpec((1, ROW_TILE, c), lambda bi, st: (bi, cidx(st), 0)),
        ]
        args += [z, hf]
    return pl.pallas_call(
        functools.partial(_lru_body, reverse=reverse, n_chunks=n_chunks),
        grid=(b, n_chunks),
        in_specs=in_specs,
        out_specs=pl.BlockSpec((1, ROW_TILE, c), lambda bi, st: (bi, cidx(st), 0)),
        out_shape=jax.ShapeDtypeStruct((b, s, c), BF16 if reverse else F32),
        scratch_shapes=[pltpu.VMEM((TILE_ROWS, c), F32)],
        compiler_params=_cparams(("parallel", "arbitrary")),
        name="lru_rev" if reverse else "lru_fwd",
    )(*args)


def _tile_lanes(t, n):
    return jnp.concatenate([t] * n, axis=1) if n > 1 else t


def _rope(x, c, s1, s2, r):
    w = x.shape[-1]
    n = w // V7X_LANES
    return (x * _tile_lanes(c, n) + pltpu.roll(x, w - r, axis=1) * _tile_lanes(s1, n)
            + pltpu.roll(x, r, axis=1) * _tile_lanes(s2, n))


def _split_dot(x, m_ref):
    hi = x.astype(BF16)
    lo = (x - hi.astype(F32)).astype(BF16)
    m = m_ref[...]
    return jnp.dot(hi, m, preferred_element_type=F32) + jnp.dot(lo, m, preferred_element_type=F32)


def _rms(x, g):
    return x * lax.rsqrt(jnp.mean(x * x, axis=-1, keepdims=True) + EPS) * g


def _prep_body(qb_ref, kb_ref, vb_ref, qg_ref, tail_ref,
               c64_ref, s164_ref, s264_ref, cm_ref, s1m_ref, s2m_ref,
               avg_ref, gq_ref, gk_ref, gcq_ref, gckv_ref, wuq_ref, wuk_ref, wuv_ref,
               qd_ref, kd_ref, vd_ref, qq_ref, kq_ref, vq_ref, qm_ref, km_ref, vm_ref):
    c64, s164, s264 = c64_ref[...], s164_ref[...], s264_ref[...]
    cm, s1m, s2m = cm_ref[...], s1m_ref[...], s2m_ref[...]
    half64 = DIFF_DK // 2
    qd_ref[0] = (_rope(qb_ref[0].astype(F32), c64, s164, s264, half64) * (DIFF_DK ** -0.5)).astype(BF16)
    kd_ref[0] = _rope(kb_ref[0].astype(F32), c64, s164, s264, half64).astype(BF16)
    vd_ref[0] = vb_ref[0].astype(BF16)
    qg = qg_ref[0].astype(F32)
    ms = _split_dot(qg * qg, avg_ref)
    qn = qg * lax.rsqrt(ms + EPS) * gq_ref[...]
    qq_ref[0] = (_rope(qn, c64, s164, s264, GQA_DH // 2) * (GQA_DH ** -0.5)).astype(BF16)
    tail = tail_ref[0].astype(F32)
    kg = tail[:, 0:128]
    msk = _split_dot(kg * kg, avg_ref.at[0:128, 0:128])
    kn = _rope(kg * lax.rsqrt(msk + EPS) * gk_ref[...], c64, s164, s264, GQA_DH // 2).astype(BF16)
    vg = tail[:, 128:256].astype(BF16)
    kq_ref[0] = jnp.concatenate([kn[:, 0:64], kn[:, 0:64], kn[:, 64:128], kn[:, 64:128]], axis=1)
    vq_ref[0] = jnp.concatenate([vg[:, 0:64], vg[:, 0:64], vg[:, 64:128], vg[:, 64:128]], axis=1)
    cq = tail[:, 256:256 + MLA_Q_RANK]
    ckv = tail[:, 640:640 + MLA_KV_RANK]
    kr = tail[:, 896:1024]
    qf = jnp.dot(_rms(cq, gcq_ref[...]).astype(BF16), wuq_ref[...], preferred_element_type=F32)
    scale = (MLA_NOPE + MLA_ROPE) ** -0.5
    qm_ref[0] = (_rope(qf, cm, s1m, s2m, MLA_ROPE // 2) * scale).astype(BF16)
    ckvn = _rms(ckv, gckv_ref[...]).astype(BF16)
    kf = jnp.dot(ckvn, wuk_ref[...], preferred_element_type=F32)
    krr = _rope(kr, cm, s1m, s2m, MLA_ROPE // 2)
    km_ref[0] = (kf + _tile_lanes(krr, MLA_HEADS)).astype(BF16)
    vm_ref[0] = jnp.dot(ckvn, wuv_ref[...], preferred_element_type=F32).astype(BF16)


def attn_prep(z, tabs, avg, gq, gk, gcq, gckv, wuq, wuk, wuv):
    b, s, _ = z.shape
    nt = s // ROW_TILE
    zspec = lambda w, idx: pl.BlockSpec((1, ROW_TILE, w), lambda ti, bi: (bi, ti, idx))
    tab = pl.BlockSpec((ROW_TILE, V7X_LANES), lambda ti, bi: (ti, 0))
    full = lambda a: pl.BlockSpec(a.shape, lambda ti, bi: (0,) * a.ndim)
    ospec = lambda w: pl.BlockSpec((1, ROW_TILE, w), lambda ti, bi: (bi, ti, 0))
    oshape = lambda w: jax.ShapeDtypeStruct((b, s, w), BF16)
    widths = (512, 512, 512, 512, 256, 256, 1024, 1024, 512)
    return pl.pallas_call(
        _prep_body,
        grid=(nt, b),
        in_specs=[zspec(512, 10), zspec(512, 11), zspec(512, 12), zspec(512, 13), zspec(1024, 7)]
        + [tab] * 6 + [full(a) for a in (avg, gq, gk, gcq, gckv, wuq, wuk, wuv)],
        out_specs=[ospec(w) for w in widths],
        out_shape=[oshape(w) for w in widths],
        compiler_params=_cparams(("parallel", "parallel")),
        name="attn_prep",
    )(z, z, z, z, z, *tabs, avg, gq, gk, gcq, gckv, wuq, wuk, wuv)


def _attn_body(lam_ref, q_ref, k_ref, v_ref, g_ref, o_ref, *, mode, out_scale):
    ti = pl.program_id(2)
    lane = lax.broadcasted_iota(jnp.int32, (ROW_TILE, V7X_LANES), 1)
    lo_half = lane < (V7X_LANES // 2)

    def run(n_keys):
        q = q_ref[0]
        outs = []
        for j in range(2):
            if mode == "mla":
                qj = q[:, j * V7X_LANES:(j + 1) * V7X_LANES]
                kj = k_ref[0, 0:n_keys, j * V7X_LANES:(j + 1) * V7X_LANES]
            else:
                keep = lo_half if j == 0 else jnp.logical_not(lo_half)
                qj = jnp.where(keep, q.astype(F32), 0.0).astype(BF16)
                kj = k_ref[0, 0:n_keys, :]
            s = lax.dot_general(qj, kj, (((1,), (1,)), ((), ())), preferred_element_type=F32)
            m = jnp.max(s, axis=-1, keepdims=True)
            p = jnp.exp(s - m)
            l = jnp.sum(p, axis=-1, keepdims=True)
            o = jnp.dot(p.astype(BF16), v_ref[0, 0:n_keys, :], preferred_element_type=F32)
            outs.append(o / l)
        if mode == "diff":
            o = outs[0] - lam_ref[0] * outs[1]
            o = o * lax.rsqrt(jnp.mean(o * o, axis=-1, keepdims=True) + EPS) * g_ref[...] * out_scale
        else:
            o = jnp.where(lo_half, outs[0], outs[1])
        o_ref[0] = o.astype(o_ref.dtype)

    @pl.when(ti == 0)
    def _():
        run(ROW_TILE)

    @pl.when(ti > 0)
    def _():
        run(k_ref.shape[1])


def attention(q, k, v, lam, g, *, mode, out_scale=1.0):
    b, s, wq = q.shape
    groups = 4
    qw = wq // groups
    kw = k.shape[-1] // (2 if mode == "pair" else groups)
    kidx = (lambda gi: gi // 2) if mode == "pair" else (lambda gi: gi)
    nt = s // ROW_TILE
    return pl.pallas_call(
        functools.partial(_attn_body, mode=mode, out_scale=out_scale),
        grid=(b, groups, nt),
        in_specs=[
            pl.BlockSpec(memory_space=pltpu.SMEM),
            pl.BlockSpec((1, ROW_TILE, qw), lambda bi, gi, ti: (bi, ti, gi)),
            pl.BlockSpec((1, s, kw), lambda bi, gi, ti: (bi, 0, kidx(gi))),
            pl.BlockSpec((1, s, V7X_LANES), lambda bi, gi, ti: (bi, 0, kidx(gi))),
            pl.BlockSpec((1, V7X_LANES), lambda bi, gi, ti: (0, 0)),
        ],
        out_specs=pl.BlockSpec((1, ROW_TILE, V7X_LANES), lambda bi, gi, ti: (bi, ti, gi)),
        out_shape=jax.ShapeDtypeStruct((b, s, groups * V7X_LANES), BF16),
        compiler_params=_cparams(("parallel", "parallel", "arbitrary")),
        name="attn_" + mode,
    )(lam, q, k, v, g)


def _merge_body(oa_ref, ob_ref, og_ref, om_ref, zg_ref, wb_ref, wo_ref, x_ref, gate_ref, o_ref):
    d = x_ref.shape[-1]
    merged = None
    for k, o_k in enumerate((oa_ref, ob_ref, og_ref, om_ref)):
        t = jnp.dot(o_k[0], wb_ref[k], preferred_element_type=F32)
        t = t * _sigmoid(zg_ref[0, :, k * d:(k + 1) * d].astype(F32))
        merged = t if merged is None else merged + t
    y = jnp.dot(merged.astype(BF16), wo_ref[...], preferred_element_type=F32)
    o_ref[0] = x_ref[0] + gate_ref[0, 0] * y


def merge(oa, ob, og, om, z, wb, wo, x, gate):
    b, s, d = x.shape
    nt = s // ROW_TILE
    bspec = pl.BlockSpec((1, ROW_TILE, BRANCH_W), lambda bi, ti: (bi, ti, 0))
    return pl.pallas_call(
        _merge_body,
        grid=(b, nt),
        in_specs=[bspec, bspec, bspec, bspec,
                  pl.BlockSpec((1, ROW_TILE, N_BRANCH * d), lambda bi, ti: (bi, ti, 0)),
                  pl.BlockSpec(wb.shape, lambda bi, ti: (0, 0, 0)),
                  pl.BlockSpec(wo.shape, lambda bi, ti: (0, 0)),
                  pl.BlockSpec((1, ROW_TILE, d), lambda bi, ti: (bi, ti, 0)),
                  pl.BlockSpec((1, 1, 1, d), lambda bi, ti: (bi, jnp.minimum(ti, 1), 0, 0))],
        out_specs=pl.BlockSpec((1, ROW_TILE, d), lambda bi, ti: (bi, ti, 0)),
        out_shape=jax.ShapeDtypeStruct((b, s, d), F32),
        compiler_params=_cparams(("parallel", "parallel")),
        name="merge",
    )(oa, ob, og, om, z, wb, wo, x, gate)


NEG_BIG = float(jnp.finfo(jnp.float32).min)


def _route_body(x_ref, g_ref, sh_ref, sc_ref, wq_ref, kb_ref,
                h_ref, a_ref, b_ref, e1_ref, e2_ref, st_ref, ta_ref, tb_ref):
    x = x_ref[0]
    ms = jnp.mean(x * x, axis=-1, keepdims=True)
    hf = x * lax.rsqrt(ms + EPS) * g_ref[...] * (1.0 + sc_ref[0, 0]) + sh_ref[0, 0]
    h_ref[...] = hf.T.astype(BF16)
    h = hf.astype(BF16)
    q = jnp.dot(h, wq_ref[...], preferred_element_type=F32).astype(BF16)
    nk = PEER_NKEYS
    for hd in range(PEER_HEADS):
        st_ref[hd * 2 * nk:(hd + 1) * 2 * nk, :] = lax.dot_general(
            kb_ref[hd], q[:, hd * PEER_DK:(hd + 1) * PEER_DK], (((1,), (1,)), ((), ())),
            preferred_element_type=F32)

    def top_rows(v, dst_ref):
        for r in range(PEER_TOPK):
            m = jnp.max(v, axis=0, keepdims=True)
            dst_ref[r:r + 1, :] = m
            v = jnp.where(v >= m, NEG_BIG, v)

    row8 = lax.broadcasted_iota(jnp.int32, (TILE_ROWS, ROW_TILE), 0)

    def head(hd, carry):
        base = pl.multiple_of(hd * (2 * nk), 2 * nk)
        s1 = st_ref[pl.ds(base, nk), :]
        s2 = st_ref[pl.ds(base + nk, nk), :]
        top_rows(s1, ta_ref)
        top_rows(s2, tb_ref)
        pieces = [ta_ref[0:1, :] + tb_ref[...]]
        tb8 = tb_ref[0:TILE_ROWS, :]
        for p in range(1, PEER_TOPK):
            n_q = PEER_TOPK // (p + 1)
            c = ta_ref[p:p + 1, :] + tb8
            pieces.append(c if n_q >= TILE_ROWS else jnp.where(row8 < n_q, c, NEG_BIG))
        cand = jnp.concatenate(pieces, axis=0)
        top = ta_ref[0:1, :] + tb_ref[0:1, :]
        zsum = jnp.zeros_like(top)
        m = top
        for r in range(PEER_TOPK):
            m = jnp.max(cand, axis=0, keepdims=True)
            zsum = zsum + jnp.exp(m - top)
            cand = jnp.where(cand >= m, NEG_BIG, cand)
        a_ref[hd] = m - s1
        e1_ref[hd] = jnp.exp(s1 - ta_ref[0:1, :]) / zsum
        e2 = jnp.exp(s2 - tb_ref[0:1, :])
        for lt in range(ROW_TILE // V7X_LANES):
            b_ref[hd, lt] = s2[:, lt * V7X_LANES:(lt + 1) * V7X_LANES]
            e2_ref[hd, lt] = e2[:, lt * V7X_LANES:(lt + 1) * V7X_LANES]
        return carry

    lax.fori_loop(0, PEER_HEADS, head, 0)


def peer_route(x, g, shift, scale, wq, kbig):
    b, s, d = x.shape
    nt = s // ROW_TILE
    m = b * s
    seg = lambda bi, ti: (bi, jnp.minimum(ti, 1), 0, 0)
    flat = lambda bi, ti: (0, 0, bi * nt + ti)
    kspec = pl.BlockSpec((PEER_HEADS, PEER_NKEYS, ROW_TILE), flat)
    kshape = jax.ShapeDtypeStruct((PEER_HEADS, PEER_NKEYS, m), F32)
    lt_per = ROW_TILE // V7X_LANES
    tspec = pl.BlockSpec((PEER_HEADS, lt_per, PEER_NKEYS, V7X_LANES), lambda bi, ti: (0, bi * nt + ti, 0, 0))
    tshape = jax.ShapeDtypeStruct((PEER_HEADS, m // V7X_LANES, PEER_NKEYS, V7X_LANES), F32)
    return pl.pallas_call(
        _route_body,
        grid=(b, nt),
        in_specs=[
            pl.BlockSpec((1, ROW_TILE, d), lambda bi, ti: (bi, ti, 0)),
            pl.BlockSpec((1, d), lambda bi, ti: (0, 0)),
            pl.BlockSpec((1, 1, 1, d), seg),
            pl.BlockSpec((1, 1, 1, d), seg),
            pl.BlockSpec(wq.shape, lambda bi, ti: (0, 0)),
            pl.BlockSpec(kbig.shape, lambda bi, ti: (0, 0, 0)),
        ],
        out_specs=[pl.BlockSpec((d, ROW_TILE), lambda bi, ti: (0, bi * nt + ti)), kspec, tspec, kspec, tspec],
        out_shape=[jax.ShapeDtypeStruct((d, m), BF16), kshape, tshape, kshape, tshape],
        scratch_shapes=[pltpu.VMEM((2 * PEER_HEADS * PEER_NKEYS, ROW_TILE), F32),
                        pltpu.VMEM((PEER_TOPK, ROW_TILE), F32),
                        pltpu.VMEM((PEER_TOPK, ROW_TILE), F32)],
        compiler_params=_cparams(("parallel", "parallel")),
        name="peer_route",
    )(x, g, shift, scale, wq, kbig)


PEER_TOK = 2 * ROW_TILE
PEER_ECH = 1024


def _expert_body(h_ref, u0_ref, u_ref, vt_ref, a_ref, e1_ref, b_ref, e2_ref, x_ref, g0_ref, g1_ref,
                 fg_ref, o_ref, acc_ref, w0_ref, w1_ref, uf0_ref, uf1_ref, thb_ref, e1b_ref, *,
                 final, n_ch):
    e = pl.program_id(1)
    nk = PEER_NKEYS
    n_il = PEER_ECH // nk
    sub = TILE_ROWS
    ib = 2
    n_ib = n_il // ib
    tn = (((1,), (1,)), ((), ()))
    mq = V7X_MXU_DIM
    uf_refs = (uf0_ref, uf1_ref)
    w_refs = (w0_ref, w1_ref)

    def gate_block(g, cur):
        uf_ref, w_ref = uf_refs[cur], w_refs[cur]
        lt = g // n_ib
        tl = pl.ds(_mult(lt * V7X_LANES, V7X_LANES), V7X_LANES)
        i0 = (g % n_ib) * ib
        gates = [None] * ib
        for hd in range(PEER_HEADS):
            bj = b_ref[hd, lt].reshape(nk // sub, sub, V7X_LANES)
            e2j = e2_ref[hd, lt].reshape(nk // sub, sub, V7X_LANES)
            for k in range(ib):
                th = thb_ref[hd * n_il + i0 + k, :, tl][None]
                e1 = e1b_ref[hd * n_il + i0 + k, :, tl][None]
                t = jnp.where(bj >= th, e1 * e2j, 0.0)
                gates[k] = t if gates[k] is None else gates[k] + t
        for k in range(ib):
            rows = pl.ds(_mult((i0 + k) * nk, nk), nk)
            act = _gelu(uf_ref[lt, rows, :]) * gates[k].reshape(nk, V7X_LANES)
            w_ref[rows, tl] = act.astype(BF16)

    def next_u(it, oth):
        r0 = _mult((it // 2) * mq, mq)
        c0 = _mult((it % 2) * mq, mq)
        res = jnp.dot(u_ref[pl.ds(r0, mq), :], h_ref[:, pl.ds(c0, mq)], preferred_element_type=F32)
        for k in range(mq // V7X_LANES):
            uf_refs[oth][(it % 2) * (mq // V7X_LANES) + k, pl.ds(r0, mq), :] = (
                res[:, k * V7X_LANES:(k + 1) * V7X_LANES])

    def prev_out(it, oth):
        r0 = _mult((it // 2) * mq, mq)
        c0 = _mult((it % 2) * mq, mq)
        acc_ref[pl.ds(r0, mq), pl.ds(c0, mq)] += jnp.dot(
            vt_ref[pl.ds(r0, mq), :], w_refs[oth][:, pl.ds(c0, mq)], preferred_element_type=F32)

    n_it = (PEER_ECH // mq) * (PEER_TOK // mq)
    per_it = (PEER_TOK // V7X_LANES) * n_ib // n_it

    def run_chunk(with_prev, cur):
        for hd in range(PEER_HEADS):
            for il in range(n_il):
                thb_ref[hd * n_il + il] = jnp.broadcast_to(a_ref[hd, il:il + 1, :], (sub, PEER_TOK))
                e1b_ref[hd * n_il + il] = jnp.broadcast_to(e1_ref[hd, il:il + 1, :], (sub, PEER_TOK))

        def step(it, carry):
            for k in range(per_it):
                gate_block(it * per_it + k, cur)
            next_u(it, 1 - cur)
            if with_prev:
                prev_out(it, 1 - cur)
            return carry

        for it in range(n_it):
            step(it, 0)

    @pl.when(e == 0)
    def _():
        acc_ref[...] = jnp.zeros_like(acc_ref)
        u_first = jnp.dot(u0_ref[...], h_ref[...], preferred_element_type=F32)
        for k in range(PEER_TOK // V7X_LANES):
            uf0_ref[k] = u_first[:, k * V7X_LANES:(k + 1) * V7X_LANES]
        run_chunk(False, 0)

    for par in range(2):
        @pl.when(jnp.logical_and(jnp.logical_and(e > 0, e < n_ch), e % 2 == par))
        def _():
            run_chunk(True, par)

    @pl.when(e == n_ch)
    def _():
        acc_ref[...] += jnp.dot(vt_ref[...], w_refs[(n_ch - 1) % 2][...], preferred_element_type=F32)
        y = acc_ref[...].T
        for half, gref in enumerate((g0_ref, g1_ref)):
            sl = slice(half * ROW_TILE, (half + 1) * ROW_TILE)
            xn = x_ref[sl, :] + gref[0, 0] * y[sl, :]
            if final:
                xn = xn * lax.rsqrt(jnp.mean(xn * xn, axis=-1, keepdims=True) + EPS) * fg_ref[...]
            o_ref[sl, :] = xn


def peer_experts(h, u_tab, vt_tab, a, e1, bm, e2, x, gate, fg, *, final):
    b, s, d = x.shape
    nt = s // ROW_TILE
    m = b * s
    n_exp = u_tab.shape[0]
    i_per = PEER_ECH // PEER_NKEYS

    def gidx(half):
        def f(i, e):
            t = 2 * i + half
            return (t // nt, jnp.minimum(t % nt, 1), 0, 0)
        return f

    n_ch = n_exp // PEER_ECH
    last = n_ch - 1
    out = pl.pallas_call(
        functools.partial(_expert_body, final=final, n_ch=n_ch),
        grid=(m // PEER_TOK, n_ch + 1),
        in_specs=[
            pl.BlockSpec((d, PEER_TOK), lambda i, e: (0, i)),
            pl.BlockSpec((PEER_ECH, d), lambda i, e: (0, 0)),
            pl.BlockSpec((PEER_ECH, d), lambda i, e: (jnp.minimum(e + 1, last), 0)),
            pl.BlockSpec((d, PEER_ECH), lambda i, e: (0, jnp.maximum(e - 1, 0))),
            pl.BlockSpec((PEER_HEADS, i_per, PEER_TOK), lambda i, e: (0, jnp.minimum(e, last), i)),
            pl.BlockSpec((PEER_HEADS, i_per, PEER_TOK), lambda i, e: (0, jnp.minimum(e, last), i)),
            pl.BlockSpec((PEER_HEADS, PEER_TOK // V7X_LANES, PEER_NKEYS, V7X_LANES), lambda i, e: (0, i, 0, 0)),
            pl.BlockSpec((PEER_HEADS, PEER_TOK // V7X_LANES, PEER_NKEYS, V7X_LANES), lambda i, e: (0, i, 0, 0)),
            pl.BlockSpec((PEER_TOK, d), lambda i, e: (i, 0)),
            pl.BlockSpec((1, 1, 1, d), gidx(0)),
            pl.BlockSpec((1, 1, 1, d), gidx(1)),
            pl.BlockSpec((1, d), lambda i, e: (0, 0)),
        ],
        out_specs=pl.BlockSpec((PEER_TOK, d), lambda i, e: (i, 0)),
        out_shape=jax.ShapeDtypeStruct((m, d), F32),
        scratch_shapes=[pltpu.VMEM((d, PEER_TOK), F32),
                        pltpu.VMEM((PEER_ECH, PEER_TOK), BF16), pltpu.VMEM((PEER_ECH, PEER_TOK), BF16),
                        pltpu.VMEM((PEER_TOK // V7X_LANES, PEER_ECH, V7X_LANES), F32),
                        pltpu.VMEM((PEER_TOK // V7X_LANES, PEER_ECH, V7X_LANES), F32),
                        pltpu.VMEM((PEER_HEADS * i_per, TILE_ROWS, PEER_TOK), F32),
                        pltpu.VMEM((PEER_HEADS * i_per, TILE_ROWS, PEER_TOK), F32)],
        compiler_params=_cparams(("parallel", "arbitrary")),
        name="peer_experts",
    )(h, u_tab, u_tab, vt_tab, a, e1, bm, e2, x.reshape(m, d), gate, gate, fg)
    return out.reshape(b, s, d)


def _axial_tables(rows, rope_dim, ctx_len, lane0):
    n_freq = rope_dim // 4
    half = rope_dim // 2
    inv_freq = ROPE_THETA ** (-jnp.arange(n_freq, dtype=F32) / n_freq)
    r = jnp.repeat(jnp.arange(rows, dtype=F32), GRID_W)
    col = jnp.tile(jnp.arange(GRID_W, dtype=F32), rows)
    ang = jnp.concatenate([r[:, None] * inv_freq, col[:, None] * inv_freq], axis=-1)
    cos, sin = jnp.cos(ang), jnp.sin(ang)
    n = cos.shape[0]
    period = 64 if rope_dim == 64 else V7X_LANES
    c_blk = jnp.ones((n, period), F32).at[:, lane0:lane0 + rope_dim].set(jnp.concatenate([cos, cos], -1))
    s1_blk = jnp.zeros((n, period), F32).at[:, lane0:lane0 + half].set(-sin)
    s2_blk = jnp.zeros((n, period), F32).at[:, lane0 + half:lane0 + rope_dim].set(sin)
    reps = V7X_LANES // period
    out = []
    for blk, fill in ((c_blk, 1.0), (s1_blk, 0.0), (s2_blk, 0.0)):
        t = jnp.tile(blk, (1, reps))
        out.append(jnp.concatenate([jnp.full((ctx_len, V7X_LANES), fill, F32), t], axis=0))
    return out


def _block_diag(w):
    n, a, b = w.shape
    eye = jnp.eye(n, dtype=w.dtype)
    return (eye[:, None, :, None] * w[:, :, None, :]).reshape(n * a, n * b)


def kernel(x, c, ctx, c_ctx, w_mod, b_mod, norm1_g, norm2_g, w_in, conv_w, conv_b, lru_wa, lru_ba,
           lru_wi, lru_bi, lru_lambda, diff_lam, diff_subln_g, gqa_qnorm_g, gqa_knorm_g, mla_qnorm_g,
           mla_w_uq, mla_kvnorm_g, mla_w_ukv, w_branch, w_out, peer_wq, peer_keys, peer_u, peer_v,
           final_norm_g):
    bsz, seq, d = x.shape
    ctx_len = ctx.shape[1]
    depth = w_in.shape[0]
    assert ctx_len == ROW_TILE and seq % PEER_TOK == 0 and seq % GRID_W == 0
    rows = seq // GRID_W
    s_all = ctx_len + seq
    xs = jnp.concatenate([ctx, x], axis=1)

    tabs = _axial_tables(rows, DIFF_DK, ctx_len, 0) + _axial_tables(rows, MLA_ROPE, ctx_len, MLA_NOPE)
    grp = jnp.arange(BRANCH_W) // GQA_DH
    avg = (grp[:, None] == grp[None, :]).astype(BF16) * (1.0 / GQA_DH)
    sc_in = jnp.zeros((16, d), F32).at[:bsz].set(jax.nn.silu(c)).at[bsz].set(jax.nn.silu(c_ctx)).astype(BF16)
    zero1 = jnp.zeros((1,), F32)
    ones_g = jnp.ones((1, V7X_LANES), F32)

    for l in range(depth):
        lam_init = 0.8 - 0.6 * math.exp(-0.3 * l)
        mod_all = matmul(sc_in, w_mod[l].astype(BF16)) + b_mod[l]
        mod_b = mod_all[:bsz].reshape(bsz, N_MOD, d)
        mod_c = jnp.broadcast_to(mod_all[bsz].reshape(1, N_MOD, d), (bsz, N_MOD, d))
        mods = [jnp.stack([mod_c[:, k], mod_b[:, k]], axis=1)[:, :, None, :] for k in range(N_MOD)]

        wl = w_in[l]
        o = 0
        parts = []
        for w in (512, 512, 512, 512, 512, 512, 128, 128, 384, 256, 32, 4096):
            parts.append(wl[:, o:o + w])
            o += w
        xa, ya, qb, kb, vb, qg, kg, vg, cq, ckv, kr, zg = parts
        kr_blk = jnp.zeros((d, V7X_LANES), F32).at[:, MLA_NOPE:MLA_NOPE + MLA_ROPE].set(kr)
        w_cat = jnp.concatenate([zg, xa, ya, qb, kb, vb, qg, kg, vg, cq, ckv, kr_blk], axis=1).astype(BF16)

        h = adaln(xs, norm1_g[l][None], mods[0], mods[1], 0)
        z = matmul(h.reshape(bsz * s_all, d), w_cat, out_dtype=BF16, tm=1024, tn=1024)
        z = z.reshape(bsz, s_all, w_cat.shape[1])

        negc = (-LRU_C * jax.nn.softplus(-lru_lambda[l]))[:, None, None, :]
        wg = jnp.stack([jnp.concatenate([_block_diag(lru_wa[l, dd]), _block_diag(lru_wi[l, dd])], axis=1)
                        for dd in range(2)]).astype(BF16)[:, None]
        bg = jnp.stack([jnp.concatenate([lru_ba[l, dd], lru_bi[l, dd]]) for dd in range(2)])[:, None, None, :]
        cb = conv_b[l][None]
        hf = lru_scan(z, 8, 9, conv_w[l], cb, wg[0], bg[0], negc[0], None, reverse=False)
        oa = lru_scan(z, 8, 9, conv_w[l], cb, wg[1], bg[1], negc[1], hf, reverse=True)

        uq = mla_w_uq[l].reshape(MLA_Q_RANK, MLA_HEADS, MLA_NOPE + MLA_ROPE)
        wuq = jnp.pad(uq, ((0, 0), (0, 0), (0, V7X_LANES - MLA_NOPE - MLA_ROPE))).reshape(MLA_Q_RANK, -1)
        ukv = mla_w_ukv[l].reshape(MLA_KV_RANK, MLA_HEADS, MLA_NOPE + MLA_DV)
        wuk = jnp.pad(ukv[:, :, :MLA_NOPE], ((0, 0), (0, 0), (0, V7X_LANES - MLA_NOPE))).reshape(MLA_KV_RANK, -1)
        wuv = ukv[:, :, MLA_NOPE:].reshape(MLA_KV_RANK, -1)
        qd, kd, vd, qq, kq, vq, qm, km, vm = attn_prep(
            z, tabs, avg, jnp.tile(gqa_qnorm_g[l], GQA_HEADS)[None], jnp.tile(gqa_knorm_g[l], GQA_KV_HEADS)[None],
            mla_qnorm_g[l][None], mla_kvnorm_g[l][None], wuq.astype(BF16), wuk.astype(BF16), wuv.astype(BF16))
        lv = diff_lam[l]
        lam = (jnp.exp(jnp.sum(lv[0] * lv[1])) - jnp.exp(jnp.sum(lv[2] * lv[3])) + lam_init).reshape(1)
        ob = attention(qd, kd, vd, lam, diff_subln_g[l][None], mode="diff", out_scale=1.0 - lam_init)
        og = attention(qq, kq, vq, zero1, ones_g, mode="pair")
        om = attention(qm, km, vm, zero1, ones_g, mode="mla")

        xs = merge(oa, ob, og, om, z, w_branch[l].astype(BF16), w_out[l].astype(BF16), xs, mods[2])

        kbig = jnp.stack([_block_diag(peer_keys[l, hd]) for hd in range(PEER_HEADS)]).astype(BF16)
        h2, pa, pb, pe1, pe2 = peer_route(xs, norm2_g[l][None], mods[3], mods[4], peer_wq[l].astype(BF16), kbig)
        xs = peer_experts(h2, peer_u[l].astype(BF16), peer_v[l].T.astype(BF16), pa, pe1, pb, pe2, xs, mods[5],
                          final_norm_g[None], final=(l == depth - 1))
    return xs[:, ctx_len:]
```

```python
import functools
import math

import jax
import jax.numpy as jnp
from jax import lax
from jax.experimental import pallas as pl
from jax.experimental.pallas import tpu as pltpu

F32 = jnp.float32
BF16 = jnp.bfloat16

EPS = 1e-6
GRID_W = 64
ROPE_THETA = 10000.0
N_MOD = 6

D_RNN = 512
RNN_BLOCKS = 8
CONV_W = 4
LRU_C = 8.0

DIFF_HEADS = 4
DIFF_DK = 64
DIFF_DV = 128
GQA_HEADS = 8
GQA_KV_HEADS = 2
GQA_DH = 64
MLA_HEADS = 8
MLA_NOPE = 64
MLA_ROPE = 32
MLA_DV = 64
MLA_Q_RANK = 384
MLA_KV_RANK = 256
N_BRANCH = 4
BRANCH_W = 512

PEER_HEADS = 8
PEER_NKEYS = 128
PEER_DK = 128
PEER_TOPK = 16

V7X_LANES = 128
V7X_SUBLANES = 8
V7X_VMEM_BYTES = 64 * 1024 * 1024
V7X_MXU_DIM = 256
VMEM_LIMIT = 48 * 1024 * 1024

ROW_TILE = 256


def _cparams(sem):
    return pltpu.CompilerParams(dimension_semantics=sem, vmem_limit_bytes=VMEM_LIMIT)


def _adaln_body(x_ref, g_ref, sh_ref, sc_ref, o_ref):
    x = x_ref[0]
    ms = jnp.mean(x * x, axis=-1, keepdims=True)
    y = x * lax.rsqrt(ms + EPS) * g_ref[...]
    o_ref[0] = (y * (1.0 + sc_ref[0, 0]) + sh_ref[0, 0]).astype(o_ref.dtype)


def adaln(x, g, shift, scale, t0):
    b, s, d = x.shape
    nt = s // ROW_TILE - t0
    seg = lambda bi, ti: (bi, jnp.minimum(ti + t0, 1), 0, 0)
    return pl.pallas_call(
        _adaln_body,
        grid=(b, nt),
        in_specs=[
            pl.BlockSpec((1, ROW_TILE, d), lambda bi, ti: (bi, ti + t0, 0)),
            pl.BlockSpec((1, d), lambda bi, ti: (0, 0)),
            pl.BlockSpec((1, 1, 1, d), seg),
            pl.BlockSpec((1, 1, 1, d), seg),
        ],
        out_specs=pl.BlockSpec((1, ROW_TILE, d), lambda bi, ti: (bi, ti, 0)),
        out_shape=jax.ShapeDtypeStruct((b, nt * ROW_TILE, d), BF16),
        compiler_params=_cparams(("parallel", "parallel")),
        name="adaln",
    )(x, g, shift, scale)


def _mm_body(a_ref, w_ref, o_ref):
    o_ref[...] = jnp.dot(a_ref[...], w_ref[...], preferred_element_type=F32).astype(o_ref.dtype)


def matmul(a, w, out_dtype=F32, tm=512, tn=512):
    m, k = a.shape
    _, n = w.shape
    tm = math.gcd(tm, m)
    tn = math.gcd(tn, n)
    return pl.pallas_call(
        _mm_body,
        grid=(n // tn, m // tm),
        in_specs=[
            pl.BlockSpec((tm, k), lambda j, i: (i, 0)),
            pl.BlockSpec((k, tn), lambda j, i: (0, j)),
        ],
        out_specs=pl.BlockSpec((tm, tn), lambda j, i: (i, j)),
        out_shape=jax.ShapeDtypeStruct((m, n), out_dtype),
        compiler_params=_cparams(("parallel", "parallel")),
        name="matmul",
    )(a, w)


def _mult(x, m):
    return x if isinstance(x, int) else pl.multiple_of(x, m)


def _gelu(x):
    c = math.sqrt(2.0 / math.pi)
    return x * (0.5 * (1.0 + jnp.tanh(c * (x + 0.044715 * (x * x * x)))))


def _sigmoid(x):
    return 1.0 / (1.0 + jnp.exp(-x))


TILE_ROWS = V7X_SUBLANES
HALO_ROWS = 2 * V7X_SUBLANES


def _lru_body(x_ref, xp_ref, xn_ref, cw_ref, cb_ref, wg_ref, bg_ref, nc_ref, *rest,
              reverse, n_chunks):
    if reverse:
        ya_ref, hf_ref, o_ref, carry_ref = rest
    else:
        o_ref, carry_ref = rest
    step = pl.program_id(1)
    if reverse:
        chunk = jnp.where(step == 0, 0, n_chunks - step)
    else:
        chunk = step
    rows = ROW_TILE
    c = x_ref.shape[-1]

    @pl.when(step == 0)
    def _():
        carry_ref[...] = jnp.zeros_like(carry_ref)

    x = x_ref[0].astype(F32)
    xx = jnp.concatenate([xp_ref[0].astype(F32), x, xn_ref[0].astype(F32)], axis=0)
    n_xx = rows + 2 * HALO_ROWS
    row = lax.broadcasted_iota(jnp.int32, (rows, c), 0)
    seg_start = jnp.logical_or(chunk == 0, chunk == 1)
    seg_end = jnp.logical_or(chunk == 0, chunk == n_chunks - 1)

    def shifted(d):
        return pltpu.roll(xx, (n_xx - d) % n_xx, axis=0)[HALO_ROWS:HALO_ROWS + rows]

    x_m1 = jnp.where(jnp.logical_and(seg_start, row == 0), 0.0, shifted(-1))
    x_p1 = jnp.where(jnp.logical_and(seg_end, row >= rows - 1), 0.0, shifted(1))
    x_p2 = jnp.where(jnp.logical_and(seg_end, row >= rows - 2), 0.0, shifted(2))
    cw = cw_ref[...]
    u = cw[0:1] * x_m1 + cw[1:2] * x + cw[2:3] * x_p1 + cw[3:4] * x_p2 + cb_ref[...]

    z = jnp.dot(u.astype(BF16), wg_ref[0], preferred_element_type=F32) + bg_ref[0]
    r = _sigmoid(z[:, :c])
    gi = _sigmoid(z[:, c:])
    a = jnp.exp(nc_ref[0] * r)
    bv = jnp.sqrt(1.0 - a * a) * (gi * u)

    sub = row % TILE_ROWS
    for dstep in (1, 2, 4):
        if reverse:
            sh = (rows - dstep) % rows
            keep = sub <= TILE_ROWS - 1 - dstep
        else:
            sh = dstep
            keep = sub >= dstep
        a_sh = jnp.where(keep, pltpu.roll(a, sh, axis=0), 1.0)
        b_sh = jnp.where(keep, pltpu.roll(bv, sh, axis=0), 0.0)
        bv = a * b_sh + bv
        a = a * a_sh

    carry = carry_ref[...]
    n_tiles = rows // TILE_ROWS
    order = range(n_tiles - 1, -1, -1) if reverse else range(n_tiles)
    hs = [None] * n_tiles
    for t in order:
        sl = slice(t * TILE_ROWS, (t + 1) * TILE_ROWS)
        h = bv[sl] + a[sl] * carry
        hs[t] = h
        edge = h[0:1] if reverse else h[TILE_ROWS - 1:TILE_ROWS]
        carry = jnp.broadcast_to(edge, (TILE_ROWS, c))
    carry_ref[...] = carry
    h_all = jnp.concatenate(hs, axis=0)
    if reverse:
        o_ref[0] = (_gelu(ya_ref[0].astype(F32)) * (hf_ref[0] + h_all)).astype(o_ref.dtype)
    else:
        o_ref[0] = h_all


def lru_scan(z, xa_col, ya_col, conv_w, conv_b, wg, bg, negc, hf, *, reverse):
    b, s, _ = z.shape
    c = D_RNN
    n_chunks = s // ROW_TILE
    per = ROW_TILE // HALO_ROWS
    n8 = s // HALO_ROWS
    if reverse:
        cidx = lambda st: jnp.where(st == 0, 0, n_chunks - st)
    else:
        cidx = lambda st: st
    in_specs = [
        pl.BlockSpec((1, ROW_TILE, c), lambda bi, st: (bi, cidx(st), xa_col)),
        pl.BlockSpec((1, HALO_ROWS, c), lambda bi, st: (bi, jnp.maximum(cidx(st) * per - 1, 0), xa_col)),
        pl.BlockSpec((1, HALO_ROWS, c), lambda bi, st: (bi, jnp.minimum((cidx(st) + 1) * per, n8 - 1), xa_col)),
        pl.BlockSpec((CONV_W, c), lambda bi, st: (0, 0)),
        pl.BlockSpec((1, c), lambda bi, st: (0, 0)),
        pl.BlockSpec((1, c, 2 * c), lambda bi, st: (0, 0, 0)),
        pl.BlockSpec((1, 1, 2 * c), lambda bi, st: (0, 0, 0)),
        pl.BlockSpec((1, 1, c), lambda bi, st: (0, 0, 0)),
    ]
    args = [z, z, z, conv_w, conv_b, wg, bg, negc]
    if reverse:
        in_specs += [
            pl.BlockSpec((1, ROW_TILE, c), lambda bi, st: (bi, cidx(st), ya_col)),
            pl.BlockSpec((1, ROW_TILE, c), lambda bi, st: (bi, cidx(st), 0)),
        ]
        args += [z, hf]
    return pl.pallas_call(
        functools.partial(_lru_body, reverse=reverse, n_chunks=n_chunks),
        grid=(b, n_chunks),
        in_specs=in_specs,
        out_specs=pl.BlockSpec((1, ROW_TILE, c), lambda bi, st: (bi, cidx(st), 0)),
        out_shape=jax.ShapeDtypeStruct((b, s, c), BF16 if reverse else F32),
        scratch_shapes=[pltpu.VMEM((TILE_ROWS, c), F32)],
        compiler_params=_cparams(("parallel", "arbitrary")),
        name="lru_rev" if reverse else "lru_fwd",
    )(*args)


def _tile_lanes(t, n):
    return jnp.concatenate([t] * n, axis=1) if n > 1 else t


def _rope(x, c, s1, s2, r):
    w = x.shape[-1]
    n = w // V7X_LANES
    return (x * _tile_lanes(c, n) + pltpu.roll(x, w - r, axis=1) * _tile_lanes(s1, n)
            + pltpu.roll(x, r, axis=1) * _tile_lanes(s2, n))


def _split_dot(x, m_ref):
    hi = x.astype(BF16)
    lo = (x - hi.astype(F32)).astype(BF16)
    m = m_ref[...]
    return jnp.dot(hi, m, preferred_element_type=F32) + jnp.dot(lo, m, preferred_element_type=F32)


def _rms(x, g):
    return x * lax.rsqrt(jnp.mean(x * x, axis=-1, keepdims=True) + EPS) * g


def _prep_body(qb_ref, kb_ref, vb_ref, qg_ref, tail_ref,
               c64_ref, s164_ref, s264_ref, cm_ref, s1m_ref, s2m_ref,
               avg_ref, gq_ref, gk_ref, gcq_ref, gckv_ref, wuq_ref, wuk_ref, wuv_ref,
               qd_ref, kd_ref, vd_ref, qq_ref, kq_ref, vq_ref, qm_ref, km_ref, vm_ref):
    c64, s164, s264 = c64_ref[...], s164_ref[...], s264_ref[...]
    cm, s1m, s2m = cm_ref[...], s1m_ref[...], s2m_ref[...]
    half64 = DIFF_DK // 2
    qd_ref[0] = (_rope(qb_ref[0].astype(F32), c64, s164, s264, half64) * (DIFF_DK ** -0.5)).astype(BF16)
    kd_ref[0] = _rope(kb_ref[0].astype(F32), c64, s164, s264, half64).astype(BF16)
    vd_ref[0] = vb_ref[0].astype(BF16)
    qg = qg_ref[0].astype(F32)
    ms = _split_dot(qg * qg, avg_ref)
    qn = qg * lax.rsqrt(ms + EPS) * gq_ref[...]
    qq_ref[0] = (_rope(qn, c64, s164, s264, GQA_DH // 2) * (GQA_DH ** -0.5)).astype(BF16)
    tail = tail_ref[0].astype(F32)
    kg = tail[:, 0:128]
    msk = _split_dot(kg * kg, avg_ref.at[0:128, 0:128])
    kn = _rope(kg * lax.rsqrt(msk + EPS) * gk_ref[...], c64, s164, s264, GQA_DH // 2).astype(BF16)
    vg = tail[:, 128:256].astype(BF16)
    kq_ref[0] = jnp.concatenate([kn[:, 0:64], kn[:, 0:64], kn[:, 64:128], kn[:, 64:128]], axis=1)
    vq_ref[0] = jnp.concatenate([vg[:, 0:64], vg[:, 0:64], vg[:, 64:128], vg[:, 64:128]], axis=1)
    cq = tail[:, 256:256 + MLA_Q_RANK]
    ckv = tail[:, 640:640 + MLA_KV_RANK]
    kr = tail[:, 896:1024]
    qf = jnp.dot(_rms(cq, gcq_ref[...]).astype(BF16), wuq_ref[...], preferred_element_type=F32)
    scale = (MLA_NOPE + MLA_ROPE) ** -0.5
    qm_ref[0] = (_rope(qf, cm, s1m, s2m, MLA_ROPE // 2) * scale).astype(BF16)
    ckvn = _rms(ckv, gckv_ref[...]).astype(BF16)
    kf = jnp.dot(ckvn, wuk_ref[...], preferred_element_type=F32)
    krr = _rope(kr, cm, s1m, s2m, MLA_ROPE // 2)
    km_ref[0] = (kf + _tile_lanes(krr, MLA_HEADS)).astype(BF16)
    vm_ref[0] = jnp.dot(ckvn, wuv_ref[...], preferred_element_type=F32).astype(BF16)


def attn_prep(z, tabs, avg, gq, gk, gcq, gckv, wuq, wuk, wuv):
    b, s, _ = z.shape
    nt = s // ROW_TILE
    zspec = lambda w, idx: pl.BlockSpec((1, ROW_TILE, w), lambda ti, bi: (bi, ti, idx))
    tab = pl.BlockSpec((ROW_TILE, V7X_LANES), lambda ti, bi: (ti, 0))
    full = lambda a: pl.BlockSpec(a.shape, lambda ti, bi: (0,) * a.ndim)
    ospec = lambda w: pl.BlockSpec((1, ROW_TILE, w), lambda ti, bi: (bi, ti, 0))
    oshape = lambda w: jax.ShapeDtypeStruct((b, s, w), BF16)
    widths = (512, 512, 512, 512, 256, 256, 1024, 1024, 512)
    return pl.pallas_call(
        _prep_body,
        grid=(nt, b),
        in_specs=[zspec(512, 10), zspec(512, 11), zspec(512, 12), zspec(512, 13), zspec(1024, 7)]
        + [tab] * 6 + [full(a) for a in (avg, gq, gk, gcq, gckv, wuq, wuk, wuv)],
        out_specs=[ospec(w) for w in widths],
        out_shape=[oshape(w) for w in widths],
        compiler_params=_cparams(("parallel", "parallel")),
        name="attn_prep",
    )(z, z, z, z, z, *tabs, avg, gq, gk, gcq, gckv, wuq, wuk, wuv)


ATTN_GROUPS = 2


def _attn_body(lam_ref, q_ref, k_ref, v_ref, g_ref, o_ref, *, mode, out_scale):
    ti = pl.program_id(2)
    lane = lax.broadcasted_iota(jnp.int32, (ROW_TILE, V7X_LANES), 1)
    lo_half = lane < (V7X_LANES // 2)

    ln = V7X_LANES

    def run(n_keys):
        for grp in range(ATTN_GROUPS):
            outs = []
            for j in range(2):
                if mode == "mla":
                    c0 = (2 * grp + j) * ln
                    qj = q_ref[0, :, c0:c0 + ln]
                    kj = k_ref[0, 0:n_keys, c0:c0 + ln]
                else:
                    q = q_ref[0, :, grp * ln:(grp + 1) * ln]
                    keep = lo_half if j == 0 else jnp.logical_not(lo_half)
                    qj = jnp.where(keep, q.astype(F32), 0.0).astype(BF16)
                    kc = 0 if mode == "pair" else grp * ln
                    kj = k_ref[0, 0:n_keys, kc:kc + ln]
                vc = 0 if mode == "pair" else grp * ln
                s = lax.dot_general(qj, kj, (((1,), (1,)), ((), ())), preferred_element_type=F32)
                m = jnp.max(s, axis=-1, keepdims=True)
                p = jnp.exp(s - m)
                l = jnp.sum(p, axis=-1, keepdims=True)
                o = jnp.dot(p.astype(BF16), v_ref[0, 0:n_keys, vc:vc + ln], preferred_element_type=F32)
                outs.append(o / l)
            if mode == "diff":
                o = outs[0] - lam_ref[0] * outs[1]
                o = o * lax.rsqrt(jnp.mean(o * o, axis=-1, keepdims=True) + EPS) * g_ref[...] * out_scale
            else:
                o = jnp.where(lo_half, outs[0], outs[1])
            o_ref[0, :, grp * ln:(grp + 1) * ln] = o.astype(o_ref.dtype)

    @pl.when(ti == 0)
    def _():
        run(ROW_TILE)

    @pl.when(ti > 0)
    def _():
        run(k_ref.shape[1])


def attention(q, k, v, lam, g, *, mode, out_scale=1.0):
    b, s, wq = q.shape
    steps = 4 // ATTN_GROUPS
    qw = wq // steps
    kw = k.shape[-1] // steps
    vw = v.shape[-1] // steps
    ow = ATTN_GROUPS * V7X_LANES
    nt = s // ROW_TILE
    return pl.pallas_call(
        functools.partial(_attn_body, mode=mode, out_scale=out_scale),
        grid=(b, steps, nt),
        in_specs=[
            pl.BlockSpec(memory_space=pltpu.SMEM),
            pl.BlockSpec((1, ROW_TILE, qw), lambda bi, gi, ti: (bi, ti, gi)),
            pl.BlockSpec((1, s, kw), lambda bi, gi, ti: (bi, 0, gi)),
            pl.BlockSpec((1, s, vw), lambda bi, gi, ti: (bi, 0, gi)),
            pl.BlockSpec((1, V7X_LANES), lambda bi, gi, ti: (0, 0)),
        ],
        out_specs=pl.BlockSpec((1, ROW_TILE, ow), lambda bi, gi, ti: (bi, ti, gi)),
        out_shape=jax.ShapeDtypeStruct((b, s, 4 * V7X_LANES), BF16),
        compiler_params=_cparams(("parallel", "parallel", "arbitrary")),
        name="attn_" + mode,
    )(lam, q, k, v, g)


def _merge_body(oa_ref, ob_ref, og_ref, om_ref, zg_ref, wb_ref, wo_ref, x_ref, gate_ref, o_ref):
    d = x_ref.shape[-1]
    merged = None
    for k, o_k in enumerate((oa_ref, ob_ref, og_ref, om_ref)):
        t = jnp.dot(o_k[0], wb_ref[k], preferred_element_type=F32)
        t = t * _sigmoid(zg_ref[0, :, k * d:(k + 1) * d].astype(F32))
        merged = t if merged is None else merged + t
    y = jnp.dot(merged.astype(BF16), wo_ref[...], preferred_element_type=F32)
    o_ref[0] = x_ref[0] + gate_ref[0, 0] * y


def merge(oa, ob, og, om, z, wb, wo, x, gate):
    b, s, d = x.shape
    nt = s // ROW_TILE
    bspec = pl.BlockSpec((1, ROW_TILE, BRANCH_W), lambda bi, ti: (bi, ti, 0))
    return pl.pallas_call(
        _merge_body,
        grid=(b, nt),
        in_specs=[bspec, bspec, bspec, bspec,
                  pl.BlockSpec((1, ROW_TILE, N_BRANCH * d), lambda bi, ti: (bi, ti, 0)),
                  pl.BlockSpec(wb.shape, lambda bi, ti: (0, 0, 0)),
                  pl.BlockSpec(wo.shape, lambda bi, ti: (0, 0)),
                  pl.BlockSpec((1, ROW_TILE, d), lambda bi, ti: (bi, ti, 0)),
                  pl.BlockSpec((1, 1, 1, d), lambda bi, ti: (bi, jnp.minimum(ti, 1), 0, 0))],
        out_specs=pl.BlockSpec((1, ROW_TILE, d), lambda bi, ti: (bi, ti, 0)),
        out_shape=jax.ShapeDtypeStruct((b, s, d), F32),
        compiler_params=_cparams(("parallel", "parallel")),
        name="merge",
    )(oa, ob, og, om, z, wb, wo, x, gate)


NEG_BIG = float(jnp.finfo(jnp.float32).min)


def _route_body(x_ref, g_ref, sh_ref, sc_ref, wq_ref, kb_ref,
                h_ref, a_ref, b_ref, e1_ref, e2_ref, st_ref, ta_ref, tb_ref):
    x = x_ref[0]
    ms = jnp.mean(x * x, axis=-1, keepdims=True)
    hf = x * lax.rsqrt(ms + EPS) * g_ref[...] * (1.0 + sc_ref[0, 0]) + sh_ref[0, 0]
    h_ref[0] = hf.T.astype(BF16)
    h = hf.astype(BF16)
    q = jnp.dot(h, wq_ref[...], preferred_element_type=F32).astype(BF16)
    nk = PEER_NKEYS
    for hd in range(PEER_HEADS):
        st_ref[hd * 2 * nk:(hd + 1) * 2 * nk, :] = lax.dot_general(
            kb_ref[hd], q[:, hd * PEER_DK:(hd + 1) * PEER_DK], (((1,), (1,)), ((), ())),
            preferred_element_type=F32)

    def top_rows(v, dst_ref):
        for r in range(PEER_TOPK):
            m = jnp.max(v, axis=0, keepdims=True)
            dst_ref[r:r + 1, :] = m
            v = jnp.where(v >= m, NEG_BIG, v)

    row8 = lax.broadcasted_iota(jnp.int32, (TILE_ROWS, ROW_TILE), 0)

    def head(hd, carry):
        base = pl.multiple_of(hd * (2 * nk), 2 * nk)
        s1 = st_ref[pl.ds(base, nk), :]
        s2 = st_ref[pl.ds(base + nk, nk), :]
        top_rows(s1, ta_ref)
        top_rows(s2, tb_ref)
        pieces = [ta_ref[0:1, :] + tb_ref[...]]
        tb8 = tb_ref[0:TILE_ROWS, :]
        for p in range(1, PEER_TOPK):
            n_q = PEER_TOPK // (p + 1)
            c = ta_ref[p:p + 1, :] + tb8
            pieces.append(c if n_q >= TILE_ROWS else jnp.where(row8 < n_q, c, NEG_BIG))
        cand = jnp.concatenate(pieces, axis=0)
        top = ta_ref[0:1, :] + tb_ref[0:1, :]
        zsum = jnp.zeros_like(top)
        m = top
        for r in range(PEER_TOPK):
            m = jnp.max(cand, axis=0, keepdims=True)
            zsum = zsum + jnp.exp(m - top)
            cand = jnp.where(cand >= m, NEG_BIG, cand)
        a_ref[hd] = m - s1
        e1_ref[hd] = jnp.exp(s1 - ta_ref[0:1, :]) / zsum
        e2 = jnp.exp(s2 - tb_ref[0:1, :])
        for lt in range(ROW_TILE // V7X_LANES):
            b_ref[hd, lt] = s2[:, lt * V7X_LANES:(lt + 1) * V7X_LANES]
            e2_ref[hd, lt] = e2[:, lt * V7X_LANES:(lt + 1) * V7X_LANES]
        return carry

    lax.fori_loop(0, PEER_HEADS, head, 0)


def peer_route(x, g, shift, scale, wq, kbig):
    b, s, d = x.shape
    nt = s // ROW_TILE
    m = b * s
    seg = lambda bi, ti: (bi, jnp.minimum(ti, 1), 0, 0)
    flat = lambda bi, ti: (0, 0, bi * nt + ti)
    kspec = pl.BlockSpec((PEER_HEADS, PEER_NKEYS, ROW_TILE), flat)
    kshape = jax.ShapeDtypeStruct((PEER_HEADS, PEER_NKEYS, m), F32)
    lt_per = ROW_TILE // V7X_LANES
    tspec = pl.BlockSpec((PEER_HEADS, lt_per, PEER_NKEYS, V7X_LANES), lambda bi, ti: (0, bi * nt + ti, 0, 0))
    tshape = jax.ShapeDtypeStruct((PEER_HEADS, m // V7X_LANES, PEER_NKEYS, V7X_LANES), F32)
    return pl.pallas_call(
        _route_body,
        grid=(b, nt),
        in_specs=[
            pl.BlockSpec((1, ROW_TILE, d), lambda bi, ti: (bi, ti, 0)),
            pl.BlockSpec((1, d), lambda bi, ti: (0, 0)),
            pl.BlockSpec((1, 1, 1, d), seg),
            pl.BlockSpec((1, 1, 1, d), seg),
            pl.BlockSpec(wq.shape, lambda bi, ti: (0, 0)),
            pl.BlockSpec(kbig.shape, lambda bi, ti: (0, 0, 0)),
        ],
        out_specs=[pl.BlockSpec((1, d, ROW_TILE), lambda bi, ti: (bi * nt + ti, 0, 0)),
                   kspec, tspec, kspec, tspec],
        out_shape=[jax.ShapeDtypeStruct((m // ROW_TILE, d, ROW_TILE), BF16), kshape, tshape, kshape, tshape],
        scratch_shapes=[pltpu.VMEM((2 * PEER_HEADS * PEER_NKEYS, ROW_TILE), F32),
                        pltpu.VMEM((PEER_TOPK, ROW_TILE), F32),
                        pltpu.VMEM((PEER_TOPK, ROW_TILE), F32)],
        compiler_params=_cparams(("parallel", "parallel")),
        name="peer_route",
    )(x, g, shift, scale, wq, kbig)


PEER_TOK = 2 * ROW_TILE
PEER_ECH = 1024


def _expert_body(h_ref, u0_ref, u_ref, vt_ref, a_ref, e1_ref, b_ref, e2_ref, x_ref, g0_ref, g1_ref,
                 fg_ref, o_ref, acc_ref, w0_ref, w1_ref, uf0_ref, uf1_ref, thb_ref, e1b_ref, *,
                 final, n_ch):
    e = pl.program_id(1)
    nk = PEER_NKEYS
    n_il = PEER_ECH // nk
    sub = TILE_ROWS
    ib = 2
    n_ib = n_il // ib
    tn = (((1,), (1,)), ((), ()))
    mq = V7X_MXU_DIM
    uf_refs = (uf0_ref, uf1_ref)
    w_refs = (w0_ref, w1_ref)

    def gate_block(g, cur):
        uf_ref, w_ref = uf_refs[cur], w_refs[cur]
        lt = g // n_ib
        i0 = (g % n_ib) * ib
        gates = [None] * ib
        for hd in range(PEER_HEADS):
            bj = b_ref[hd, lt].reshape(nk // sub, sub, V7X_LANES)
            e2j = e2_ref[hd, lt].reshape(nk // sub, sub, V7X_LANES)
            for k in range(ib):
                th = thb_ref[lt, hd * n_il + i0 + k][None]
                e1 = e1b_ref[lt, hd * n_il + i0 + k][None]
                t = jnp.where(bj >= th, e1 * e2j, 0.0)
                gates[k] = t if gates[k] is None else gates[k] + t
        lpm = mq // V7X_LANES
        wl = pl.ds(_mult((lt % lpm) * V7X_LANES, V7X_LANES), V7X_LANES)
        for k in range(ib):
            rows = pl.ds(_mult((i0 + k) * nk, nk), nk)
            act = _gelu(uf_ref[lt, rows, :]) * gates[k].reshape(nk, V7X_LANES)
            w_ref[lt // lpm, rows, wl] = act.astype(BF16)

    def k_major(ref, r0):
        return jnp.concatenate([ref[k, pl.ds(r0, mq), :] for k in range(ref.shape[0])], axis=1)

    def next_u(it, oth):
        r0 = _mult((it // 2) * mq, mq)
        res = jnp.dot(k_major(u_ref, r0), h_ref[it % 2], preferred_element_type=F32)
        for k in range(mq // V7X_LANES):
            uf_refs[oth][(it % 2) * (mq // V7X_LANES) + k, pl.ds(r0, mq), :] = (
                res[:, k * V7X_LANES:(k + 1) * V7X_LANES])

    def prev_out(it, oth):
        r0 = _mult((it // 2) * mq, mq)
        acc_ref[it % 2, pl.ds(r0, mq), :] += jnp.dot(
            k_major(vt_ref, r0), w_refs[oth][it % 2], preferred_element_type=F32)

    n_it = (PEER_ECH // mq) * (PEER_TOK // mq)
    per_it = (PEER_TOK // V7X_LANES) * n_ib // n_it

    def run_chunk(with_prev, cur):
        for lt in range(PEER_TOK // V7X_LANES):
            tl = slice(lt * V7X_LANES, (lt + 1) * V7X_LANES)
            for hd in range(PEER_HEADS):
                for il in range(n_il):
                    thb_ref[lt, hd * n_il + il] = jnp.broadcast_to(a_ref[hd, il:il + 1, tl], (sub, V7X_LANES))
                    e1b_ref[lt, hd * n_il + il] = jnp.broadcast_to(e1_ref[hd, il:il + 1, tl], (sub, V7X_LANES))

        def step(it, carry):
            for k in range(per_it):
                gate_block(it * per_it + k, cur)
            next_u(it, 1 - cur)
            if with_prev:
                prev_out(it, 1 - cur)
            return carry

        for it in range(n_it):
            step(it, 0)

    @pl.when(e == 0)
    def _():
        acc_ref[...] = jnp.zeros_like(acc_ref)
        u_all = jnp.concatenate([u0_ref[k] for k in range(u0_ref.shape[0])], axis=1)
        lpm = mq // V7X_LANES
        for c in range(PEER_TOK // mq):
            u_first = jnp.dot(u_all, h_ref[c], preferred_element_type=F32)
            for k in range(lpm):
                uf0_ref[c * lpm + k] = u_first[:, k * V7X_LANES:(k + 1) * V7X_LANES]
        run_chunk(False, 0)

    for par in range(2):
        @pl.when(jnp.logical_and(jnp.logical_and(e > 0, e < n_ch), e % 2 == par))
        def _():
            run_chunk(True, par)

    @pl.when(e == n_ch)
    def _():
        vt_all = jnp.concatenate([vt_ref[k] for k in range(vt_ref.shape[0])], axis=1)
        for half, gref in enumerate((g0_ref, g1_ref)):
            y = acc_ref[half] + jnp.dot(vt_all, w_refs[(n_ch - 1) % 2][half], preferred_element_type=F32)
            sl = slice(half * ROW_TILE, (half + 1) * ROW_TILE)
            xn = x_ref[sl, :] + gref[0, 0] * y.T
            if final:
                xn = xn * lax.rsqrt(jnp.mean(xn * xn, axis=-1, keepdims=True) + EPS) * fg_ref[...]
            o_ref[sl, :] = xn


def peer_experts(h, u_tab, vt_tab, a, e1, bm, e2, x, gate, fg, *, final):
    b, s, d = x.shape
    nt = s // ROW_TILE
    m = b * s
    n_exp = u_tab.shape[1]
    i_per = PEER_ECH // PEER_NKEYS

    def gidx(half):
        def f(i, e):
            t = 2 * i + half
            return (t // nt, jnp.minimum(t % nt, 1), 0, 0)
        return f

    n_ch = n_exp // PEER_ECH
    last = n_ch - 1
    mq = V7X_MXU_DIM
    n_lt = PEER_TOK // V7X_LANES
    assert mq == ROW_TILE and PEER_TOK == 2 * mq
    out = pl.pallas_call(
        functools.partial(_expert_body, final=final, n_ch=n_ch),
        grid=(m // PEER_TOK, n_ch + 1),
        in_specs=[
            pl.BlockSpec((PEER_TOK // mq, d, mq), lambda i, e: (i, 0, 0)),
            pl.BlockSpec((d // mq, PEER_ECH, mq), lambda i, e: (0, 0, 0)),
            pl.BlockSpec((d // mq, PEER_ECH, mq), lambda i, e: (0, jnp.minimum(e + 1, last), 0)),
            pl.BlockSpec((PEER_ECH // mq, d, mq), lambda i, e: (jnp.maximum(e - 1, 0), 0, 0)),
            pl.BlockSpec((PEER_HEADS, i_per, PEER_TOK), lambda i, e: (0, jnp.minimum(e, last), i)),
            pl.BlockSpec((PEER_HEADS, i_per, PEER_TOK), lambda i, e: (0, jnp.minimum(e, last), i)),
            pl.BlockSpec((PEER_HEADS, PEER_TOK // V7X_LANES, PEER_NKEYS, V7X_LANES), lambda i, e: (0, i, 0, 0)),
            pl.BlockSpec((PEER_HEADS, PEER_TOK // V7X_LANES, PEER_NKEYS, V7X_LANES), lambda i, e: (0, i, 0, 0)),
            pl.BlockSpec((PEER_TOK, d), lambda i, e: (i, 0)),
            pl.BlockSpec((1, 1, 1, d), gidx(0)),
            pl.BlockSpec((1, 1, 1, d), gidx(1)),
            pl.BlockSpec((1, d), lambda i, e: (0, 0)),
        ],
        out_specs=pl.BlockSpec((PEER_TOK, d), lambda i, e: (i, 0)),
        out_shape=jax.ShapeDtypeStruct((m, d), F32),
        scratch_shapes=[pltpu.VMEM((PEER_TOK // mq, d, mq), F32),
                        pltpu.VMEM((PEER_TOK // mq, PEER_ECH, mq), BF16),
                        pltpu.VMEM((PEER_TOK // mq, PEER_ECH, mq), BF16),
                        pltpu.VMEM((n_lt, PEER_ECH, V7X_LANES), F32),
                        pltpu.VMEM((n_lt, PEER_ECH, V7X_LANES), F32),
                        pltpu.VMEM((n_lt, PEER_HEADS * i_per, TILE_ROWS, V7X_LANES), F32),
                        pltpu.VMEM((n_lt, PEER_HEADS * i_per, TILE_ROWS, V7X_LANES), F32)],
        compiler_params=_cparams(("parallel", "arbitrary")),
        name="peer_experts",
    )(h, u_tab, u_tab, vt_tab, a, e1, bm, e2, x.reshape(m, d), gate, gate, fg)
    return out.reshape(b, s, d)


def _axial_tables(rows, rope_dim, ctx_len, lane0):
    n_freq = rope_dim // 4
    half = rope_dim // 2
    inv_freq = ROPE_THETA ** (-jnp.arange(n_freq, dtype=F32) / n_freq)
    r = jnp.repeat(jnp.arange(rows, dtype=F32), GRID_W)
    col = jnp.tile(jnp.arange(GRID_W, dtype=F32), rows)
    ang = jnp.concatenate([r[:, None] * inv_freq, col[:, None] * inv_freq], axis=-1)
    cos, sin = jnp.cos(ang), jnp.sin(ang)
    n = cos.shape[0]
    period = 64 if rope_dim == 64 else V7X_LANES
    c_blk = jnp.ones((n, period), F32).at[:, lane0:lane0 + rope_dim].set(jnp.concatenate([cos, cos], -1))
    s1_blk = jnp.zeros((n, period), F32).at[:, lane0:lane0 + half].set(-sin)
    s2_blk = jnp.zeros((n, period), F32).at[:, lane0 + half:lane0 + rope_dim].set(sin)
    reps = V7X_LANES // period
    out = []
    for blk, fill in ((c_blk, 1.0), (s1_blk, 0.0), (s2_blk, 0.0)):
        t = jnp.tile(blk, (1, reps))
        out.append(jnp.concatenate([jnp.full((ctx_len, V7X_LANES), fill, F32), t], axis=0))
    return out


def _block_diag(w):
    n, a, b = w.shape
    eye = jnp.eye(n, dtype=w.dtype)
    return (eye[:, None, :, None] * w[:, :, None, :]).reshape(n * a, n * b)


def kernel(x, c, ctx, c_ctx, w_mod, b_mod, norm1_g, norm2_g, w_in, conv_w, conv_b, lru_wa, lru_ba,
           lru_wi, lru_bi, lru_lambda, diff_lam, diff_subln_g, gqa_qnorm_g, gqa_knorm_g, mla_qnorm_g,
           mla_w_uq, mla_kvnorm_g, mla_w_ukv, w_branch, w_out, peer_wq, peer_keys, peer_u, peer_v,
           final_norm_g):
    bsz, seq, d = x.shape
    ctx_len = ctx.shape[1]
    depth = w_in.shape[0]
    assert ctx_len == ROW_TILE and seq % PEER_TOK == 0 and seq % GRID_W == 0
    rows = seq // GRID_W
    s_all = ctx_len + seq
    xs = jnp.concatenate([ctx, x], axis=1)

    tabs = _axial_tables(rows, DIFF_DK, ctx_len, 0) + _axial_tables(rows, MLA_ROPE, ctx_len, MLA_NOPE)
    grp = jnp.arange(BRANCH_W) // GQA_DH
    avg = (grp[:, None] == grp[None, :]).astype(BF16) * (1.0 / GQA_DH)
    sc_in = jnp.zeros((16, d), F32).at[:bsz].set(jax.nn.silu(c)).at[bsz].set(jax.nn.silu(c_ctx)).astype(BF16)
    zero1 = jnp.zeros((1,), F32)
    ones_g = jnp.ones((1, V7X_LANES), F32)

    for l in range(depth):
        lam_init = 0.8 - 0.6 * math.exp(-0.3 * l)
        mod_all = matmul(sc_in, w_mod[l].astype(BF16)) + b_mod[l]
        mod_b = mod_all[:bsz].reshape(bsz, N_MOD, d)
        mod_c = jnp.broadcast_to(mod_all[bsz].reshape(1, N_MOD, d), (bsz, N_MOD, d))
        mods = [jnp.stack([mod_c[:, k], mod_b[:, k]], axis=1)[:, :, None, :] for k in range(N_MOD)]

        wl = w_in[l]
        o = 0
        parts = []
        for w in (512, 512, 512, 512, 512, 512, 128, 128, 384, 256, 32, 4096):
            parts.append(wl[:, o:o + w])
            o += w
        xa, ya, qb, kb, vb, qg, kg, vg, cq, ckv, kr, zg = parts
        kr_blk = jnp.zeros((d, V7X_LANES), F32).at[:, MLA_NOPE:MLA_NOPE + MLA_ROPE].set(kr)
        w_cat = jnp.concatenate([zg, xa, ya, qb, kb, vb, qg, kg, vg, cq, ckv, kr_blk], axis=1).astype(BF16)

        h = adaln(xs, norm1_g[l][None], mods[0], mods[1], 0)
        z = matmul(h.reshape(bsz * s_all, d), w_cat, out_dtype=BF16, tm=1024, tn=1024)
        z = z.reshape(bsz, s_all, w_cat.shape[1])

        negc = (-LRU_C * jax.nn.softplus(-lru_lambda[l]))[:, None, None, :]
        wg = jnp.stack([jnp.concatenate([_block_diag(lru_wa[l, dd]), _block_diag(lru_wi[l, dd])], axis=1)
                        for dd in range(2)]).astype(BF16)[:, None]
        bg = jnp.stack([jnp.concatenate([lru_ba[l, dd], lru_bi[l, dd]]) for dd in range(2)])[:, None, None, :]
        cb = conv_b[l][None]
        hf = lru_scan(z, 8, 9, conv_w[l], cb, wg[0], bg[0], negc[0], None, reverse=False)
        oa = lru_scan(z, 8, 9, conv_w[l], cb, wg[1], bg[1], negc[1], hf, reverse=True)

        uq = mla_w_uq[l].reshape(MLA_Q_RANK, MLA_HEADS, MLA_NOPE + MLA_ROPE)
        wuq = jnp.pad(uq, ((0, 0), (0, 0), (0, V7X_LANES - MLA_NOPE - MLA_ROPE))).reshape(MLA_Q_RANK, -1)
        ukv = mla_w_ukv[l].reshape(MLA_KV_RANK, MLA_HEADS, MLA_NOPE + MLA_DV)
        wuk = jnp.pad(ukv[:, :, :MLA_NOPE], ((0, 0), (0, 0), (0, V7X_LANES - MLA_NOPE))).reshape(MLA_KV_RANK, -1)
        wuv = ukv[:, :, MLA_NOPE:].reshape(MLA_KV_RANK, -1)
        qd, kd, vd, qq, kq, vq, qm, km, vm = attn_prep(
            z, tabs, avg, jnp.tile(gqa_qnorm_g[l], GQA_HEADS)[None], jnp.tile(gqa_knorm_g[l], GQA_KV_HEADS)[None],
            mla_qnorm_g[l][None], mla_kvnorm_g[l][None], wuq.astype(BF16), wuk.astype(BF16), wuv.astype(BF16))
        lv = diff_lam[l]
        lam = (jnp.exp(jnp.sum(lv[0] * lv[1])) - jnp.exp(jnp.sum(lv[2] * lv[3])) + lam_init).reshape(1)
        ob = attention(qd, kd, vd, lam, diff_subln_g[l][None], mode="diff", out_scale=1.0 - lam_init)
        og = attention(qq, kq, vq, zero1, ones_g, mode="pair")
        om = attention(qm, km, vm, zero1, ones_g, mode="mla")

        xs = merge(oa, ob, og, om, z, w_branch[l].astype(BF16), w_out[l].astype(BF16), xs, mods[2])

        kbig = jnp.stack([_block_diag(peer_keys[l, hd]) for hd in range(PEER_HEADS)]).astype(BF16)
        h2, pa, pb, pe1, pe2 = peer_route(xs, norm2_g[l][None], mods[3], mods[4], peer_wq[l].astype(BF16), kbig)
        n_exp = peer_u.shape[1]
        u_km = peer_u[l].astype(BF16).reshape(n_exp, d // V7X_MXU_DIM, V7X_MXU_DIM).transpose(1, 0, 2)
        vt_km = peer_v[l].astype(BF16).reshape(n_exp // V7X_MXU_DIM, V7X_MXU_DIM, d).transpose(0, 2, 1)
        xs = peer_experts(h2, u_km, vt_km, pa, pe1, pb, pe2, xs, mods[5],
                          final_norm_g[None], final=(l == depth - 1))
    return xs[:, ctx_len:]
```

```python
import functools
import math

import jax
import jax.numpy as jnp
from jax import lax
from jax.experimental import pallas as pl
from jax.experimental.pallas import tpu as pltpu

F32 = jnp.float32
BF16 = jnp.bfloat16

EPS = 1e-6
GRID_W = 64
ROPE_THETA = 10000.0
N_MOD = 6

D_RNN = 512
RNN_BLOCKS = 8
CONV_W = 4
LRU_C = 8.0

DIFF_HEADS = 4
DIFF_DK = 64
DIFF_DV = 128
GQA_HEADS = 8
GQA_KV_HEADS = 2
GQA_DH = 64
MLA_HEADS = 8
MLA_NOPE = 64
MLA_ROPE = 32
MLA_DV = 64
MLA_Q_RANK = 384
MLA_KV_RANK = 256
N_BRANCH = 4
BRANCH_W = 512

PEER_HEADS = 8
PEER_NKEYS = 128
PEER_DK = 128
PEER_TOPK = 16

V7X_LANES = 128
V7X_SUBLANES = 8
V7X_VMEM_BYTES = 64 * 1024 * 1024
V7X_MXU_DIM = 256
VMEM_LIMIT = 48 * 1024 * 1024

ROW_TILE = 256


def _cparams(sem):
    return pltpu.CompilerParams(dimension_semantics=sem, vmem_limit_bytes=VMEM_LIMIT)


def _adaln_body(x_ref, g_ref, sh_ref, sc_ref, o_ref):
    x = x_ref[0]
    ms = jnp.mean(x * x, axis=-1, keepdims=True)
    y = x * lax.rsqrt(ms + EPS) * g_ref[...]
    o_ref[0] = (y * (1.0 + sc_ref[0, 0]) + sh_ref[0, 0]).astype(o_ref.dtype)


def adaln(x, g, shift, scale, t0):
    b, s, d = x.shape
    nt = s // ROW_TILE - t0
    seg = lambda bi, ti: (bi, jnp.minimum(ti + t0, 1), 0, 0)
    return pl.pallas_call(
        _adaln_body,
        grid=(b, nt),
        in_specs=[
            pl.BlockSpec((1, ROW_TILE, d), lambda bi, ti: (bi, ti + t0, 0)),
            pl.BlockSpec((1, d), lambda bi, ti: (0, 0)),
            pl.BlockSpec((1, 1, 1, d), seg),
            pl.BlockSpec((1, 1, 1, d), seg),
        ],
        out_specs=pl.BlockSpec((1, ROW_TILE, d), lambda bi, ti: (bi, ti, 0)),
        out_shape=jax.ShapeDtypeStruct((b, nt * ROW_TILE, d), BF16),
        compiler_params=_cparams(("parallel", "parallel")),
        name="adaln",
    )(x, g, shift, scale)


def _mm_body(a_ref, w_ref, o_ref):
    o_ref[...] = jnp.dot(a_ref[...], w_ref[...], preferred_element_type=F32).astype(o_ref.dtype)


def matmul(a, w, out_dtype=F32, tm=512, tn=512):
    m, k = a.shape
    _, n = w.shape
    tm = math.gcd(tm, m)
    tn = math.gcd(tn, n)
    return pl.pallas_call(
        _mm_body,
        grid=(n // tn, m // tm),
        in_specs=[
            pl.BlockSpec((tm, k), lambda j, i: (i, 0)),
            pl.BlockSpec((k, tn), lambda j, i: (0, j)),
        ],
        out_specs=pl.BlockSpec((tm, tn), lambda j, i: (i, j)),
        out_shape=jax.ShapeDtypeStruct((m, n), out_dtype),
        compiler_params=_cparams(("parallel", "parallel")),
        name="matmul",
    )(a, w)


def _mult(x, m):
    return x if isinstance(x, int) else pl.multiple_of(x, m)


def _gelu(x):
    c = math.sqrt(2.0 / math.pi)
    return x * (0.5 * (1.0 + jnp.tanh(c * (x + 0.044715 * (x * x * x)))))


def _sigmoid(x):
    return 1.0 / (1.0 + jnp.exp(-x))


TILE_ROWS = V7X_SUBLANES
HALO_ROWS = 2 * V7X_SUBLANES


def _lru_body(x_ref, xp_ref, xn_ref, cw_ref, cb_ref, wg_ref, bg_ref, nc_ref, *rest,
              reverse, n_chunks):
    if reverse:
        ya_ref, hf_ref, o_ref, carry_ref = rest
    else:
        o_ref, carry_ref = rest
    step = pl.program_id(1)
    if reverse:
        chunk = jnp.where(step == 0, 0, n_chunks - step)
    else:
        chunk = step
    rows = ROW_TILE
    c = x_ref.shape[-1]

    @pl.when(step == 0)
    def _():
        carry_ref[...] = jnp.zeros_like(carry_ref)

    x = x_ref[0].astype(F32)
    xx = jnp.concatenate([xp_ref[0].astype(F32), x, xn_ref[0].astype(F32)], axis=0)
    n_xx = rows + 2 * HALO_ROWS
    row = lax.broadcasted_iota(jnp.int32, (rows, c), 0)
    seg_start = jnp.logical_or(chunk == 0, chunk == 1)
    seg_end = jnp.logical_or(chunk == 0, chunk == n_chunks - 1)

    def shifted(d):
        return pltpu.roll(xx, (n_xx - d) % n_xx, axis=0)[HALO_ROWS:HALO_ROWS + rows]

    x_m1 = jnp.where(jnp.logical_and(seg_start, row == 0), 0.0, shifted(-1))
    x_p1 = jnp.where(jnp.logical_and(seg_end, row >= rows - 1), 0.0, shifted(1))
    x_p2 = jnp.where(jnp.logical_and(seg_end, row >= rows - 2), 0.0, shifted(2))
    cw = cw_ref[...]
    u = cw[0:1] * x_m1 + cw[1:2] * x + cw[2:3] * x_p1 + cw[3:4] * x_p2 + cb_ref[...]

    z = jnp.dot(u.astype(BF16), wg_ref[0], preferred_element_type=F32) + bg_ref[0]
    r = _sigmoid(z[:, :c])
    gi = _sigmoid(z[:, c:])
    a = jnp.exp(nc_ref[0] * r)
    bv = jnp.sqrt(1.0 - a * a) * (gi * u)

    sub = row % TILE_ROWS
    for dstep in (1, 2, 4):
        if reverse:
            sh = (rows - dstep) % rows
            keep = sub <= TILE_ROWS - 1 - dstep
        else:
            sh = dstep
            keep = sub >= dstep
        a_sh = jnp.where(keep, pltpu.roll(a, sh, axis=0), 1.0)
        b_sh = jnp.where(keep, pltpu.roll(bv, sh, axis=0), 0.0)
        bv = a * b_sh + bv
        a = a * a_sh

    carry = carry_ref[...]
    n_tiles = rows // TILE_ROWS
    order = range(n_tiles - 1, -1, -1) if reverse else range(n_tiles)
    hs = [None] * n_tiles
    for t in order:
        sl = slice(t * TILE_ROWS, (t + 1) * TILE_ROWS)
        h = bv[sl] + a[sl] * carry
        hs[t] = h
        edge = h[0:1] if reverse else h[TILE_ROWS - 1:TILE_ROWS]
        carry = jnp.broadcast_to(edge, (TILE_ROWS, c))
    carry_ref[...] = carry
    h_all = jnp.concatenate(hs, axis=0)
    if reverse:
        o_ref[0] = (_gelu(ya_ref[0].astype(F32)) * (hf_ref[0] + h_all)).astype(o_ref.dtype)
    else:
        o_ref[0] = h_all


def lru_scan(z, xa_col, ya_col, conv_w, conv_b, wg, bg, negc, hf, *, reverse):
    b, s, _ = z.shape
    c = D_RNN
    n_chunks = s // ROW_TILE
    per = ROW_TILE // HALO_ROWS
    n8 = s // HALO_ROWS
    if reverse:
        cidx = lambda st: jnp.where(st == 0, 0, n_chunks - st)
    else:
        cidx = lambda st: st
    in_specs = [
        pl.BlockSpec((1, ROW_TILE, c), lambda bi, st: (bi, cidx(st), xa_col)),
        pl.BlockSpec((1, HALO_ROWS, c), lambda bi, st: (bi, jnp.maximum(cidx(st) * per - 1, 0), xa_col)),
        pl.BlockSpec((1, HALO_ROWS, c), lambda bi, st: (bi, jnp.minimum((cidx(st) + 1) * per, n8 - 1), xa_col)),
        pl.BlockSpec((CONV_W, c), lambda bi, st: (0, 0)),
        pl.BlockSpec((1, c), lambda bi, st: (0, 0)),
        pl.BlockSpec((1, c, 2 * c), lambda bi, st: (0, 0, 0)),
        pl.BlockSpec((1, 1, 2 * c), lambda bi, st: (0, 0, 0)),
        pl.BlockSpec((1, 1, c), lambda bi, st: (0, 0, 0)),
    ]
    args = [z, z, z, conv_w, conv_b, wg, bg, negc]
    if reverse:
        in_specs += [
            pl.BlockSpec((1, ROW_TILE, c), lambda bi, st: (bi, cidx(st), ya_col)),
            pl.BlockSpec((1, ROW_TILE, c), lambda bi, st: (bi, cidx(st), 0)),
        ]
        args += [z, hf]
    return pl.pallas_call(
        functools.partial(_lru_body, reverse=reverse, n_chunks=n_chunks),
        grid=(b, n_chunks),
        in_specs=in_specs,
        out_specs=pl.BlockSpec((1, ROW_TILE, c), lambda bi, st: (bi, cidx(st), 0)),
        out_shape=jax.ShapeDtypeStruct((b, s, c), BF16 if reverse else F32),
        scratch_shapes=[pltpu.VMEM((TILE_ROWS, c), F32)],
        compiler_params=_cparams(("parallel", "arbitrary")),
        name="lru_rev" if reverse else "lru_fwd",
    )(*args)


def _tile_lanes(t, n):
    return jnp.concatenate([t] * n, axis=1) if n > 1 else t


def _rope(x, c, s1, s2, r):
    w = x.shape[-1]
    n = w // V7X_LANES
    return (x * _tile_lanes(c, n) + pltpu.roll(x, w - r, axis=1) * _tile_lanes(s1, n)
            + pltpu.roll(x, r, axis=1) * _tile_lanes(s2, n))


def _split_dot(x, m_ref):
    hi = x.astype(BF16)
    lo = (x - hi.astype(F32)).astype(BF16)
    m = m_ref[...]
    return jnp.dot(hi, m, preferred_element_type=F32) + jnp.dot(lo, m, preferred_element_type=F32)


def _with_ones(v):
    ones = jnp.ones((v.shape[0], V7X_LANES), v.dtype)
    parts = []
    for c in range(v.shape[1] // V7X_LANES):
        parts += [v[:, c * V7X_LANES:(c + 1) * V7X_LANES], ones]
    return jnp.concatenate(parts, axis=1)


def _rms(x, g):
    return x * lax.rsqrt(jnp.mean(x * x, axis=-1, keepdims=True) + EPS) * g


def _prep_body(qb_ref, kb_ref, vb_ref, qg_ref, tail_ref,
               c64_ref, s164_ref, s264_ref, cm_ref, s1m_ref, s2m_ref,
               avg_ref, gq_ref, gk_ref, gcq_ref, gckv_ref, wuq_ref, wuk_ref, wuv_ref,
               qd_ref, kd_ref, vd_ref, qq_ref, kq_ref, vq_ref, qm_ref, km_ref, vm_ref):
    c64, s164, s264 = c64_ref[...], s164_ref[...], s264_ref[...]
    cm, s1m, s2m = cm_ref[...], s1m_ref[...], s2m_ref[...]
    half64 = DIFF_DK // 2
    qd_ref[0] = (_rope(qb_ref[0].astype(F32), c64, s164, s264, half64) * (DIFF_DK ** -0.5)).astype(BF16)
    kd_ref[0] = _rope(kb_ref[0].astype(F32), c64, s164, s264, half64).astype(BF16)
    vd_ref[0] = _with_ones(vb_ref[0].astype(BF16))
    qg = qg_ref[0].astype(F32)
    ms = _split_dot(qg * qg, avg_ref)
    qn = qg * lax.rsqrt(ms + EPS) * gq_ref[...]
    qq_ref[0] = (_rope(qn, c64, s164, s264, GQA_DH // 2) * (GQA_DH ** -0.5)).astype(BF16)
    tail = tail_ref[0].astype(F32)
    kg = tail[:, 0:128]
    msk = _split_dot(kg * kg, avg_ref.at[0:128, 0:128])
    kn = _rope(kg * lax.rsqrt(msk + EPS) * gk_ref[...], c64, s164, s264, GQA_DH // 2).astype(BF16)
    vg = tail[:, 128:256].astype(BF16)
    kq_ref[0] = jnp.concatenate([kn[:, 0:64], kn[:, 0:64], kn[:, 64:128], kn[:, 64:128]], axis=1)
    vq_ref[0] = _with_ones(jnp.concatenate([vg[:, 0:64], vg[:, 0:64], vg[:, 64:128], vg[:, 64:128]], axis=1))
    cq = tail[:, 256:256 + MLA_Q_RANK]
    ckv = tail[:, 640:640 + MLA_KV_RANK]
    kr = tail[:, 896:1024]
    qf = jnp.dot(_rms(cq, gcq_ref[...]).astype(BF16), wuq_ref[...], preferred_element_type=F32)
    scale = (MLA_NOPE + MLA_ROPE) ** -0.5
    qm_ref[0] = (_rope(qf, cm, s1m, s2m, MLA_ROPE // 2) * scale).astype(BF16)
    ckvn = _rms(ckv, gckv_ref[...]).astype(BF16)
    kf = jnp.dot(ckvn, wuk_ref[...], preferred_element_type=F32)
    krr = _rope(kr, cm, s1m, s2m, MLA_ROPE // 2)
    km_ref[0] = (kf + _tile_lanes(krr, MLA_HEADS)).astype(BF16)
    vm_ref[0] = _with_ones(jnp.dot(ckvn, wuv_ref[...], preferred_element_type=F32).astype(BF16))


def attn_prep(z, tabs, avg, gq, gk, gcq, gckv, wuq, wuk, wuv):
    b, s, _ = z.shape
    nt = s // ROW_TILE
    zspec = lambda w, idx: pl.BlockSpec((1, ROW_TILE, w), lambda ti, bi: (bi, ti, idx))
    tab = pl.BlockSpec((ROW_TILE, V7X_LANES), lambda ti, bi: (ti, 0))
    full = lambda a: pl.BlockSpec(a.shape, lambda ti, bi: (0,) * a.ndim)
    ospec = lambda w: pl.BlockSpec((1, ROW_TILE, w), lambda ti, bi: (bi, ti, 0))
    oshape = lambda w: jax.ShapeDtypeStruct((b, s, w), BF16)
    widths = (512, 512, 1024, 512, 256, 512, 1024, 1024, 1024)
    return pl.pallas_call(
        _prep_body,
        grid=(nt, b),
        in_specs=[zspec(512, 10), zspec(512, 11), zspec(512, 12), zspec(512, 13), zspec(1024, 7)]
        + [tab] * 6 + [full(a) for a in (avg, gq, gk, gcq, gckv, wuq, wuk, wuv)],
        out_specs=[ospec(w) for w in widths],
        out_shape=[oshape(w) for w in widths],
        compiler_params=_cparams(("parallel", "parallel")),
        name="attn_prep",
    )(z, z, z, z, z, *tabs, avg, gq, gk, gcq, gckv, wuq, wuk, wuv)


ATTN_GROUPS = 2


def _attn_body(lam_ref, q_ref, k_ref, v_ref, g_ref, o_ref, *, mode, out_scale):
    ti = pl.program_id(2)
    lane = lax.broadcasted_iota(jnp.int32, (ROW_TILE, V7X_LANES), 1)
    lo_half = lane < (V7X_LANES // 2)

    ln = V7X_LANES

    def run(n_keys):
        for grp in range(ATTN_GROUPS):
            outs = []
            for j in range(2):
                if mode == "mla":
                    c0 = (2 * grp + j) * ln
                    qj = q_ref[0, :, c0:c0 + ln]
                    kj = k_ref[0, 0:n_keys, c0:c0 + ln]
                else:
                    q = q_ref[0, :, grp * ln:(grp + 1) * ln]
                    keep = lo_half if j == 0 else jnp.logical_not(lo_half)
                    qj = jnp.where(keep, q.astype(F32), 0.0).astype(BF16)
                    kc = 0 if mode == "pair" else grp * ln
                    kj = k_ref[0, 0:n_keys, kc:kc + ln]
                vc = 0 if mode == "pair" else grp * 2 * ln
                hk = (n_keys // (2 * ln)) * ln
                tn = (((1,), (1,)), ((), ()))
                s = jnp.concatenate(
                    [lax.dot_general(qj, kj[0:hk], tn, preferred_element_type=F32).astype(BF16),
                     lax.dot_general(qj, kj[hk:], tn, preferred_element_type=F32).astype(BF16)], axis=1)
                m = jnp.max(s, axis=-1, keepdims=True)
                p = jnp.exp(s - m)
                o = (jnp.dot(p[:, 0:hk], v_ref[0, 0:hk, vc:vc + 2 * ln], preferred_element_type=F32)
                     + jnp.dot(p[:, hk:], v_ref[0, hk:n_keys, vc:vc + 2 * ln], preferred_element_type=F32))
                outs.append(o[:, 0:ln] / o[:, ln:2 * ln])
            if mode == "diff":
                o = outs[0] - lam_ref[0] * outs[1]
                o = o * lax.rsqrt(jnp.mean(o * o, axis=-1, keepdims=True) + EPS) * g_ref[...] * out_scale
            else:
                o = jnp.where(lo_half, outs[0], outs[1])
            o_ref[0, :, grp * ln:(grp + 1) * ln] = o.astype(o_ref.dtype)

    @pl.when(ti == 0)
    def _():
        run(ROW_TILE)

    @pl.when(ti > 0)
    def _():
        run(k_ref.shape[1])


def attention(q, k, v, lam, g, *, mode, out_scale=1.0):
    b, s, wq = q.shape
    steps = 4 // ATTN_GROUPS
    qw = wq // steps
    kw = k.shape[-1] // steps
    vw = v.shape[-1] // steps
    ow = ATTN_GROUPS * V7X_LANES
    nt = s // ROW_TILE
    return pl.pallas_call(
        functools.partial(_attn_body, mode=mode, out_scale=out_scale),
        grid=(b, steps, nt),
        in_specs=[
            pl.BlockSpec(memory_space=pltpu.SMEM),
            pl.BlockSpec((1, ROW_TILE, qw), lambda bi, gi, ti: (bi, ti, gi)),
            pl.BlockSpec((1, s, kw), lambda bi, gi, ti: (bi, 0, gi)),
            pl.BlockSpec((1, s, vw), lambda bi, gi, ti: (bi, 0, gi)),
            pl.BlockSpec((1, V7X_LANES), lambda bi, gi, ti: (0, 0)),
        ],
        out_specs=pl.BlockSpec((1, ROW_TILE, ow), lambda bi, gi, ti: (bi, ti, gi)),
        out_shape=jax.ShapeDtypeStruct((b, s, 4 * V7X_LANES), BF16),
        compiler_params=_cparams(("parallel", "parallel", "arbitrary")),
        name="attn_" + mode,
    )(lam, q, k, v, g)


def _merge_body(oa_ref, ob_ref, og_ref, om_ref, zg_ref, wb_ref, wo_ref, x_ref, gate_ref, o_ref):
    d = x_ref.shape[-1]
    merged = None
    for k, o_k in enumerate((oa_ref, ob_ref, og_ref, om_ref)):
        t = jnp.dot(o_k[0], wb_ref[k], preferred_element_type=F32)
        t = t * _sigmoid(zg_ref[0, :, k * d:(k + 1) * d].astype(F32))
        merged = t if merged is None else merged + t
    y = jnp.dot(merged.astype(BF16), wo_ref[...], preferred_element_type=F32)
    o_ref[0] = x_ref[0] + gate_ref[0, 0] * y


def merge(oa, ob, og, om, z, wb, wo, x, gate):
    b, s, d = x.shape
    nt = s // ROW_TILE
    bspec = pl.BlockSpec((1, ROW_TILE, BRANCH_W), lambda bi, ti: (bi, ti, 0))
    return pl.pallas_call(
        _merge_body,
        grid=(b, nt),
        in_specs=[bspec, bspec, bspec, bspec,
                  pl.BlockSpec((1, ROW_TILE, N_BRANCH * d), lambda bi, ti: (bi, ti, 0)),
                  pl.BlockSpec(wb.shape, lambda bi, ti: (0, 0, 0)),
                  pl.BlockSpec(wo.shape, lambda bi, ti: (0, 0)),
                  pl.BlockSpec((1, ROW_TILE, d), lambda bi, ti: (bi, ti, 0)),
                  pl.BlockSpec((1, 1, 1, d), lambda bi, ti: (bi, jnp.minimum(ti, 1), 0, 0))],
        out_specs=pl.BlockSpec((1, ROW_TILE, d), lambda bi, ti: (bi, ti, 0)),
        out_shape=jax.ShapeDtypeStruct((b, s, d), F32),
        compiler_params=_cparams(("parallel", "parallel")),
        name="merge",
    )(oa, ob, og, om, z, wb, wo, x, gate)


NEG_BIG = float(jnp.finfo(jnp.float32).min)


def _route_body(x_ref, g_ref, sh_ref, sc_ref, wq_ref, kb_ref,
                h_ref, a_ref, b_ref, e1_ref, e2_ref, st_ref, ta_ref, tb_ref):
    x = x_ref[0]
    ms = jnp.mean(x * x, axis=-1, keepdims=True)
    hf = x * lax.rsqrt(ms + EPS) * g_ref[...] * (1.0 + sc_ref[0, 0]) + sh_ref[0, 0]
    h_ref[0] = hf.T.astype(BF16)
    h = hf.astype(BF16)
    q = jnp.dot(h, wq_ref[...], preferred_element_type=F32).astype(BF16)
    nk = PEER_NKEYS
    for hd in range(PEER_HEADS):
        st_ref[hd * 2 * nk:(hd + 1) * 2 * nk, :] = lax.dot_general(
            kb_ref[hd], q[:, hd * PEER_DK:(hd + 1) * PEER_DK], (((1,), (1,)), ((), ())),
            preferred_element_type=F32)

    def top_rows(v, dst_ref):
        for r in range(PEER_TOPK):
            m = jnp.max(v, axis=0, keepdims=True)
            dst_ref[r:r + 1, :] = m
            v = jnp.where(v >= m, NEG_BIG, v)

    row8 = lax.broadcasted_iota(jnp.int32, (TILE_ROWS, ROW_TILE), 0)

    def head(hd, carry):
        base = pl.multiple_of(hd * (2 * nk), 2 * nk)
        s1 = st_ref[pl.ds(base, nk), :]
        s2 = st_ref[pl.ds(base + nk, nk), :]
        top_rows(s1, ta_ref)
        top_rows(s2, tb_ref)
        pieces = [ta_ref[0:1, :] + tb_ref[...]]
        tb8 = tb_ref[0:TILE_ROWS, :]
        for p in range(1, PEER_TOPK):
            n_q = PEER_TOPK // (p + 1)
            c = ta_ref[p:p + 1, :] + tb8
            pieces.append(c if n_q >= TILE_ROWS else jnp.where(row8 < n_q, c, NEG_BIG))
        cand = jnp.concatenate(pieces, axis=0)
        top = ta_ref[0:1, :] + tb_ref[0:1, :]
        zsum = jnp.zeros_like(top)
        m = top
        for r in range(PEER_TOPK):
            m = jnp.max(cand, axis=0, keepdims=True)
            zsum = zsum + jnp.exp(m - top)
            cand = jnp.where(cand >= m, NEG_BIG, cand)
        a_ref[hd] = m - s1
        e1_ref[hd] = jnp.exp(s1 - ta_ref[0:1, :]) / zsum
        e2 = jnp.exp(s2 - tb_ref[0:1, :])
        for lt in range(ROW_TILE // V7X_LANES):
            b_ref[hd, lt] = s2[:, lt * V7X_LANES:(lt + 1) * V7X_LANES]
            e2_ref[hd, lt] = e2[:, lt * V7X_LANES:(lt + 1) * V7X_LANES]
        return carry

    lax.fori_loop(0, PEER_HEADS, head, 0)


def peer_route(x, g, shift, scale, wq, kbig):
    b, s, d = x.shape
    nt = s // ROW_TILE
    m = b * s
    seg = lambda bi, ti: (bi, jnp.minimum(ti, 1), 0, 0)
    flat = lambda bi, ti: (0, 0, bi * nt + ti)
    kspec = pl.BlockSpec((PEER_HEADS, PEER_NKEYS, ROW_TILE), flat)
    kshape = jax.ShapeDtypeStruct((PEER_HEADS, PEER_NKEYS, m), F32)
    lt_per = ROW_TILE // V7X_LANES
    tspec = pl.BlockSpec((PEER_HEADS, lt_per, PEER_NKEYS, V7X_LANES), lambda bi, ti: (0, bi * nt + ti, 0, 0))
    tshape = jax.ShapeDtypeStruct((PEER_HEADS, m // V7X_LANES, PEER_NKEYS, V7X_LANES), F32)
    return pl.pallas_call(
        _route_body,
        grid=(b, nt),
        in_specs=[
            pl.BlockSpec((1, ROW_TILE, d), lambda bi, ti: (bi, ti, 0)),
            pl.BlockSpec((1, d), lambda bi, ti: (0, 0)),
            pl.BlockSpec((1, 1, 1, d), seg),
            pl.BlockSpec((1, 1, 1, d), seg),
            pl.BlockSpec(wq.shape, lambda bi, ti: (0, 0)),
            pl.BlockSpec(kbig.shape, lambda bi, ti: (0, 0, 0)),
        ],
        out_specs=[pl.BlockSpec((1, d, ROW_TILE), lambda bi, ti: (bi * nt + ti, 0, 0)),
                   kspec, tspec, kspec, tspec],
        out_shape=[jax.ShapeDtypeStruct((m // ROW_TILE, d, ROW_TILE), BF16), kshape, tshape, kshape, tshape],
        scratch_shapes=[pltpu.VMEM((2 * PEER_HEADS * PEER_NKEYS, ROW_TILE), F32),
                        pltpu.VMEM((PEER_TOPK, ROW_TILE), F32),
                        pltpu.VMEM((PEER_TOPK, ROW_TILE), F32)],
        compiler_params=_cparams(("parallel", "parallel")),
        name="peer_route",
    )(x, g, shift, scale, wq, kbig)


PEER_TOK = 2 * ROW_TILE
PEER_ECH = 1024


def _expert_body(h_ref, u0_ref, u_ref, vt_ref, a_ref, e1_ref, b_ref, e2_ref, x_ref, g0_ref, g1_ref,
                 fg_ref, o_ref, acc_ref, w0_ref, w1_ref, uf0_ref, uf1_ref, thb_ref, e1b_ref, *,
                 final, n_ch):
    e = pl.program_id(1)
    nk = PEER_NKEYS
    n_il = PEER_ECH // nk
    sub = TILE_ROWS
    ib = 2
    n_ib = n_il // ib
    tn = (((1,), (1,)), ((), ()))
    mq = V7X_MXU_DIM
    uf_refs = (uf0_ref, uf1_ref)
    w_refs = (w0_ref, w1_ref)

    def gate_block(g, cur):
        uf_ref, w_ref = uf_refs[cur], w_refs[cur]
        lt = g // n_ib
        i0 = (g % n_ib) * ib
        gates = [None] * ib
        for hd in range(PEER_HEADS):
            bj = b_ref[hd, lt].reshape(nk // sub, sub, V7X_LANES)
            e2j = e2_ref[hd, lt].reshape(nk // sub, sub, V7X_LANES)
            for k in range(ib):
                th = thb_ref[lt, hd * n_il + i0 + k][None]
                e1 = e1b_ref[lt, hd * n_il + i0 + k][None]
                t = jnp.where(bj >= th, e1 * e2j, 0.0)
                gates[k] = t if gates[k] is None else gates[k] + t
        lpm = mq // V7X_LANES
        wl = pl.ds(_mult((lt % lpm) * V7X_LANES, V7X_LANES), V7X_LANES)
        for k in range(ib):
            rows = pl.ds(_mult((i0 + k) * nk, nk), nk)
            act = _gelu(uf_ref[lt, rows, :]) * gates[k].reshape(nk, V7X_LANES)
            w_ref[lt // lpm, rows, wl] = act.astype(BF16)

    def k_major(ref, r0):
        return jnp.concatenate([ref[k, pl.ds(r0, mq), :] for k in range(ref.shape[0])], axis=1)

    def next_u(it, oth):
        r0 = _mult((it // 2) * mq, mq)
        res = jnp.dot(k_major(u_ref, r0), h_ref[it % 2], preferred_element_type=F32)
        for k in range(mq // V7X_LANES):
            uf_refs[oth][(it % 2) * (mq // V7X_LANES) + k, pl.ds(r0, mq), :] = (
                res[:, k * V7X_LANES:(k + 1) * V7X_LANES])

    def prev_out(it, oth):
        r0 = _mult((it // 2) * mq, mq)
        acc_ref[it % 2, pl.ds(r0, mq), :] += jnp.dot(
            k_major(vt_ref, r0), w_refs[oth][it % 2], preferred_element_type=F32)

    n_it = (PEER_ECH // mq) * (PEER_TOK // mq)
    per_it = (PEER_TOK // V7X_LANES) * n_ib // n_it

    def run_chunk(with_prev, cur):
        for lt in range(PEER_TOK // V7X_LANES):
            tl = slice(lt * V7X_LANES, (lt + 1) * V7X_LANES)
            for hd in range(PEER_HEADS):
                for il in range(n_il):
                    thb_ref[lt, hd * n_il + il] = jnp.broadcast_to(a_ref[hd, il:il + 1, tl], (sub, V7X_LANES))
                    e1b_ref[lt, hd * n_il + il] = jnp.broadcast_to(e1_ref[hd, il:il + 1, tl], (sub, V7X_LANES))

        def step(it, carry):
            for k in range(per_it):
                gate_block(it * per_it + k, cur)
            next_u(it, 1 - cur)
            if with_prev:
                prev_out(it, 1 - cur)
            return carry

        lax.fori_loop(0, n_it, step, 0)

    @pl.when(e == 0)
    def _():
        acc_ref[...] = jnp.zeros_like(acc_ref)
        u_all = jnp.concatenate([u0_ref[k] for k in range(u0_ref.shape[0])], axis=1)
        lpm = mq // V7X_LANES
        for c in range(PEER_TOK // mq):
            u_first = jnp.dot(u_all, h_ref[c], preferred_element_type=F32)
            for k in range(lpm):
                uf0_ref[c * lpm + k] = u_first[:, k * V7X_LANES:(k + 1) * V7X_LANES]
        run_chunk(False, 0)

    for par in range(2):
        @pl.when(jnp.logical_and(jnp.logical_and(e > 0, e < n_ch), e % 2 == par))
        def _():
            run_chunk(True, par)

    @pl.when(e == n_ch)
    def _():
        vt_all = jnp.concatenate([vt_ref[k] for k in range(vt_ref.shape[0])], axis=1)
        for half, gref in enumerate((g0_ref, g1_ref)):
            y = acc_ref[half] + jnp.dot(vt_all, w_refs[(n_ch - 1) % 2][half], preferred_element_type=F32)
            sl = slice(half * ROW_TILE, (half + 1) * ROW_TILE)
            xn = x_ref[sl, :] + gref[0, 0] * y.T
            if final:
                xn = xn * lax.rsqrt(jnp.mean(xn * xn, axis=-1, keepdims=True) + EPS) * fg_ref[...]
            o_ref[sl, :] = xn


def peer_experts(h, u_tab, vt_tab, a, e1, bm, e2, x, gate, fg, *, final):
    b, s, d = x.shape
    nt = s // ROW_TILE
    m = b * s
    n_exp = u_tab.shape[1]
    i_per = PEER_ECH // PEER_NKEYS

    def gidx(half):
        def f(i, e):
            t = 2 * i + half
            return (t // nt, jnp.minimum(t % nt, 1), 0, 0)
        return f

    n_ch = n_exp // PEER_ECH
    last = n_ch - 1
    mq = V7X_MXU_DIM
    n_lt = PEER_TOK // V7X_LANES
    assert mq == ROW_TILE and PEER_TOK == 2 * mq
    out = pl.pallas_call(
        functools.partial(_expert_body, final=final, n_ch=n_ch),
        grid=(m // PEER_TOK, n_ch + 1),
        in_specs=[
            pl.BlockSpec((PEER_TOK // mq, d, mq), lambda i, e: (i, 0, 0)),
            pl.BlockSpec((d // mq, PEER_ECH, mq), lambda i, e: (0, 0, 0)),
            pl.BlockSpec((d // mq, PEER_ECH, mq), lambda i, e: (0, jnp.minimum(e + 1, last), 0)),
            pl.BlockSpec((PEER_ECH // mq, d, mq), lambda i, e: (jnp.maximum(e - 1, 0), 0, 0)),
            pl.BlockSpec((PEER_HEADS, i_per, PEER_TOK), lambda i, e: (0, jnp.minimum(e, last), i)),
            pl.BlockSpec((PEER_HEADS, i_per, PEER_TOK), lambda i, e: (0, jnp.minimum(e, last), i)),
            pl.BlockSpec((PEER_HEADS, PEER_TOK // V7X_LANES, PEER_NKEYS, V7X_LANES), lambda i, e: (0, i, 0, 0)),
            pl.BlockSpec((PEER_HEADS, PEER_TOK // V7X_LANES, PEER_NKEYS, V7X_LANES), lambda i, e: (0, i, 0, 0)),
            pl.BlockSpec((PEER_TOK, d), lambda i, e: (i, 0)),
            pl.BlockSpec((1, 1, 1, d), gidx(0)),
            pl.BlockSpec((1, 1, 1, d), gidx(1)),
            pl.BlockSpec((1, d), lambda i, e: (0, 0)),
        ],
        out_specs=pl.BlockSpec((PEER_TOK, d), lambda i, e: (i, 0)),
        out_shape=jax.ShapeDtypeStruct((m, d), F32),
        scratch_shapes=[pltpu.VMEM((PEER_TOK // mq, d, mq), F32),
                        pltpu.VMEM((PEER_TOK // mq, PEER_ECH, mq), BF16),
                        pltpu.VMEM((PEER_TOK // mq, PEER_ECH, mq), BF16),
                        pltpu.VMEM((n_lt, PEER_ECH, V7X_LANES), F32),
                        pltpu.VMEM((n_lt, PEER_ECH, V7X_LANES), F32),
                        pltpu.VMEM((n_lt, PEER_HEADS * i_per, TILE_ROWS, V7X_LANES), F32),
                        pltpu.VMEM((n_lt, PEER_HEADS * i_per, TILE_ROWS, V7X_LANES), F32)],
        compiler_params=_cparams(("parallel", "arbitrary")),
        name="peer_experts",
    )(h, u_tab, u_tab, vt_tab, a, e1, bm, e2, x.reshape(m, d), gate, gate, fg)
    return out.reshape(b, s, d)


def _axial_tables(rows, rope_dim, ctx_len, lane0):
    n_freq = rope_dim // 4
    half = rope_dim // 2
    inv_freq = ROPE_THETA ** (-jnp.arange(n_freq, dtype=F32) / n_freq)
    r = jnp.repeat(jnp.arange(rows, dtype=F32), GRID_W)
    col = jnp.tile(jnp.arange(GRID_W, dtype=F32), rows)
    ang = jnp.concatenate([r[:, None] * inv_freq, col[:, None] * inv_freq], axis=-1)
    cos, sin = jnp.cos(ang), jnp.sin(ang)
    n = cos.shape[0]
    period = 64 if rope_dim == 64 else V7X_LANES
    c_blk = jnp.ones((n, period), F32).at[:, lane0:lane0 + rope_dim].set(jnp.concatenate([cos, cos], -1))
    s1_blk = jnp.zeros((n, period), F32).at[:, lane0:lane0 + half].set(-sin)
    s2_blk = jnp.zeros((n, period), F32).at[:, lane0 + half:lane0 + rope_dim].set(sin)
    reps = V7X_LANES // period
    out = []
    for blk, fill in ((c_blk, 1.0), (s1_blk, 0.0), (s2_blk, 0.0)):
        t = jnp.tile(blk, (1, reps))
        out.append(jnp.concatenate([jnp.full((ctx_len, V7X_LANES), fill, F32), t], axis=0))
    return out


def _block_diag(w):
    n, a, b = w.shape
    eye = jnp.eye(n, dtype=w.dtype)
    return (eye[:, None, :, None] * w[:, :, None, :]).reshape(n * a, n * b)


def kernel(x, c, ctx, c_ctx, w_mod, b_mod, norm1_g, norm2_g, w_in, conv_w, conv_b, lru_wa, lru_ba,
           lru_wi, lru_bi, lru_lambda, diff_lam, diff_subln_g, gqa_qnorm_g, gqa_knorm_g, mla_qnorm_g,
           mla_w_uq, mla_kvnorm_g, mla_w_ukv, w_branch, w_out, peer_wq, peer_keys, peer_u, peer_v,
           final_norm_g):
    bsz, seq, d = x.shape
    ctx_len = ctx.shape[1]
    depth = w_in.shape[0]
    assert ctx_len == ROW_TILE and seq % PEER_TOK == 0 and seq % GRID_W == 0
    rows = seq // GRID_W
    s_all = ctx_len + seq
    xs = jnp.concatenate([ctx, x], axis=1)

    tabs = _axial_tables(rows, DIFF_DK, ctx_len, 0) + _axial_tables(rows, MLA_ROPE, ctx_len, MLA_NOPE)
    grp = jnp.arange(BRANCH_W) // GQA_DH
    avg = (grp[:, None] == grp[None, :]).astype(BF16) * (1.0 / GQA_DH)
    sc_in = jnp.zeros((16, d), F32).at[:bsz].set(jax.nn.silu(c)).at[bsz].set(jax.nn.silu(c_ctx)).astype(BF16)
    zero1 = jnp.zeros((1,), F32)
    ones_g = jnp.ones((1, V7X_LANES), F32)

    for l in range(depth):
        lam_init = 0.8 - 0.6 * math.exp(-0.3 * l)
        mod_all = matmul(sc_in, w_mod[l].astype(BF16)) + b_mod[l]
        mod_b = mod_all[:bsz].reshape(bsz, N_MOD, d)
        mod_c = jnp.broadcast_to(mod_all[bsz].reshape(1, N_MOD, d), (bsz, N_MOD, d))
        mods = [jnp.stack([mod_c[:, k], mod_b[:, k]], axis=1)[:, :, None, :] for k in range(N_MOD)]

        wl = w_in[l]
        o = 0
        parts = []
        for w in (512, 512, 512, 512, 512, 512, 128, 128, 384, 256, 32, 4096):
            parts.append(wl[:, o:o + w])
            o += w
        xa, ya, qb, kb, vb, qg, kg, vg, cq, ckv, kr, zg = parts
        kr_blk = jnp.zeros((d, V7X_LANES), F32).at[:, MLA_NOPE:MLA_NOPE + MLA_ROPE].set(kr)
        w_cat = jnp.concatenate([zg, xa, ya, qb, kb, vb, qg, kg, vg, cq, ckv, kr_blk], axis=1).astype(BF16)

        h = adaln(xs, norm1_g[l][None], mods[0], mods[1], 0)
        z = matmul(h.reshape(bsz * s_all, d), w_cat, out_dtype=BF16, tm=1024, tn=1024)
        z = z.reshape(bsz, s_all, w_cat.shape[1])

        negc = (-LRU_C * jax.nn.softplus(-lru_lambda[l]))[:, None, None, :]
        wg = jnp.stack([jnp.concatenate([_block_diag(lru_wa[l, dd]), _block_diag(lru_wi[l, dd])], axis=1)
                        for dd in range(2)]).astype(BF16)[:, None]
        bg = jnp.stack([jnp.concatenate([lru_ba[l, dd], lru_bi[l, dd]]) for dd in range(2)])[:, None, None, :]
        cb = conv_b[l][None]
        hf = lru_scan(z, 8, 9, conv_w[l], cb, wg[0], bg[0], negc[0], None, reverse=False)
        oa = lru_scan(z, 8, 9, conv_w[l], cb, wg[1], bg[1], negc[1], hf, reverse=True)

        uq = mla_w_uq[l].reshape(MLA_Q_RANK, MLA_HEADS, MLA_NOPE + MLA_ROPE)
        wuq = jnp.pad(uq, ((0, 0), (0, 0), (0, V7X_LANES - MLA_NOPE - MLA_ROPE))).reshape(MLA_Q_RANK, -1)
        ukv = mla_w_ukv[l].reshape(MLA_KV_RANK, MLA_HEADS, MLA_NOPE + MLA_DV)
        wuk = jnp.pad(ukv[:, :, :MLA_NOPE], ((0, 0), (0, 0), (0, V7X_LANES - MLA_NOPE))).reshape(MLA_KV_RANK, -1)
        wuv = ukv[:, :, MLA_NOPE:].reshape(MLA_KV_RANK, -1)
        qd, kd, vd, qq, kq, vq, qm, km, vm = attn_prep(
            z, tabs, avg, jnp.tile(gqa_qnorm_g[l], GQA_HEADS)[None], jnp.tile(gqa_knorm_g[l], GQA_KV_HEADS)[None],
            mla_qnorm_g[l][None], mla_kvnorm_g[l][None], wuq.astype(BF16), wuk.astype(BF16), wuv.astype(BF16))
        lv = diff_lam[l]
        lam = (jnp.exp(jnp.sum(lv[0] * lv[1])) - jnp.exp(jnp.sum(lv[2] * lv[3])) + lam_init).reshape(1)
        ob = attention(qd, kd, vd, lam, diff_subln_g[l][None], mode="diff", out_scale=1.0 - lam_init)
        og = attention(qq, kq, vq, zero1, ones_g, mode="pair")
        om = attention(qm, km, vm, zero1, ones_g, mode="mla")

        xs = merge(oa, ob, og, om, z, w_branch[l].astype(BF16), w_out[l].astype(BF16), xs, mods[2])

        kbig = jnp.stack([_block_diag(peer_keys[l, hd]) for hd in range(PEER_HEADS)]).astype(BF16)
        h2, pa, pb, pe1, pe2 = peer_route(xs, norm2_g[l][None], mods[3], mods[4], peer_wq[l].astype(BF16), kbig)
        n_exp = peer_u.shape[1]
        u_km = peer_u[l].astype(BF16).reshape(n_exp, d // V7X_MXU_DIM, V7X_MXU_DIM).transpose(1, 0, 2)
        vt_km = peer_v[l].astype(BF16).reshape(n_exp // V7X_MXU_DIM, V7X_MXU_DIM, d).transpose(0, 2, 1)
        xs = peer_experts(h2, u_km, vt_km, pa, pe1, pb, pe2, xs, mods[5],
                          final_norm_g[None], final=(l == depth - 1))
    return xs[:, ctx_len:]
```

```python
import functools
import math

import jax
import jax.numpy as jnp
from jax import lax
from jax.experimental import pallas as pl
from jax.experimental.pallas import tpu as pltpu

F32 = jnp.float32
BF16 = jnp.bfloat16

EPS = 1e-6
GRID_W = 64
ROPE_THETA = 10000.0
N_MOD = 6

D_RNN = 512
RNN_BLOCKS = 8
CONV_W = 4
LRU_C = 8.0

DIFF_HEADS = 4
DIFF_DK = 64
DIFF_DV = 128
GQA_HEADS = 8
GQA_KV_HEADS = 2
GQA_DH = 64
MLA_HEADS = 8
MLA_NOPE = 64
MLA_ROPE = 32
MLA_DV = 64
MLA_Q_RANK = 384
MLA_KV_RANK = 256
N_BRANCH = 4
BRANCH_W = 512

PEER_HEADS = 8
PEER_NKEYS = 128
PEER_DK = 128
PEER_TOPK = 16

V7X_LANES = 128
V7X_SUBLANES = 8
V7X_VMEM_BYTES = 64 * 1024 * 1024
V7X_MXU_DIM = 256
VMEM_LIMIT = 48 * 1024 * 1024

ROW_TILE = 256


def _cparams(sem):
    return pltpu.CompilerParams(dimension_semantics=sem, vmem_limit_bytes=VMEM_LIMIT)


def _adaln_body(x_ref, g_ref, sh_ref, sc_ref, o_ref):
    x = x_ref[0]
    ms = jnp.mean(x * x, axis=-1, keepdims=True)
    y = x * lax.rsqrt(ms + EPS) * g_ref[...]
    o_ref[0] = (y * (1.0 + sc_ref[0, 0]) + sh_ref[0, 0]).astype(o_ref.dtype)


def adaln(x, g, shift, scale, t0):
    b, s, d = x.shape
    nt = s // ROW_TILE - t0
    seg = lambda bi, ti: (bi, jnp.minimum(ti + t0, 1), 0, 0)
    return pl.pallas_call(
        _adaln_body,
        grid=(b, nt),
        in_specs=[
            pl.BlockSpec((1, ROW_TILE, d), lambda bi, ti: (bi, ti + t0, 0)),
            pl.BlockSpec((1, d), lambda bi, ti: (0, 0)),
            pl.BlockSpec((1, 1, 1, d), seg),
            pl.BlockSpec((1, 1, 1, d), seg),
        ],
        out_specs=pl.BlockSpec((1, ROW_TILE, d), lambda bi, ti: (bi, ti, 0)),
        out_shape=jax.ShapeDtypeStruct((b, nt * ROW_TILE, d), BF16),
        compiler_params=_cparams(("parallel", "parallel")),
        name="adaln",
    )(x, g, shift, scale)


def _mm_body(a_ref, w_ref, o_ref):
    o_ref[...] = jnp.dot(a_ref[...], w_ref[...], preferred_element_type=F32).astype(o_ref.dtype)


def matmul(a, w, out_dtype=F32, tm=512, tn=512):
    m, k = a.shape
    _, n = w.shape
    tm = math.gcd(tm, m)
    tn = math.gcd(tn, n)
    return pl.pallas_call(
        _mm_body,
        grid=(n // tn, m // tm),
        in_specs=[
            pl.BlockSpec((tm, k), lambda j, i: (i, 0)),
            pl.BlockSpec((k, tn), lambda j, i: (0, j)),
        ],
        out_specs=pl.BlockSpec((tm, tn), lambda j, i: (i, j)),
        out_shape=jax.ShapeDtypeStruct((m, n), out_dtype),
        compiler_params=_cparams(("parallel", "parallel")),
        name="matmul",
    )(a, w)


def _mult(x, m):
    return x if isinstance(x, int) else pl.multiple_of(x, m)


def _gelu(x):
    c = math.sqrt(2.0 / math.pi)
    return x * (0.5 * (1.0 + jnp.tanh(c * (x + 0.044715 * (x * x * x)))))


def _sigmoid(x):
    return 1.0 / (1.0 + jnp.exp(-x))


TILE_ROWS = V7X_SUBLANES
HALO_ROWS = 2 * V7X_SUBLANES


def _lru_body(x_ref, xp_ref, xn_ref, cw_ref, cb_ref, wg_ref, bg_ref, nc_ref, *rest,
              reverse, n_chunks):
    if reverse:
        ya_ref, hf_ref, o_ref, carry_ref = rest
    else:
        o_ref, carry_ref = rest
    step = pl.program_id(1)
    if reverse:
        chunk = jnp.where(step == 0, 0, n_chunks - step)
    else:
        chunk = step
    rows = ROW_TILE
    c = x_ref.shape[-1]

    @pl.when(step == 0)
    def _():
        carry_ref[...] = jnp.zeros_like(carry_ref)

    x = x_ref[0].astype(F32)
    xx = jnp.concatenate([xp_ref[0].astype(F32), x, xn_ref[0].astype(F32)], axis=0)
    n_xx = rows + 2 * HALO_ROWS
    row = lax.broadcasted_iota(jnp.int32, (rows, c), 0)
    seg_start = jnp.logical_or(chunk == 0, chunk == 1)
    seg_end = jnp.logical_or(chunk == 0, chunk == n_chunks - 1)

    def shifted(d):
        return pltpu.roll(xx, (n_xx - d) % n_xx, axis=0)[HALO_ROWS:HALO_ROWS + rows]

    x_m1 = jnp.where(jnp.logical_and(seg_start, row == 0), 0.0, shifted(-1))
    x_p1 = jnp.where(jnp.logical_and(seg_end, row >= rows - 1), 0.0, shifted(1))
    x_p2 = jnp.where(jnp.logical_and(seg_end, row >= rows - 2), 0.0, shifted(2))
    cw = cw_ref[...]
    u = cw[0:1] * x_m1 + cw[1:2] * x + cw[2:3] * x_p1 + cw[3:4] * x_p2 + cb_ref[...]

    z = jnp.dot(u.astype(BF16), wg_ref[0], preferred_element_type=F32) + bg_ref[0]
    r = _sigmoid(z[:, :c])
    gi = _sigmoid(z[:, c:])
    a = jnp.exp(nc_ref[0] * r)
    bv = jnp.sqrt(1.0 - a * a) * (gi * u)

    sub = row % TILE_ROWS
    for dstep in (1, 2, 4):
        if reverse:
            sh = (rows - dstep) % rows
            keep = sub <= TILE_ROWS - 1 - dstep
        else:
            sh = dstep
            keep = sub >= dstep
        a_sh = jnp.where(keep, pltpu.roll(a, sh, axis=0), 1.0)
        b_sh = jnp.where(keep, pltpu.roll(bv, sh, axis=0), 0.0)
        bv = a * b_sh + bv
        a = a * a_sh

    carry = carry_ref[...]
    n_tiles = rows // TILE_ROWS
    order = range(n_tiles - 1, -1, -1) if reverse else range(n_tiles)
    hs = [None] * n_tiles
    for t in order:
        sl = slice(t * TILE_ROWS, (t + 1) * TILE_ROWS)
        h = bv[sl] + a[sl] * carry
        hs[t] = h
        edge = h[0:1] if reverse else h[TILE_ROWS - 1:TILE_ROWS]
        carry = jnp.broadcast_to(edge, (TILE_ROWS, c))
    carry_ref[...] = carry
    h_all = jnp.concatenate(hs, axis=0)
    if reverse:
        o_ref[0] = (_gelu(ya_ref[0].astype(F32)) * (hf_ref[0] + h_all)).astype(o_ref.dtype)
    else:
        o_ref[0] = h_all


def lru_scan(z, xa_col, ya_col, conv_w, conv_b, wg, bg, negc, hf, *, reverse):
    b, s, _ = z.shape
    c = D_RNN
    n_chunks = s // ROW_TILE
    per = ROW_TILE // HALO_ROWS
    n8 = s // HALO_ROWS
    if reverse:
        cidx = lambda st: jnp.where(st == 0, 0, n_chunks - st)
    else:
        cidx = lambda st: st
    in_specs = [
        pl.BlockSpec((1, ROW_TILE, c), lambda bi, st: (bi, cidx(st), xa_col)),
        pl.BlockSpec((1, HALO_ROWS, c), lambda bi, st: (bi, jnp.maximum(cidx(st) * per - 1, 0), xa_col)),
        pl.BlockSpec((1, HALO_ROWS, c), lambda bi, st: (bi, jnp.minimum((cidx(st) + 1) * per, n8 - 1), xa_col)),
        pl.BlockSpec((CONV_W, c), lambda bi, st: (0, 0)),
        pl.BlockSpec((1, c), lambda bi, st: (0, 0)),
        pl.BlockSpec((1, c, 2 * c), lambda bi, st: (0, 0, 0)),
        pl.BlockSpec((1, 1, 2 * c), lambda bi, st: (0, 0, 0)),
        pl.BlockSpec((1, 1, c), lambda bi, st: (0, 0, 0)),
    ]
    args = [z, z, z, conv_w, conv_b, wg, bg, negc]
    if reverse:
        in_specs += [
            pl.BlockSpec((1, ROW_TILE, c), lambda bi, st: (bi, cidx(st), ya_col)),
            pl.BlockSpec((1, ROW_TILE, c), lambda bi, st: (bi, cidx(st), 0)),
        ]
        args += [z, hf]
    return pl.pallas_call(
        functools.partial(_lru_body, reverse=reverse, n_chunks=n_chunks),
        grid=(b, n_chunks),
        in_specs=in_specs,
        out_specs=pl.BlockSpec((1, ROW_TILE, c), lambda bi, st: (bi, cidx(st), 0)),
        out_shape=jax.ShapeDtypeStruct((b, s, c), BF16 if reverse else F32),
        scratch_shapes=[pltpu.VMEM((TILE_ROWS, c), F32)],
        compiler_params=_cparams(("parallel", "arbitrary")),
        name="lru_rev" if reverse else "lru_fwd",
    )(*args)


def _tile_lanes(t, n):
    return jnp.concatenate([t] * n, axis=1) if n > 1 else t


def _rope(x, c, s1, s2, r):
    w = x.shape[-1]
    n = w // V7X_LANES
    return (x * _tile_lanes(c, n) + pltpu.roll(x, w - r, axis=1) * _tile_lanes(s1, n)
            + pltpu.roll(x, r, axis=1) * _tile_lanes(s2, n))


def _split_dot(x, m_ref):
    hi = x.astype(BF16)
    lo = (x - hi.astype(F32)).astype(BF16)
    m = m_ref[...]
    return jnp.dot(hi, m, preferred_element_type=F32) + jnp.dot(lo, m, preferred_element_type=F32)


def _with_ones(v):
    ones = jnp.ones((v.shape[0], V7X_LANES), v.dtype)
    parts = []
    for c in range(v.shape[1] // V7X_LANES):
        parts += [v[:, c * V7X_LANES:(c + 1) * V7X_LANES], ones]
    return jnp.concatenate(parts, axis=1)


def _rms(x, g):
    return x * lax.rsqrt(jnp.mean(x * x, axis=-1, keepdims=True) + EPS) * g


def _prep_body(qb_ref, kb_ref, vb_ref, qg_ref, tail_ref,
               c64_ref, s164_ref, s264_ref, cm_ref, s1m_ref, s2m_ref,
               avg_ref, gq_ref, gk_ref, gcq_ref, gckv_ref, wuq_ref, wuk_ref, wuv_ref,
               qd_ref, kd_ref, vd_ref, qq_ref, kq_ref, vq_ref, qm_ref, km_ref, vm_ref):
    c64, s164, s264 = c64_ref[...], s164_ref[...], s264_ref[...]
    cm, s1m, s2m = cm_ref[...], s1m_ref[...], s2m_ref[...]
    half64 = DIFF_DK // 2
    qd_ref[0] = (_rope(qb_ref[0].astype(F32), c64, s164, s264, half64) * (DIFF_DK ** -0.5)).astype(BF16)
    kd_ref[0] = _rope(kb_ref[0].astype(F32), c64, s164, s264, half64).astype(BF16)
    vd_ref[0] = _with_ones(vb_ref[0].astype(BF16))
    qg = qg_ref[0].astype(F32)
    ms = _split_dot(qg * qg, avg_ref)
    qn = qg * lax.rsqrt(ms + EPS) * gq_ref[...]
    qq_ref[0] = (_rope(qn, c64, s164, s264, GQA_DH // 2) * (GQA_DH ** -0.5)).astype(BF16)
    tail = tail_ref[0].astype(F32)
    kg = tail[:, 0:128]
    msk = _split_dot(kg * kg, avg_ref.at[0:128, 0:128])
    kn = _rope(kg * lax.rsqrt(msk + EPS) * gk_ref[...], c64, s164, s264, GQA_DH // 2).astype(BF16)
    vg = tail[:, 128:256].astype(BF16)
    kq_ref[0] = jnp.concatenate([kn[:, 0:64], kn[:, 0:64], kn[:, 64:128], kn[:, 64:128]], axis=1)
    vq_ref[0] = _with_ones(jnp.concatenate([vg[:, 0:64], vg[:, 0:64], vg[:, 64:128], vg[:, 64:128]], axis=1))
    cq = tail[:, 256:256 + MLA_Q_RANK]
    ckv = tail[:, 640:640 + MLA_KV_RANK]
    kr = tail[:, 896:1024]
    qf = jnp.dot(_rms(cq, gcq_ref[...]).astype(BF16), wuq_ref[...], preferred_element_type=F32)
    scale = (MLA_NOPE + MLA_ROPE) ** -0.5
    qm_ref[0] = (_rope(qf, cm, s1m, s2m, MLA_ROPE // 2) * scale).astype(BF16)
    ckvn = _rms(ckv, gckv_ref[...]).astype(BF16)
    kf = jnp.dot(ckvn, wuk_ref[...], preferred_element_type=F32)
    krr = _rope(kr, cm, s1m, s2m, MLA_ROPE // 2)
    km_ref[0] = (kf + _tile_lanes(krr, MLA_HEADS)).astype(BF16)
    vm_ref[0] = _with_ones(jnp.dot(ckvn, wuv_ref[...], preferred_element_type=F32).astype(BF16))


def attn_prep(z, tabs, avg, gq, gk, gcq, gckv, wuq, wuk, wuv):
    b, s, _ = z.shape
    nt = s // ROW_TILE
    zspec = lambda w, idx: pl.BlockSpec((1, ROW_TILE, w), lambda ti, bi: (bi, ti, idx))
    tab = pl.BlockSpec((ROW_TILE, V7X_LANES), lambda ti, bi: (ti, 0))
    full = lambda a: pl.BlockSpec(a.shape, lambda ti, bi: (0,) * a.ndim)
    ospec = lambda w: pl.BlockSpec((1, ROW_TILE, w), lambda ti, bi: (bi, ti, 0))
    oshape = lambda w: jax.ShapeDtypeStruct((b, s, w), BF16)
    widths = (512, 512, 1024, 512, 256, 512, 1024, 1024, 1024)
    return pl.pallas_call(
        _prep_body,
        grid=(nt, b),
        in_specs=[zspec(512, 10), zspec(512, 11), zspec(512, 12), zspec(512, 13), zspec(1024, 7)]
        + [tab] * 6 + [full(a) for a in (avg, gq, gk, gcq, gckv, wuq, wuk, wuv)],
        out_specs=[ospec(w) for w in widths],
        out_shape=[oshape(w) for w in widths],
        compiler_params=_cparams(("parallel", "parallel")),
        name="attn_prep",
    )(z, z, z, z, z, *tabs, avg, gq, gk, gcq, gckv, wuq, wuk, wuv)


ATTN_GROUPS = 2


def _attn_body(lam_ref, q_ref, k_ref, v_ref, g_ref, o_ref, *, mode, out_scale, t0):
    ti = pl.program_id(2)
    lane = lax.broadcasted_iota(jnp.int32, (ROW_TILE, V7X_LANES), 1)
    lo_half = lane < (V7X_LANES // 2)

    ln = V7X_LANES

    def run(n_keys):
        for grp in range(ATTN_GROUPS):
            outs = []
            for j in range(2):
                if mode == "mla":
                    c0 = (2 * grp + j) * ln
                    qj = q_ref[0, :, c0:c0 + ln]
                    kj = k_ref[0, 0:n_keys, c0:c0 + ln]
                else:
                    q = q_ref[0, :, grp * ln:(grp + 1) * ln]
                    keep = lo_half if j == 0 else jnp.logical_not(lo_half)
                    qj = jnp.where(keep, q.astype(F32), 0.0).astype(BF16)
                    kc = 0 if mode == "pair" else grp * ln
                    kj = k_ref[0, 0:n_keys, kc:kc + ln]
                vc = 0 if mode == "pair" else grp * 2 * ln
                hk = (n_keys // (2 * ln)) * ln
                tn = (((1,), (1,)), ((), ()))
                s = jnp.concatenate(
                    [lax.dot_general(qj, kj[0:hk], tn, preferred_element_type=F32).astype(BF16),
                     lax.dot_general(qj, kj[hk:], tn, preferred_element_type=F32).astype(BF16)], axis=1)
                m = jnp.max(s, axis=-1, keepdims=True)
                p = jnp.exp(s - m)
                o = (jnp.dot(p[:, 0:hk], v_ref[0, 0:hk, vc:vc + 2 * ln], preferred_element_type=F32)
                     + jnp.dot(p[:, hk:], v_ref[0, hk:n_keys, vc:vc + 2 * ln], preferred_element_type=F32))
                outs.append(o[:, 0:ln] / o[:, ln:2 * ln])
            if mode == "diff":
                o = outs[0] - lam_ref[0] * outs[1]
                o = o * lax.rsqrt(jnp.mean(o * o, axis=-1, keepdims=True) + EPS) * g_ref[...] * out_scale
            else:
                o = jnp.where(lo_half, outs[0], outs[1])
            o_ref[0, :, grp * ln:(grp + 1) * ln] = o.astype(o_ref.dtype)

    if t0 == 0:
        @pl.when(ti == 0)
        def _():
            run(ROW_TILE)

        @pl.when(ti > 0)
        def _():
            run(k_ref.shape[1])
    else:
        run(k_ref.shape[1])


def attention(q, k, v, lam, g, *, mode, out_scale=1.0, t0=0):
    b, s, wq = q.shape
    steps = 4 // ATTN_GROUPS
    qw = wq // steps
    kw = k.shape[-1] // steps
    vw = v.shape[-1] // steps
    ow = ATTN_GROUPS * V7X_LANES
    nt = s // ROW_TILE - t0
    return pl.pallas_call(
        functools.partial(_attn_body, mode=mode, out_scale=out_scale, t0=t0),
        grid=(b, steps, nt),
        in_specs=[
            pl.BlockSpec(memory_space=pltpu.SMEM),
            pl.BlockSpec((1, ROW_TILE, qw), lambda bi, gi, ti: (bi, ti + t0, gi)),
            pl.BlockSpec((1, s, kw), lambda bi, gi, ti: (bi, 0, gi)),
            pl.BlockSpec((1, s, vw), lambda bi, gi, ti: (bi, 0, gi)),
            pl.BlockSpec((1, V7X_LANES), lambda bi, gi, ti: (0, 0)),
        ],
        out_specs=pl.BlockSpec((1, ROW_TILE, ow), lambda bi, gi, ti: (bi, ti, gi)),
        out_shape=jax.ShapeDtypeStruct((b, nt * ROW_TILE, 4 * V7X_LANES), BF16),
        compiler_params=_cparams(("parallel", "parallel", "arbitrary")),
        name="attn_" + mode,
    )(lam, q, k, v, g)


def _merge_body(oa_ref, ob_ref, og_ref, om_ref, zg_ref, wb_ref, wo_ref, x_ref, gate_ref, o_ref):
    d = x_ref.shape[-1]
    merged = None
    for k, o_k in enumerate((oa_ref, ob_ref, og_ref, om_ref)):
        t = jnp.dot(o_k[0], wb_ref[k], preferred_element_type=F32)
        t = t * _sigmoid(zg_ref[0, :, k * d:(k + 1) * d].astype(F32))
        merged = t if merged is None else merged + t
    y = jnp.dot(merged.astype(BF16), wo_ref[...], preferred_element_type=F32)
    o_ref[0] = x_ref[0] + gate_ref[0, 0] * y


def merge(oa, ob, og, om, z, wb, wo, x, gate, t0=0):
    b, s, d = x.shape
    nt = s // ROW_TILE - t0
    full = lambda w: pl.BlockSpec((1, ROW_TILE, w), lambda bi, ti: (bi, ti + t0, 0))
    part = pl.BlockSpec((1, ROW_TILE, BRANCH_W), lambda bi, ti: (bi, ti, 0))
    return pl.pallas_call(
        _merge_body,
        grid=(b, nt),
        in_specs=[full(BRANCH_W), part, part, part,
                  full(N_BRANCH * d),
                  pl.BlockSpec(wb.shape, lambda bi, ti: (0, 0, 0)),
                  pl.BlockSpec(wo.shape, lambda bi, ti: (0, 0)),
                  full(d),
                  pl.BlockSpec((1, 1, 1, d), lambda bi, ti: (bi, jnp.minimum(ti + t0, 1), 0, 0))],
        out_specs=pl.BlockSpec((1, ROW_TILE, d), lambda bi, ti: (bi, ti, 0)),
        out_shape=jax.ShapeDtypeStruct((b, nt * ROW_TILE, d), F32),
        compiler_params=_cparams(("parallel", "parallel")),
        name="merge",
    )(oa, ob, og, om, z, wb, wo, x, gate)


NEG_BIG = float(jnp.finfo(jnp.float32).min)


def _route_body(x_ref, g_ref, sh_ref, sc_ref, wq_ref, kb_ref,
                h_ref, a_ref, b_ref, e1_ref, e2_ref, st_ref, ta_ref, tb_ref):
    x = x_ref[0]
    ms = jnp.mean(x * x, axis=-1, keepdims=True)
    hf = x * lax.rsqrt(ms + EPS) * g_ref[...] * (1.0 + sc_ref[0, 0]) + sh_ref[0, 0]
    h_ref[0] = hf.T.astype(BF16)
    h = hf.astype(BF16)
    q = jnp.dot(h, wq_ref[...], preferred_element_type=F32).astype(BF16)
    nk = PEER_NKEYS
    for hd in range(PEER_HEADS):
        st_ref[hd * 2 * nk:(hd + 1) * 2 * nk, :] = lax.dot_general(
            kb_ref[hd], q[:, hd * PEER_DK:(hd + 1) * PEER_DK], (((1,), (1,)), ((), ())),
            preferred_element_type=F32)

    def top_rows(v, dst_ref):
        for r in range(PEER_TOPK):
            m = jnp.max(v, axis=0, keepdims=True)
            dst_ref[r:r + 1, :] = m
            v = jnp.where(v >= m, NEG_BIG, v)

    row8 = lax.broadcasted_iota(jnp.int32, (TILE_ROWS, ROW_TILE), 0)

    def head(hd, carry):
        base = pl.multiple_of(hd * (2 * nk), 2 * nk)
        s1 = st_ref[pl.ds(base, nk), :]
        s2 = st_ref[pl.ds(base + nk, nk), :]
        top_rows(s1, ta_ref)
        top_rows(s2, tb_ref)
        pieces = [ta_ref[0:1, :] + tb_ref[...]]
        tb8 = tb_ref[0:TILE_ROWS, :]
        for p in range(1, PEER_TOPK):
            n_q = PEER_TOPK // (p + 1)
            c = ta_ref[p:p + 1, :] + tb8
            pieces.append(c if n_q >= TILE_ROWS else jnp.where(row8 < n_q, c, NEG_BIG))
        cand = jnp.concatenate(pieces, axis=0)
        top = ta_ref[0:1, :] + tb_ref[0:1, :]
        zsum = jnp.zeros_like(top)
        m = top
        for r in range(PEER_TOPK):
            m = jnp.max(cand, axis=0, keepdims=True)
            zsum = zsum + jnp.exp(m - top)
            cand = jnp.where(cand >= m, NEG_BIG, cand)
        a_ref[hd] = m - s1
        e1_ref[hd] = jnp.exp(s1 - ta_ref[0:1, :]) / zsum
        e2 = jnp.exp(s2 - tb_ref[0:1, :])
        for lt in range(ROW_TILE // V7X_LANES):
            b_ref[hd, lt] = s2[:, lt * V7X_LANES:(lt + 1) * V7X_LANES]
            e2_ref[hd, lt] = e2[:, lt * V7X_LANES:(lt + 1) * V7X_LANES]
        return carry

    lax.fori_loop(0, PEER_HEADS, head, 0)


def peer_route(x, g, shift, scale, wq, kbig, t0=0):
    b, s, d = x.shape
    nt = s // ROW_TILE
    m = b * s
    seg = lambda bi, ti: (bi, jnp.minimum(ti + t0, 1), 0, 0)
    flat = lambda bi, ti: (0, 0, bi * nt + ti)
    kspec = pl.BlockSpec((PEER_HEADS, PEER_NKEYS, ROW_TILE), flat)
    kshape = jax.ShapeDtypeStruct((PEER_HEADS, PEER_NKEYS, m), F32)
    lt_per = ROW_TILE // V7X_LANES
    tspec = pl.BlockSpec((PEER_HEADS, lt_per, PEER_NKEYS, V7X_LANES), lambda bi, ti: (0, bi * nt + ti, 0, 0))
    tshape = jax.ShapeDtypeStruct((PEER_HEADS, m // V7X_LANES, PEER_NKEYS, V7X_LANES), F32)
    return pl.pallas_call(
        _route_body,
        grid=(b, nt),
        in_specs=[
            pl.BlockSpec((1, ROW_TILE, d), lambda bi, ti: (bi, ti, 0)),
            pl.BlockSpec((1, d), lambda bi, ti: (0, 0)),
            pl.BlockSpec((1, 1, 1, d), seg),
            pl.BlockSpec((1, 1, 1, d), seg),
            pl.BlockSpec(wq.shape, lambda bi, ti: (0, 0)),
            pl.BlockSpec(kbig.shape, lambda bi, ti: (0, 0, 0)),
        ],
        out_specs=[pl.BlockSpec((1, d, ROW_TILE), lambda bi, ti: (bi * nt + ti, 0, 0)),
                   kspec, tspec, kspec, tspec],
        out_shape=[jax.ShapeDtypeStruct((m // ROW_TILE, d, ROW_TILE), BF16), kshape, tshape, kshape, tshape],
        scratch_shapes=[pltpu.VMEM((2 * PEER_HEADS * PEER_NKEYS, ROW_TILE), F32),
                        pltpu.VMEM((PEER_TOPK, ROW_TILE), F32),
                        pltpu.VMEM((PEER_TOPK, ROW_TILE), F32)],
        compiler_params=_cparams(("parallel", "parallel")),
        name="peer_route",
    )(x, g, shift, scale, wq, kbig)


PEER_TOK = 2 * ROW_TILE
PEER_ECH = 1024


def _expert_body(h_ref, u0_ref, u_ref, vt_ref, a_ref, e1_ref, b_ref, e2_ref, x_ref, g0_ref, g1_ref,
                 fg_ref, o_ref, acc_ref, w0_ref, w1_ref, uf0_ref, uf1_ref, thb_ref, e1b_ref, *,
                 final, n_ch):
    e = pl.program_id(1)
    nk = PEER_NKEYS
    n_il = PEER_ECH // nk
    sub = TILE_ROWS
    ib = 2
    n_ib = n_il // ib
    tn = (((1,), (1,)), ((), ()))
    mq = V7X_MXU_DIM
    uf_refs = (uf0_ref, uf1_ref)
    w_refs = (w0_ref, w1_ref)

    def gate_block(g, cur):
        uf_ref, w_ref = uf_refs[cur], w_refs[cur]
        lt = g // n_ib
        i0 = (g % n_ib) * ib
        gates = [None] * ib
        for hd in range(PEER_HEADS):
            bj = b_ref[hd, lt].reshape(nk // sub, sub, V7X_LANES)
            e2j = e2_ref[hd, lt].reshape(nk // sub, sub, V7X_LANES)
            for k in range(ib):
                th = thb_ref[lt, hd * n_il + i0 + k][None]
                e1 = e1b_ref[lt, hd * n_il + i0 + k][None]
                t = jnp.where(bj >= th, e1 * e2j, 0.0)
                gates[k] = t if gates[k] is None else gates[k] + t
        lpm = mq // V7X_LANES
        wl = pl.ds(_mult((lt % lpm) * V7X_LANES, V7X_LANES), V7X_LANES)
        for k in range(ib):
            rows = pl.ds(_mult((i0 + k) * nk, nk), nk)
            act = _gelu(uf_ref[lt, rows, :]) * gates[k].reshape(nk, V7X_LANES)
            w_ref[lt // lpm, rows, wl] = act.astype(BF16)

    def k_major(ref, r0):
        return jnp.concatenate([ref[k, pl.ds(r0, mq), :] for k in range(ref.shape[0])], axis=1)

    def next_u(it, oth):
        r0 = _mult((it // 2) * mq, mq)
        res = jnp.dot(k_major(u_ref, r0), h_ref[it % 2], preferred_element_type=F32)
        for k in range(mq // V7X_LANES):
            uf_refs[oth][(it % 2) * (mq // V7X_LANES) + k, pl.ds(r0, mq), :] = (
                res[:, k * V7X_LANES:(k + 1) * V7X_LANES])

    def prev_out(it, oth):
        r0 = _mult((it // 2) * mq, mq)
        acc_ref[it % 2, pl.ds(r0, mq), :] += jnp.dot(
            k_major(vt_ref, r0), w_refs[oth][it % 2], preferred_element_type=F32)

    n_it = (PEER_ECH // mq) * (PEER_TOK // mq)
    per_it = (PEER_TOK // V7X_LANES) * n_ib // n_it

    def run_chunk(with_prev, cur):
        for lt in range(PEER_TOK // V7X_LANES):
            tl = slice(lt * V7X_LANES, (lt + 1) * V7X_LANES)
            for hd in range(PEER_HEADS):
                for il in range(n_il):
                    thb_ref[lt, hd * n_il + il] = jnp.broadcast_to(a_ref[hd, il:il + 1, tl], (sub, V7X_LANES))
                    e1b_ref[lt, hd * n_il + il] = jnp.broadcast_to(e1_ref[hd, il:il + 1, tl], (sub, V7X_LANES))

        def step(it, carry):
            for k in range(per_it):
                gate_block(it * per_it + k, cur)
            next_u(it, 1 - cur)
            if with_prev:
                prev_out(it, 1 - cur)
            return carry

        for it in range(n_it):
            step(it, 0)

    @pl.when(e == 0)
    def _():
        acc_ref[...] = jnp.zeros_like(acc_ref)
        u_all = jnp.concatenate([u0_ref[k] for k in range(u0_ref.shape[0])], axis=1)
        lpm = mq // V7X_LANES
        for c in range(PEER_TOK // mq):
            u_first = jnp.dot(u_all, h_ref[c], preferred_element_type=F32)
            for k in range(lpm):
                uf0_ref[c * lpm + k] = u_first[:, k * V7X_LANES:(k + 1) * V7X_LANES]
        run_chunk(False, 0)

    for par in range(2):
        @pl.when(jnp.logical_and(jnp.logical_and(e > 0, e < n_ch), e % 2 == par))
        def _():
            run_chunk(True, par)

    @pl.when(e == n_ch)
    def _():
        vt_all = jnp.concatenate([vt_ref[k] for k in range(vt_ref.shape[0])], axis=1)
        for half, gref in enumerate((g0_ref, g1_ref)):
            y = acc_ref[half] + jnp.dot(vt_all, w_refs[(n_ch - 1) % 2][half], preferred_element_type=F32)
            sl = slice(half * ROW_TILE, (half + 1) * ROW_TILE)
            xn = x_ref[sl, :] + gref[0, 0] * y.T
            if final:
                xn = xn * lax.rsqrt(jnp.mean(xn * xn, axis=-1, keepdims=True) + EPS) * fg_ref[...]
            o_ref[sl, :] = xn


def peer_experts(h, u_tab, vt_tab, a, e1, bm, e2, x, gate, fg, *, final, t0=0):
    b, s, d = x.shape
    nt = s // ROW_TILE
    m = b * s
    n_exp = u_tab.shape[1]
    i_per = PEER_ECH // PEER_NKEYS

    def gidx(half):
        def f(i, e):
            t = 2 * i + half
            return (t // nt, jnp.minimum(t % nt + t0, 1), 0, 0)
        return f

    n_ch = n_exp // PEER_ECH
    last = n_ch - 1
    mq = V7X_MXU_DIM
    n_lt = PEER_TOK // V7X_LANES
    assert mq == ROW_TILE and PEER_TOK == 2 * mq
    out = pl.pallas_call(
        functools.partial(_expert_body, final=final, n_ch=n_ch),
        grid=(m // PEER_TOK, n_ch + 1),
        in_specs=[
            pl.BlockSpec((PEER_TOK // mq, d, mq), lambda i, e: (i, 0, 0)),
            pl.BlockSpec((d // mq, PEER_ECH, mq), lambda i, e: (0, 0, 0)),
            pl.BlockSpec((d // mq, PEER_ECH, mq), lambda i, e: (0, jnp.minimum(e + 1, last), 0)),
            pl.BlockSpec((PEER_ECH // mq, d, mq), lambda i, e: (jnp.maximum(e - 1, 0), 0, 0)),
            pl.BlockSpec((PEER_HEADS, i_per, PEER_TOK), lambda i, e: (0, jnp.minimum(e, last), i)),
            pl.BlockSpec((PEER_HEADS, i_per, PEER_TOK), lambda i, e: (0, jnp.minimum(e, last), i)),
            pl.BlockSpec((PEER_HEADS, PEER_TOK // V7X_LANES, PEER_NKEYS, V7X_LANES), lambda i, e: (0, i, 0, 0)),
            pl.BlockSpec((PEER_HEADS, PEER_TOK // V7X_LANES, PEER_NKEYS, V7X_LANES), lambda i, e: (0, i, 0, 0)),
            pl.BlockSpec((PEER_TOK, d), lambda i, e: (i, 0)),
            pl.BlockSpec((1, 1, 1, d), gidx(0)),
            pl.BlockSpec((1, 1, 1, d), gidx(1)),
            pl.BlockSpec((1, d), lambda i, e: (0, 0)),
        ],
        out_specs=pl.BlockSpec((PEER_TOK, d), lambda i, e: (i, 0)),
        out_shape=jax.ShapeDtypeStruct((m, d), F32),
        scratch_shapes=[pltpu.VMEM((PEER_TOK // mq, d, mq), F32),
                        pltpu.VMEM((PEER_TOK // mq, PEER_ECH, mq), BF16),
                        pltpu.VMEM((PEER_TOK // mq, PEER_ECH, mq), BF16),
                        pltpu.VMEM((n_lt, PEER_ECH, V7X_LANES), F32),
                        pltpu.VMEM((n_lt, PEER_ECH, V7X_LANES), F32),
                        pltpu.VMEM((n_lt, PEER_HEADS * i_per, TILE_ROWS, V7X_LANES), F32),
                        pltpu.VMEM((n_lt, PEER_HEADS * i_per, TILE_ROWS, V7X_LANES), F32)],
        compiler_params=_cparams(("parallel", "arbitrary")),
        name="peer_experts",
    )(h, u_tab, u_tab, vt_tab, a, e1, bm, e2, x.reshape(m, d), gate, gate, fg)
    return out.reshape(b, s, d)


def _axial_tables(rows, rope_dim, ctx_len, lane0):
    n_freq = rope_dim // 4
    half = rope_dim // 2
    inv_freq = ROPE_THETA ** (-jnp.arange(n_freq, dtype=F32) / n_freq)
    r = jnp.repeat(jnp.arange(rows, dtype=F32), GRID_W)
    col = jnp.tile(jnp.arange(GRID_W, dtype=F32), rows)
    ang = jnp.concatenate([r[:, None] * inv_freq, col[:, None] * inv_freq], axis=-1)
    cos, sin = jnp.cos(ang), jnp.sin(ang)
    n = cos.shape[0]
    period = 64 if rope_dim == 64 else V7X_LANES
    c_blk = jnp.ones((n, period), F32).at[:, lane0:lane0 + rope_dim].set(jnp.concatenate([cos, cos], -1))
    s1_blk = jnp.zeros((n, period), F32).at[:, lane0:lane0 + half].set(-sin)
    s2_blk = jnp.zeros((n, period), F32).at[:, lane0 + half:lane0 + rope_dim].set(sin)
    reps = V7X_LANES // period
    out = []
    for blk, fill in ((c_blk, 1.0), (s1_blk, 0.0), (s2_blk, 0.0)):
        t = jnp.tile(blk, (1, reps))
        out.append(jnp.concatenate([jnp.full((ctx_len, V7X_LANES), fill, F32), t], axis=0))
    return out


def _block_diag(w):
    n, a, b = w.shape
    eye = jnp.eye(n, dtype=w.dtype)
    return (eye[:, None, :, None] * w[:, :, None, :]).reshape(n * a, n * b)


def kernel(x, c, ctx, c_ctx, w_mod, b_mod, norm1_g, norm2_g, w_in, conv_w, conv_b, lru_wa, lru_ba,
           lru_wi, lru_bi, lru_lambda, diff_lam, diff_subln_g, gqa_qnorm_g, gqa_knorm_g, mla_qnorm_g,
           mla_w_uq, mla_kvnorm_g, mla_w_ukv, w_branch, w_out, peer_wq, peer_keys, peer_u, peer_v,
           final_norm_g):
    bsz, seq, d = x.shape
    ctx_len = ctx.shape[1]
    depth = w_in.shape[0]
    assert ctx_len == ROW_TILE and seq % PEER_TOK == 0 and seq % GRID_W == 0
    rows = seq // GRID_W
    s_all = ctx_len + seq
    xs = jnp.concatenate([ctx, x], axis=1)

    tabs = _axial_tables(rows, DIFF_DK, ctx_len, 0) + _axial_tables(rows, MLA_ROPE, ctx_len, MLA_NOPE)
    grp = jnp.arange(BRANCH_W) // GQA_DH
    avg = (grp[:, None] == grp[None, :]).astype(BF16) * (1.0 / GQA_DH)
    sc_in = jnp.zeros((16, d), F32).at[:bsz].set(jax.nn.silu(c)).at[bsz].set(jax.nn.silu(c_ctx)).astype(BF16)
    zero1 = jnp.zeros((1,), F32)
    ones_g = jnp.ones((1, V7X_LANES), F32)

    for l in range(depth):
        lam_init = 0.8 - 0.6 * math.exp(-0.3 * l)
        mod_all = matmul(sc_in, w_mod[l].astype(BF16)) + b_mod[l]
        mod_b = mod_all[:bsz].reshape(bsz, N_MOD, d)
        mod_c = jnp.broadcast_to(mod_all[bsz].reshape(1, N_MOD, d), (bsz, N_MOD, d))
        mods = [jnp.stack([mod_c[:, k], mod_b[:, k]], axis=1)[:, :, None, :] for k in range(N_MOD)]

        wl = w_in[l]
        o = 0
        parts = []
        for w in (512, 512, 512, 512, 512, 512, 128, 128, 384, 256, 32, 4096):
            parts.append(wl[:, o:o + w])
            o += w
        xa, ya, qb, kb, vb, qg, kg, vg, cq, ckv, kr, zg = parts
        kr_blk = jnp.zeros((d, V7X_LANES), F32).at[:, MLA_NOPE:MLA_NOPE + MLA_ROPE].set(kr)
        w_cat = jnp.concatenate([zg, xa, ya, qb, kb, vb, qg, kg, vg, cq, ckv, kr_blk], axis=1).astype(BF16)

        h = adaln(xs, norm1_g[l][None], mods[0], mods[1], 0)
        z = matmul(h.reshape(bsz * s_all, d), w_cat, out_dtype=BF16, tm=1024, tn=1024)
        z = z.reshape(bsz, s_all, w_cat.shape[1])

        negc = (-LRU_C * jax.nn.softplus(-lru_lambda[l]))[:, None, None, :]
        wg = jnp.stack([jnp.concatenate([_block_diag(lru_wa[l, dd]), _block_diag(lru_wi[l, dd])], axis=1)
                        for dd in range(2)]).astype(BF16)[:, None]
        bg = jnp.stack([jnp.concatenate([lru_ba[l, dd], lru_bi[l, dd]]) for dd in range(2)])[:, None, None, :]
        cb = conv_b[l][None]
        hf = lru_scan(z, 8, 9, conv_w[l], cb, wg[0], bg[0], negc[0], None, reverse=False)
        oa = lru_scan(z, 8, 9, conv_w[l], cb, wg[1], bg[1], negc[1], hf, reverse=True)

        uq = mla_w_uq[l].reshape(MLA_Q_RANK, MLA_HEADS, MLA_NOPE + MLA_ROPE)
        wuq = jnp.pad(uq, ((0, 0), (0, 0), (0, V7X_LANES - MLA_NOPE - MLA_ROPE))).reshape(MLA_Q_RANK, -1)
        ukv = mla_w_ukv[l].reshape(MLA_KV_RANK, MLA_HEADS, MLA_NOPE + MLA_DV)
        wuk = jnp.pad(ukv[:, :, :MLA_NOPE], ((0, 0), (0, 0), (0, V7X_LANES - MLA_NOPE))).reshape(MLA_KV_RANK, -1)
        wuv = ukv[:, :, MLA_NOPE:].reshape(MLA_KV_RANK, -1)
        qd, kd, vd, qq, kq, vq, qm, km, vm = attn_prep(
            z, tabs, avg, jnp.tile(gqa_qnorm_g[l], GQA_HEADS)[None], jnp.tile(gqa_knorm_g[l], GQA_KV_HEADS)[None],
            mla_qnorm_g[l][None], mla_kvnorm_g[l][None], wuq.astype(BF16), wuk.astype(BF16), wuv.astype(BF16))
        lv = diff_lam[l]
        lam = (jnp.exp(jnp.sum(lv[0] * lv[1])) - jnp.exp(jnp.sum(lv[2] * lv[3])) + lam_init).reshape(1)
        t0 = 1 if l == depth - 1 else 0
        ob = attention(qd, kd, vd, lam, diff_subln_g[l][None], mode="diff", out_scale=1.0 - lam_init, t0=t0)
        og = attention(qq, kq, vq, zero1, ones_g, mode="pair", t0=t0)
        om = attention(qm, km, vm, zero1, ones_g, mode="mla", t0=t0)

        xs = merge(oa, ob, og, om, z, w_branch[l].astype(BF16), w_out[l].astype(BF16), xs, mods[2], t0=t0)

        kbig = jnp.stack([_block_diag(peer_keys[l, hd]) for hd in range(PEER_HEADS)]).astype(BF16)
        h2, pa, pb, pe1, pe2 = peer_route(xs, norm2_g[l][None], mods[3], mods[4], peer_wq[l].astype(BF16), kbig,
                                          t0=t0)
        n_exp = peer_u.shape[1]
        u_km = peer_u[l].astype(BF16).reshape(n_exp, d // V7X_MXU_DIM, V7X_MXU_DIM).transpose(1, 0, 2)
        vt_km = peer_v[l].astype(BF16).reshape(n_exp // V7X_MXU_DIM, V7X_MXU_DIM, d).transpose(0, 2, 1)
        xs = peer_experts(h2, u_km, vt_km, pa, pe1, pb, pe2, xs, mods[5],
                          final_norm_g[None], final=(l == depth - 1), t0=t0)
    return xs
```

```python
import functools
import math

import jax
import jax.numpy as jnp
from jax import lax
from jax.experimental import pallas as pl
from jax.experimental.pallas import tpu as pltpu

F32 = jnp.float32
BF16 = jnp.bfloat16

EPS = 1e-6
GRID_W = 64
ROPE_THETA = 10000.0
N_MOD = 6

D_RNN = 512
RNN_BLOCKS = 8
CONV_W = 4
LRU_C = 8.0

DIFF_HEADS = 4
DIFF_DK = 64
DIFF_DV = 128
GQA_HEADS = 8
GQA_KV_HEADS = 2
GQA_DH = 64
MLA_HEADS = 8
MLA_NOPE = 64
MLA_ROPE = 32
MLA_DV = 64
MLA_Q_RANK = 384
MLA_KV_RANK = 256
N_BRANCH = 4
BRANCH_W = 512

PEER_HEADS = 8
PEER_NKEYS = 128
PEER_DK = 128
PEER_TOPK = 16

V7X_LANES = 128
V7X_SUBLANES = 8
V7X_VMEM_BYTES = 64 * 1024 * 1024
V7X_MXU_DIM = 256
VMEM_LIMIT = 48 * 1024 * 1024

ROW_TILE = 256


def _cparams(sem):
    return pltpu.CompilerParams(dimension_semantics=sem, vmem_limit_bytes=VMEM_LIMIT)


def _adaln_body(x_ref, g_ref, sh_ref, sc_ref, o_ref):
    x = x_ref[0]
    ms = jnp.mean(x * x, axis=-1, keepdims=True)
    y = x * lax.rsqrt(ms + EPS) * g_ref[...]
    o_ref[0] = (y * (1.0 + sc_ref[0, 0]) + sh_ref[0, 0]).astype(o_ref.dtype)


def adaln(x, g, shift, scale, t0):
    b, s, d = x.shape
    nt = s // ROW_TILE - t0
    seg = lambda bi, ti: (bi, jnp.minimum(ti + t0, 1), 0, 0)
    return pl.pallas_call(
        _adaln_body,
        grid=(b, nt),
        in_specs=[
            pl.BlockSpec((1, ROW_TILE, d), lambda bi, ti: (bi, ti + t0, 0)),
            pl.BlockSpec((1, d), lambda bi, ti: (0, 0)),
            pl.BlockSpec((1, 1, 1, d), seg),
            pl.BlockSpec((1, 1, 1, d), seg),
        ],
        out_specs=pl.BlockSpec((1, ROW_TILE, d), lambda bi, ti: (bi, ti, 0)),
        out_shape=jax.ShapeDtypeStruct((b, nt * ROW_TILE, d), BF16),
        compiler_params=_cparams(("parallel", "parallel")),
        name="adaln",
    )(x, g, shift, scale)


def _mm_body(a_ref, w_ref, o_ref):
    o_ref[...] = jnp.dot(a_ref[...], w_ref[...], preferred_element_type=F32).astype(o_ref.dtype)


def matmul(a, w, out_dtype=F32, tm=512, tn=512):
    m, k = a.shape
    _, n = w.shape
    tm = math.gcd(tm, m)
    tn = math.gcd(tn, n)
    return pl.pallas_call(
        _mm_body,
        grid=(n // tn, m // tm),
        in_specs=[
            pl.BlockSpec((tm, k), lambda j, i: (i, 0)),
            pl.BlockSpec((k, tn), lambda j, i: (0, j)),
        ],
        out_specs=pl.BlockSpec((tm, tn), lambda j, i: (i, j)),
        out_shape=jax.ShapeDtypeStruct((m, n), out_dtype),
        compiler_params=_cparams(("parallel", "parallel")),
        name="matmul",
    )(a, w)


def _mult(x, m):
    return x if isinstance(x, int) else pl.multiple_of(x, m)


def _gelu(x):
    c = math.sqrt(2.0 / math.pi)
    return x * (0.5 * (1.0 + jnp.tanh(c * (x + 0.044715 * (x * x * x)))))


def _sigmoid(x):
    return 1.0 / (1.0 + jnp.exp(-x))


TILE_ROWS = V7X_SUBLANES
HALO_ROWS = 2 * V7X_SUBLANES


def _lru_body(x_ref, xp_ref, xn_ref, cw_ref, cb_ref, wg_ref, bg_ref, nc_ref, *rest,
              reverse, n_chunks):
    if reverse:
        ya_ref, hf_ref, o_ref, carry_ref = rest
    else:
        o_ref, carry_ref = rest
    step = pl.program_id(1)
    if reverse:
        chunk = jnp.where(step == 0, 0, n_chunks - step)
    else:
        chunk = step
    rows = ROW_TILE
    c = x_ref.shape[-1]

    @pl.when(step == 0)
    def _():
        carry_ref[...] = jnp.zeros_like(carry_ref)

    x = x_ref[0].astype(F32)
    xx = jnp.concatenate([xp_ref[0].astype(F32), x, xn_ref[0].astype(F32)], axis=0)
    n_xx = rows + 2 * HALO_ROWS
    row = lax.broadcasted_iota(jnp.int32, (rows, c), 0)
    seg_start = jnp.logical_or(chunk == 0, chunk == 1)
    seg_end = jnp.logical_or(chunk == 0, chunk == n_chunks - 1)

    def shifted(d):
        return pltpu.roll(xx, (n_xx - d) % n_xx, axis=0)[HALO_ROWS:HALO_ROWS + rows]

    x_m1 = jnp.where(jnp.logical_and(seg_start, row == 0), 0.0, shifted(-1))
    x_p1 = jnp.where(jnp.logical_and(seg_end, row >= rows - 1), 0.0, shifted(1))
    x_p2 = jnp.where(jnp.logical_and(seg_end, row >= rows - 2), 0.0, shifted(2))
    cw = cw_ref[...]
    u = cw[0:1] * x_m1 + cw[1:2] * x + cw[2:3] * x_p1 + cw[3:4] * x_p2 + cb_ref[...]

    z = jnp.dot(u.astype(BF16), wg_ref[0], preferred_element_type=F32) + bg_ref[0]
    r = _sigmoid(z[:, :c])
    gi = _sigmoid(z[:, c:])
    a = jnp.exp(nc_ref[0] * r)
    bv = jnp.sqrt(1.0 - a * a) * (gi * u)

    sub = row % TILE_ROWS
    for dstep in (1, 2, 4):
        if reverse:
            sh = (rows - dstep) % rows
            keep = sub <= TILE_ROWS - 1 - dstep
        else:
            sh = dstep
            keep = sub >= dstep
        a_sh = jnp.where(keep, pltpu.roll(a, sh, axis=0), 1.0)
        b_sh = jnp.where(keep, pltpu.roll(bv, sh, axis=0), 0.0)
        bv = a * b_sh + bv
        a = a * a_sh

    carry = carry_ref[...]
    n_tiles = rows // TILE_ROWS
    order = range(n_tiles - 1, -1, -1) if reverse else range(n_tiles)
    hs = [None] * n_tiles
    for t in order:
        sl = slice(t * TILE_ROWS, (t + 1) * TILE_ROWS)
        h = bv[sl] + a[sl] * carry
        hs[t] = h
        edge = h[0:1] if reverse else h[TILE_ROWS - 1:TILE_ROWS]
        carry = jnp.broadcast_to(edge, (TILE_ROWS, c))
    carry_ref[...] = carry
    h_all = jnp.concatenate(hs, axis=0)
    if reverse:
        o_ref[0] = (_gelu(ya_ref[0].astype(F32)) * (hf_ref[0] + h_all)).astype(o_ref.dtype)
    else:
        o_ref[0] = h_all


def lru_scan(z, xa_col, ya_col, conv_w, conv_b, wg, bg, negc, hf, *, reverse):
    b, s, _ = z.shape
    c = D_RNN
    n_chunks = s // ROW_TILE
    per = ROW_TILE // HALO_ROWS
    n8 = s // HALO_ROWS
    if reverse:
        cidx = lambda st: jnp.where(st == 0, 0, n_chunks - st)
    else:
        cidx = lambda st: st
    in_specs = [
        pl.BlockSpec((1, ROW_TILE, c), lambda bi, st: (bi, cidx(st), xa_col)),
        pl.BlockSpec((1, HALO_ROWS, c), lambda bi, st: (bi, jnp.maximum(cidx(st) * per - 1, 0), xa_col)),
        pl.BlockSpec((1, HALO_ROWS, c), lambda bi, st: (bi, jnp.minimum((cidx(st) + 1) * per, n8 - 1), xa_col)),
        pl.BlockSpec((CONV_W, c), lambda bi, st: (0, 0)),
        pl.BlockSpec((1, c), lambda bi, st: (0, 0)),
        pl.BlockSpec((1, c, 2 * c), lambda bi, st: (0, 0, 0)),
        pl.BlockSpec((1, 1, 2 * c), lambda bi, st: (0, 0, 0)),
        pl.BlockSpec((1, 1, c), lambda bi, st: (0, 0, 0)),
    ]
    args = [z, z, z, conv_w, conv_b, wg, bg, negc]
    if reverse:
        in_specs += [
            pl.BlockSpec((1, ROW_TILE, c), lambda bi, st: (bi, cidx(st), ya_col)),
            pl.BlockSpec((1, ROW_TILE, c), lambda bi, st: (bi, cidx(st), 0)),
        ]
        args += [z, hf]
    return pl.pallas_call(
        functools.partial(_lru_body, reverse=reverse, n_chunks=n_chunks),
        grid=(b, n_chunks),
        in_specs=in_specs,
        out_specs=pl.BlockSpec((1, ROW_TILE, c), lambda bi, st: (bi, cidx(st), 0)),
        out_shape=jax.ShapeDtypeStruct((b, s, c), BF16 if reverse else F32),
        scratch_shapes=[pltpu.VMEM((TILE_ROWS, c), F32)],
        compiler_params=_cparams(("parallel", "arbitrary")),
        name="lru_rev" if reverse else "lru_fwd",
    )(*args)


def _tile_lanes(t, n):
    return jnp.concatenate([t] * n, axis=1) if n > 1 else t


def _rope(x, c, s1, s2, r):
    w = x.shape[-1]
    n = w // V7X_LANES
    return (x * _tile_lanes(c, n) + pltpu.roll(x, w - r, axis=1) * _tile_lanes(s1, n)
            + pltpu.roll(x, r, axis=1) * _tile_lanes(s2, n))


def _split_dot(x, m_ref):
    hi = x.astype(BF16)
    lo = (x - hi.astype(F32)).astype(BF16)
    m = m_ref[...]
    return jnp.dot(hi, m, preferred_element_type=F32) + jnp.dot(lo, m, preferred_element_type=F32)


def _with_ones(v):
    ones = jnp.ones((v.shape[0], V7X_LANES), v.dtype)
    parts = []
    for c in range(v.shape[1] // V7X_LANES):
        parts += [v[:, c * V7X_LANES:(c + 1) * V7X_LANES], ones]
    return jnp.concatenate(parts, axis=1)


def _rms(x, g):
    return x * lax.rsqrt(jnp.mean(x * x, axis=-1, keepdims=True) + EPS) * g


def _prep_body(qb_ref, kb_ref, vb_ref, qg_ref, tail_ref,
               c64_ref, s164_ref, s264_ref, cm_ref, s1m_ref, s2m_ref,
               avg_ref, gq_ref, gk_ref, gcq_ref, gckv_ref, wuq_ref, wuk_ref, wuv_ref,
               qd_ref, kd_ref, vd_ref, qq_ref, kq_ref, vq_ref, qm_ref, km_ref, vm_ref):
    c64, s164, s264 = c64_ref[...], s164_ref[...], s264_ref[...]
    cm, s1m, s2m = cm_ref[...], s1m_ref[...], s2m_ref[...]
    half64 = DIFF_DK // 2
    qd_ref[0] = (_rope(qb_ref[0].astype(F32), c64, s164, s264, half64) * (DIFF_DK ** -0.5)).astype(BF16)
    kd_ref[0] = _rope(kb_ref[0].astype(F32), c64, s164, s264, half64).astype(BF16)
    vd_ref[0] = _with_ones(vb_ref[0].astype(BF16))
    qg = qg_ref[0].astype(F32)
    ms = _split_dot(qg * qg, avg_ref)
    qn = qg * lax.rsqrt(ms + EPS) * gq_ref[...]
    qq_ref[0] = (_rope(qn, c64, s164, s264, GQA_DH // 2) * (GQA_DH ** -0.5)).astype(BF16)
    tail = tail_ref[0].astype(F32)
    kg = tail[:, 0:128]
    msk = _split_dot(kg * kg, avg_ref.at[0:128, 0:128])
    kn = _rope(kg * lax.rsqrt(msk + EPS) * gk_ref[...], c64, s164, s264, GQA_DH // 2).astype(BF16)
    vg = tail[:, 128:256].astype(BF16)
    kq_ref[0] = jnp.concatenate([kn[:, 0:64], kn[:, 0:64], kn[:, 64:128], kn[:, 64:128]], axis=1)
    vq_ref[0] = _with_ones(jnp.concatenate([vg[:, 0:64], vg[:, 0:64], vg[:, 64:128], vg[:, 64:128]], axis=1))
    cq = tail[:, 256:256 + MLA_Q_RANK]
    ckv = tail[:, 640:640 + MLA_KV_RANK]
    kr = tail[:, 896:1024]
    qf = jnp.dot(_rms(cq, gcq_ref[...]).astype(BF16), wuq_ref[...], preferred_element_type=F32)
    scale = (MLA_NOPE + MLA_ROPE) ** -0.5
    qm_ref[0] = (_rope(qf, cm, s1m, s2m, MLA_ROPE // 2) * scale).astype(BF16)
    ckvn = _rms(ckv, gckv_ref[...]).astype(BF16)
    kf = jnp.dot(ckvn, wuk_ref[...], preferred_element_type=F32)
    krr = _rope(kr, cm, s1m, s2m, MLA_ROPE // 2)
    km_ref[0] = (kf + _tile_lanes(krr, MLA_HEADS)).astype(BF16)
    vm_ref[0] = _with_ones(jnp.dot(ckvn, wuv_ref[...], preferred_element_type=F32).astype(BF16))


def attn_prep(z, tabs, avg, gq, gk, gcq, gckv, wuq, wuk, wuv):
    b, s, _ = z.shape
    nt = s // ROW_TILE
    zspec = lambda w, idx: pl.BlockSpec((1, ROW_TILE, w), lambda ti, bi: (bi, ti, idx))
    tab = pl.BlockSpec((ROW_TILE, V7X_LANES), lambda ti, bi: (ti, 0))
    full = lambda a: pl.BlockSpec(a.shape, lambda ti, bi: (0,) * a.ndim)
    ospec = lambda w: pl.BlockSpec((1, ROW_TILE, w), lambda ti, bi: (bi, ti, 0))
    oshape = lambda w: jax.ShapeDtypeStruct((b, s, w), BF16)
    widths = (512, 512, 1024, 512, 256, 512, 1024, 1024, 1024)
    return pl.pallas_call(
        _prep_body,
        grid=(nt, b),
        in_specs=[zspec(512, 10), zspec(512, 11), zspec(512, 12), zspec(512, 13), zspec(1024, 7)]
        + [tab] * 6 + [full(a) for a in (avg, gq, gk, gcq, gckv, wuq, wuk, wuv)],
        out_specs=[ospec(w) for w in widths],
        out_shape=[oshape(w) for w in widths],
        compiler_params=_cparams(("parallel", "parallel")),
        name="attn_prep",
    )(z, z, z, z, z, *tabs, avg, gq, gk, gcq, gckv, wuq, wuk, wuv)


ATTN_GROUPS = 2


def _attn_body(lam_ref, q_ref, k_ref, v_ref, g_ref, o_ref, *, mode, out_scale, t0):
    ti = pl.program_id(2)
    lane = lax.broadcasted_iota(jnp.int32, (ROW_TILE, V7X_LANES), 1)
    lo_half = lane < (V7X_LANES // 2)

    ln = V7X_LANES

    def run(n_keys):
        for grp in range(ATTN_GROUPS):
            outs = []
            for j in range(2):
                if mode == "mla":
                    c0 = (2 * grp + j) * ln
                    qj = q_ref[0, :, c0:c0 + ln]
                    kj = k_ref[0, 0:n_keys, c0:c0 + ln]
                else:
                    q = q_ref[0, :, grp * ln:(grp + 1) * ln]
                    keep = lo_half if j == 0 else jnp.logical_not(lo_half)
                    qj = jnp.where(keep, q.astype(F32), 0.0).astype(BF16)
                    kc = 0 if mode == "pair" else grp * ln
                    kj = k_ref[0, 0:n_keys, kc:kc + ln]
                vc = 0 if mode == "pair" else grp * 2 * ln
                hk = (n_keys // (2 * ln)) * ln
                tn = (((1,), (1,)), ((), ()))
                s = jnp.concatenate(
                    [lax.dot_general(qj, kj[0:hk], tn, preferred_element_type=F32).astype(BF16),
                     lax.dot_general(qj, kj[hk:], tn, preferred_element_type=F32).astype(BF16)], axis=1)
                m = jnp.max(s, axis=-1, keepdims=True)
                p = jnp.exp(s - m)
                o = (jnp.dot(p[:, 0:hk], v_ref[0, 0:hk, vc:vc + 2 * ln], preferred_element_type=F32)
                     + jnp.dot(p[:, hk:], v_ref[0, hk:n_keys, vc:vc + 2 * ln], preferred_element_type=F32))
                outs.append(o[:, 0:ln] / o[:, ln:2 * ln])
            if mode == "diff":
                o = outs[0] - lam_ref[0] * outs[1]
                o = o * lax.rsqrt(jnp.mean(o * o, axis=-1, keepdims=True) + EPS) * g_ref[...] * out_scale
            else:
                o = jnp.where(lo_half, outs[0], outs[1])
            o_ref[0, :, grp * ln:(grp + 1) * ln] = o.astype(o_ref.dtype)

    if t0 == 0:
        @pl.when(ti == 0)
        def _():
            run(ROW_TILE)

        @pl.when(ti > 0)
        def _():
            run(k_ref.shape[1])
    else:
        run(k_ref.shape[1])


def attention(q, k, v, lam, g, *, mode, out_scale=1.0, t0=0):
    b, s, wq = q.shape
    steps = 4 // ATTN_GROUPS
    qw = wq // steps
    kw = k.shape[-1] // steps
    vw = v.shape[-1] // steps
    ow = ATTN_GROUPS * V7X_LANES
    nt = s // ROW_TILE - t0
    return pl.pallas_call(
        functools.partial(_attn_body, mode=mode, out_scale=out_scale, t0=t0),
        grid=(b, steps, nt),
        in_specs=[
            pl.BlockSpec(memory_space=pltpu.SMEM),
            pl.BlockSpec((1, ROW_TILE, qw), lambda bi, gi, ti: (bi, ti + t0, gi)),
            pl.BlockSpec((1, s, kw), lambda bi, gi, ti: (bi, 0, gi)),
            pl.BlockSpec((1, s, vw), lambda bi, gi, ti: (bi, 0, gi)),
            pl.BlockSpec((1, V7X_LANES), lambda bi, gi, ti: (0, 0)),
        ],
        out_specs=pl.BlockSpec((1, ROW_TILE, ow), lambda bi, gi, ti: (bi, ti, gi)),
        out_shape=jax.ShapeDtypeStruct((b, nt * ROW_TILE, 4 * V7X_LANES), BF16),
        compiler_params=_cparams(("parallel", "parallel", "arbitrary")),
        name="attn_" + mode,
    )(lam, q, k, v, g)


def _merge_body(oa_ref, ob_ref, og_ref, om_ref, zg_ref, wb_ref, wo_ref, x_ref, gate_ref, o_ref):
    d = x_ref.shape[-1]
    merged = None
    for k, o_k in enumerate((oa_ref, ob_ref, og_ref, om_ref)):
        t = jnp.dot(o_k[0], wb_ref[k], preferred_element_type=F32)
        t = t * _sigmoid(zg_ref[0, :, k * d:(k + 1) * d].astype(F32))
        merged = t if merged is None else merged + t
    y = jnp.dot(merged.astype(BF16), wo_ref[...], preferred_element_type=F32)
    o_ref[0] = x_ref[0] + gate_ref[0, 0] * y


def merge(oa, ob, og, om, z, wb, wo, x, gate, t0=0):
    b, s, d = x.shape
    nt = s // ROW_TILE - t0
    full = lambda w: pl.BlockSpec((1, ROW_TILE, w), lambda bi, ti: (bi, ti + t0, 0))
    part = pl.BlockSpec((1, ROW_TILE, BRANCH_W), lambda bi, ti: (bi, ti, 0))
    return pl.pallas_call(
        _merge_body,
        grid=(b, nt),
        in_specs=[full(BRANCH_W), part, part, part,
                  full(N_BRANCH * d),
                  pl.BlockSpec(wb.shape, lambda bi, ti: (0, 0, 0)),
                  pl.BlockSpec(wo.shape, lambda bi, ti: (0, 0)),
                  full(d),
                  pl.BlockSpec((1, 1, 1, d), lambda bi, ti: (bi, jnp.minimum(ti + t0, 1), 0, 0))],
        out_specs=pl.BlockSpec((1, ROW_TILE, d), lambda bi, ti: (bi, ti, 0)),
        out_shape=jax.ShapeDtypeStruct((b, nt * ROW_TILE, d), F32),
        compiler_params=_cparams(("parallel", "parallel")),
        name="merge",
    )(oa, ob, og, om, z, wb, wo, x, gate)


NEG_BIG = float(jnp.finfo(jnp.float32).min)


def _oddeven_merge(lo, hi, r):
    step = r * 2
    if step < hi - lo:
        yield from _oddeven_merge(lo, hi, step)
        yield from _oddeven_merge(lo + r, hi, step)
        yield from [(i, i + r) for i in range(lo + r, hi - r, step)]
    else:
        yield (lo, lo + r)


def _oddeven_sort(lo, hi):
    if hi - lo >= 1:
        mid = lo + (hi - lo) // 2
        yield from _oddeven_sort(lo, mid)
        yield from _oddeven_sort(mid + 1, hi)
        yield from _oddeven_merge(lo, hi, 1)


def _bitonic_merge(n):
    pairs, s = [], n // 2
    while s >= 1:
        pairs += [(i, i + s) for i in range(n) if (i // s) % 2 == 0]
        s //= 2
    return pairs


_SORT16 = tuple(_oddeven_sort(0, PEER_TOPK - 1))
_BITONIC16 = tuple(_bitonic_merge(PEER_TOPK))


def _route_body(x_ref, g_ref, sh_ref, sc_ref, wq_ref, kb_ref,
                h_ref, a_ref, b_ref, e1_ref, e2_ref, st_ref):
    x = x_ref[0]
    ms = jnp.mean(x * x, axis=-1, keepdims=True)
    hf = x * lax.rsqrt(ms + EPS) * g_ref[...] * (1.0 + sc_ref[0, 0]) + sh_ref[0, 0]
    h_ref[0] = hf.T.astype(BF16)
    h = hf.astype(BF16)
    q = jnp.dot(h, wq_ref[...], preferred_element_type=F32).astype(BF16)
    nk = PEER_NKEYS
    for hd in range(PEER_HEADS):
        st_ref[hd * 2 * nk:(hd + 1) * 2 * nk, :] = lax.dot_general(
            kb_ref[hd], q[:, hd * PEER_DK:(hd + 1) * PEER_DK], (((1,), (1,)), ((), ())),
            preferred_element_type=F32)

    def exchange(t, pairs):
        for i, j in pairs:
            t[i], t[j] = jnp.maximum(t[i], t[j]), jnp.minimum(t[i], t[j])

    def merge_top(a, b):
        t = [jnp.maximum(a[k], b[PEER_TOPK - 1 - k]) for k in range(PEER_TOPK)]
        exchange(t, _BITONIC16)
        return t

    def across_sublanes(t):
        for sh in (4, 2, 1):
            t = merge_top(t, [pltpu.roll(x, sh, axis=0) for x in t])
        return t

    def top16(s):
        t = [s[k * TILE_ROWS:(k + 1) * TILE_ROWS] for k in range(nk // TILE_ROWS)]
        exchange(t, _SORT16)
        return across_sublanes(t)

    row8 = lax.broadcasted_iota(jnp.int32, (TILE_ROWS, V7X_LANES), 0)
    n_q = jnp.zeros((TILE_ROWS, V7X_LANES), jnp.int32)
    for p in range(TILE_ROWS):
        n_q = jnp.where(row8 == p, PEER_TOPK // (p + 1), n_q)

    def rank_rows(t, lo):
        out = t[lo]
        for r in range(1, TILE_ROWS):
            out = jnp.where(row8 == r, t[lo + r], out)
        return out

    def head(it, carry):
        hd = it // (ROW_TILE // V7X_LANES)
        lt = it % (ROW_TILE // V7X_LANES)
        base = pl.multiple_of(hd * (2 * nk), 2 * nk)
        tl = pl.ds(pl.multiple_of(lt * V7X_LANES, V7X_LANES), V7X_LANES)
        s1 = st_ref[pl.ds(base, nk), tl]
        s2 = st_ref[pl.ds(base + nk, nk), tl]
        ta = top16(s1)
        tb = top16(s2)
        lo = rank_rows(ta, 0)
        cand = [jnp.where(n_q > q, lo + tb[q], NEG_BIG) for q in range(PEER_TOPK)]
        cand[PEER_TOPK - 1] = jnp.maximum(cand[PEER_TOPK - 1], rank_rows(ta, TILE_ROWS) + tb[0])
        exchange(cand, _BITONIC16)
        cand = across_sublanes(cand)
        top = cand[0][0:1]
        tau = cand[PEER_TOPK - 1][0:1]
        zsum = jnp.zeros_like(top)
        for k in range(PEER_TOPK):
            zsum = zsum + jnp.exp(cand[k][0:1] - top)
        a_ref[hd, :, tl] = tau - s1
        e1_ref[hd, :, tl] = jnp.exp(s1 - ta[0][0:1]) / zsum
        b_ref[hd, lt] = s2
        e2_ref[hd, lt] = jnp.exp(s2 - tb[0][0:1])
        return carry

    lax.fori_loop(0, PEER_HEADS * (ROW_TILE // V7X_LANES), head, 0)


def peer_route(x, g, shift, scale, wq, kbig, t0=0):
    b, s, d = x.shape
    nt = s // ROW_TILE
    m = b * s
    seg = lambda bi, ti: (bi, jnp.minimum(ti + t0, 1), 0, 0)
    flat = lambda bi, ti: (0, 0, bi * nt + ti)
    kspec = pl.BlockSpec((PEER_HEADS, PEER_NKEYS, ROW_TILE), flat)
    kshape = jax.ShapeDtypeStruct((PEER_HEADS, PEER_NKEYS, m), F32)
    lt_per = ROW_TILE // V7X_LANES
    tspec = pl.BlockSpec((PEER_HEADS, lt_per, PEER_NKEYS, V7X_LANES), lambda bi, ti: (0, bi * nt + ti, 0, 0))
    tshape = jax.ShapeDtypeStruct((PEER_HEADS, m // V7X_LANES, PEER_NKEYS, V7X_LANES), F32)
    return pl.pallas_call(
        _route_body,
        grid=(b, nt),
        in_specs=[
            pl.BlockSpec((1, ROW_TILE, d), lambda bi, ti: (bi, ti, 0)),
            pl.BlockSpec((1, d), lambda bi, ti: (0, 0)),
            pl.BlockSpec((1, 1, 1, d), seg),
            pl.BlockSpec((1, 1, 1, d), seg),
            pl.BlockSpec(wq.shape, lambda bi, ti: (0, 0)),
            pl.BlockSpec(kbig.shape, lambda bi, ti: (0, 0, 0)),
        ],
        out_specs=[pl.BlockSpec((1, d, ROW_TILE), lambda bi, ti: (bi * nt + ti, 0, 0)),
                   kspec, tspec, kspec, tspec],
        out_shape=[jax.ShapeDtypeStruct((m // ROW_TILE, d, ROW_TILE), BF16), kshape, tshape, kshape, tshape],
        scratch_shapes=[pltpu.VMEM((2 * PEER_HEADS * PEER_NKEYS, ROW_TILE), F32)],
        compiler_params=_cparams(("parallel", "parallel")),
        name="peer_route",
    )(x, g, shift, scale, wq, kbig)


PEER_TOK = 2 * ROW_TILE
PEER_ECH = 1024


def _expert_body(h_ref, u0_ref, u_ref, vt_ref, a_ref, e1_ref, b_ref, e2_ref, x_ref, g0_ref, g1_ref,
                 fg_ref, o_ref, acc_ref, w0_ref, w1_ref, uf0_ref, uf1_ref, thb_ref, e1b_ref, *,
                 final, n_ch):
    e = pl.program_id(1)
    nk = PEER_NKEYS
    n_il = PEER_ECH // nk
    sub = TILE_ROWS
    ib = 2
    n_ib = n_il // ib
    tn = (((1,), (1,)), ((), ()))
    mq = V7X_MXU_DIM
    uf_refs = (uf0_ref, uf1_ref)
    w_refs = (w0_ref, w1_ref)

    def gate_block(g, cur):
        uf_ref, w_ref = uf_refs[cur], w_refs[cur]
        lt = g // n_ib
        i0 = (g % n_ib) * ib
        gates = [None] * ib
        for hd in range(PEER_HEADS):
            bj = b_ref[hd, lt].reshape(nk // sub, sub, V7X_LANES)
            e2j = e2_ref[hd, lt].reshape(nk // sub, sub, V7X_LANES)
            for k in range(ib):
                th = thb_ref[lt, hd * n_il + i0 + k][None]
                e1 = e1b_ref[lt, hd * n_il + i0 + k][None]
                t = jnp.where(bj >= th, e1 * e2j, 0.0)
                gates[k] = t if gates[k] is None else gates[k] + t
        lpm = mq // V7X_LANES
        wl = pl.ds(_mult((lt % lpm) * V7X_LANES, V7X_LANES), V7X_LANES)
        for k in range(ib):
            rows = pl.ds(_mult((i0 + k) * nk, nk), nk)
            act = _gelu(uf_ref[lt, rows, :]) * gates[k].reshape(nk, V7X_LANES)
            w_ref[lt // lpm, rows, wl] = act.astype(BF16)

    def k_major(ref, r0):
        return jnp.concatenate([ref[k, pl.ds(r0, mq), :] for k in range(ref.shape[0])], axis=1)

    def next_u(it, oth):
        r0 = _mult((it // 2) * mq, mq)
        res = jnp.dot(k_major(u_ref, r0), h_ref[it % 2], preferred_element_type=F32)
        for k in range(mq // V7X_LANES):
            uf_refs[oth][(it % 2) * (mq // V7X_LANES) + k, pl.ds(r0, mq), :] = (
                res[:, k * V7X_LANES:(k + 1) * V7X_LANES])

    def prev_out(it, oth):
        r0 = _mult((it // 2) * mq, mq)
        acc_ref[it % 2, pl.ds(r0, mq), :] += jnp.dot(
            k_major(vt_ref, r0), w_refs[oth][it % 2], preferred_element_type=F32)

    n_it = (PEER_ECH // mq) * (PEER_TOK // mq)
    per_it = (PEER_TOK // V7X_LANES) * n_ib // n_it

    def run_chunk(with_prev, cur):
        for lt in range(PEER_TOK // V7X_LANES):
            tl = slice(lt * V7X_LANES, (lt + 1) * V7X_LANES)
            for hd in range(PEER_HEADS):
                for il in range(n_il):
                    thb_ref[lt, hd * n_il + il] = jnp.broadcast_to(a_ref[hd, il:il + 1, tl], (sub, V7X_LANES))
                    e1b_ref[lt, hd * n_il + il] = jnp.broadcast_to(e1_ref[hd, il:il + 1, tl], (sub, V7X_LANES))

        def step(it, carry):
            for k in range(per_it):
                gate_block(it * per_it + k, cur)
            next_u(it, 1 - cur)
            if with_prev:
                prev_out(it, 1 - cur)
            return carry

        for it in range(n_it):
            step(it, 0)

    @pl.when(e == 0)
    def _():
        acc_ref[...] = jnp.zeros_like(acc_ref)
        u_all = jnp.concatenate([u0_ref[k] for k in range(u0_ref.shape[0])], axis=1)
        lpm = mq // V7X_LANES
        for c in range(PEER_TOK // mq):
            u_first = jnp.dot(u_all, h_ref[c], preferred_element_type=F32)
            for k in range(lpm):
                uf0_ref[c * lpm + k] = u_first[:, k * V7X_LANES:(k + 1) * V7X_LANES]
        run_chunk(False, 0)

    for par in range(2):
        @pl.when(jnp.logical_and(jnp.logical_and(e > 0, e < n_ch), e % 2 == par))
        def _():
            run_chunk(True, par)

    @pl.when(e == n_ch)
    def _():
        vt_all = jnp.concatenate([vt_ref[k] for k in range(vt_ref.shape[0])], axis=1)
        for half, gref in enumerate((g0_ref, g1_ref)):
            y = acc_ref[half] + jnp.dot(vt_all, w_refs[(n_ch - 1) % 2][half], preferred_element_type=F32)
            sl = slice(half * ROW_TILE, (half + 1) * ROW_TILE)
            xn = x_ref[sl, :] + gref[0, 0] * y.T
            if final:
                xn = xn * lax.rsqrt(jnp.mean(xn * xn, axis=-1, keepdims=True) + EPS) * fg_ref[...]
            o_ref[sl, :] = xn


def peer_experts(h, u_tab, vt_tab, a, e1, bm, e2, x, gate, fg, *, final, t0=0):
    b, s, d = x.shape
    nt = s // ROW_TILE
    m = b * s
    n_exp = u_tab.shape[1]
    i_per = PEER_ECH // PEER_NKEYS

    def gidx(half):
        def f(i, e):
            t = 2 * i + half
            return (t // nt, jnp.minimum(t % nt + t0, 1), 0, 0)
        return f

    n_ch = n_exp // PEER_ECH
    last = n_ch - 1
    mq = V7X_MXU_DIM
    n_lt = PEER_TOK // V7X_LANES
    assert mq == ROW_TILE and PEER_TOK == 2 * mq
    out = pl.pallas_call(
        functools.partial(_expert_body, final=final, n_ch=n_ch),
        grid=(m // PEER_TOK, n_ch + 1),
        in_specs=[
            pl.BlockSpec((PEER_TOK // mq, d, mq), lambda i, e: (i, 0, 0)),
            pl.BlockSpec((d // mq, PEER_ECH, mq), lambda i, e: (0, 0, 0)),
            pl.BlockSpec((d // mq, PEER_ECH, mq), lambda i, e: (0, jnp.minimum(e + 1, last), 0)),
            pl.BlockSpec((PEER_ECH // mq, d, mq), lambda i, e: (jnp.maximum(e - 1, 0), 0, 0)),
            pl.BlockSpec((PEER_HEADS, i_per, PEER_TOK), lambda i, e: (0, jnp.minimum(e, last), i)),
            pl.BlockSpec((PEER_HEADS, i_per, PEER_TOK), lambda i, e: (0, jnp.minimum(e, last), i)),
            pl.BlockSpec((PEER_HEADS, PEER_TOK // V7X_LANES, PEER_NKEYS, V7X_LANES), lambda i, e: (0, i, 0, 0)),
            pl.BlockSpec((PEER_HEADS, PEER_TOK // V7X_LANES, PEER_NKEYS, V7X_LANES), lambda i, e: (0, i, 0, 0)),
            pl.BlockSpec((PEER_TOK, d), lambda i, e: (i, 0)),
            pl.BlockSpec((1, 1, 1, d), gidx(0)),
            pl.BlockSpec((1, 1, 1, d), gidx(1)),
            pl.BlockSpec((1, d), lambda i, e: (0, 0)),
        ],
        out_specs=pl.BlockSpec((PEER_TOK, d), lambda i, e: (i, 0)),
        out_shape=jax.ShapeDtypeStruct((m, d), F32),
        scratch_shapes=[pltpu.VMEM((PEER_TOK // mq, d, mq), F32),
                        pltpu.VMEM((PEER_TOK // mq, PEER_ECH, mq), BF16),
                        pltpu.VMEM((PEER_TOK // mq, PEER_ECH, mq), BF16),
                        pltpu.VMEM((n_lt, PEER_ECH, V7X_LANES), F32),
                        pltpu.VMEM((n_lt, PEER_ECH, V7X_LANES), F32),
                        pltpu.VMEM((n_lt, PEER_HEADS * i_per, TILE_ROWS, V7X_LANES), F32),
                        pltpu.VMEM((n_lt, PEER_HEADS * i_per, TILE_ROWS, V7X_LANES), F32)],
        compiler_params=_cparams(("parallel", "arbitrary")),
        name="peer_experts",
    )(h, u_tab, u_tab, vt_tab, a, e1, bm, e2, x.reshape(m, d), gate, gate, fg)
    return out.reshape(b, s, d)


def _axial_tables(rows, rope_dim, ctx_len, lane0):
    n_freq = rope_dim // 4
    half = rope_dim // 2
    inv_freq = ROPE_THETA ** (-jnp.arange(n_freq, dtype=F32) / n_freq)
    r = jnp.repeat(jnp.arange(rows, dtype=F32), GRID_W)
    col = jnp.tile(jnp.arange(GRID_W, dtype=F32), rows)
    ang = jnp.concatenate([r[:, None] * inv_freq, col[:, None] * inv_freq], axis=-1)
    cos, sin = jnp.cos(ang), jnp.sin(ang)
    n = cos.shape[0]
    period = 64 if rope_dim == 64 else V7X_LANES
    c_blk = jnp.ones((n, period), F32).at[:, lane0:lane0 + rope_dim].set(jnp.concatenate([cos, cos], -1))
    s1_blk = jnp.zeros((n, period), F32).at[:, lane0:lane0 + half].set(-sin)
    s2_blk = jnp.zeros((n, period), F32).at[:, lane0 + half:lane0 + rope_dim].set(sin)
    reps = V7X_LANES // period
    out = []
    for blk, fill in ((c_blk, 1.0), (s1_blk, 0.0), (s2_blk, 0.0)):
        t = jnp.tile(blk, (1, reps))
        out.append(jnp.concatenate([jnp.full((ctx_len, V7X_LANES), fill, F32), t], axis=0))
    return out


def _block_diag(w):
    n, a, b = w.shape
    eye = jnp.eye(n, dtype=w.dtype)
    return (eye[:, None, :, None] * w[:, :, None, :]).reshape(n * a, n * b)


def kernel(x, c, ctx, c_ctx, w_mod, b_mod, norm1_g, norm2_g, w_in, conv_w, conv_b, lru_wa, lru_ba,
           lru_wi, lru_bi, lru_lambda, diff_lam, diff_subln_g, gqa_qnorm_g, gqa_knorm_g, mla_qnorm_g,
           mla_w_uq, mla_kvnorm_g, mla_w_ukv, w_branch, w_out, peer_wq, peer_keys, peer_u, peer_v,
           final_norm_g):
    bsz, seq, d = x.shape
    ctx_len = ctx.shape[1]
    depth = w_in.shape[0]
    assert ctx_len == ROW_TILE and seq % PEER_TOK == 0 and seq % GRID_W == 0
    rows = seq // GRID_W
    s_all = ctx_len + seq
    xs = jnp.concatenate([ctx, x], axis=1)

    tabs = _axial_tables(rows, DIFF_DK, ctx_len, 0) + _axial_tables(rows, MLA_ROPE, ctx_len, MLA_NOPE)
    grp = jnp.arange(BRANCH_W) // GQA_DH
    avg = (grp[:, None] == grp[None, :]).astype(BF16) * (1.0 / GQA_DH)
    sc_in = jnp.zeros((16, d), F32).at[:bsz].set(jax.nn.silu(c)).at[bsz].set(jax.nn.silu(c_ctx)).astype(BF16)
    zero1 = jnp.zeros((1,), F32)
    ones_g = jnp.ones((1, V7X_LANES), F32)

    for l in range(depth):
        lam_init = 0.8 - 0.6 * math.exp(-0.3 * l)
        mod_all = matmul(sc_in, w_mod[l].astype(BF16)) + b_mod[l]
        mod_b = mod_all[:bsz].reshape(bsz, N_MOD, d)
        mod_c = jnp.broadcast_to(mod_all[bsz].reshape(1, N_MOD, d), (bsz, N_MOD, d))
        mods = [jnp.stack([mod_c[:, k], mod_b[:, k]], axis=1)[:, :, None, :] for k in range(N_MOD)]

        wl = w_in[l]
        o = 0
        parts = []
        for w in (512, 512, 512, 512, 512, 512, 128, 128, 384, 256, 32, 4096):
            parts.append(wl[:, o:o + w])
            o += w
        xa, ya, qb, kb, vb, qg, kg, vg, cq, ckv, kr, zg = parts
        kr_blk = jnp.zeros((d, V7X_LANES), F32).at[:, MLA_NOPE:MLA_NOPE + MLA_ROPE].set(kr)
        w_cat = jnp.concatenate([zg, xa, ya, qb, kb, vb, qg, kg, vg, cq, ckv, kr_blk], axis=1).astype(BF16)

        h = adaln(xs, norm1_g[l][None], mods[0], mods[1], 0)
        z = matmul(h.reshape(bsz * s_all, d), w_cat, out_dtype=BF16, tm=1024, tn=1024)
        z = z.reshape(bsz, s_all, w_cat.shape[1])

        negc = (-LRU_C * jax.nn.softplus(-lru_lambda[l]))[:, None, None, :]
        wg = jnp.stack([jnp.concatenate([_block_diag(lru_wa[l, dd]), _block_diag(lru_wi[l, dd])], axis=1)
                        for dd in range(2)]).astype(BF16)[:, None]
        bg = jnp.stack([jnp.concatenate([lru_ba[l, dd], lru_bi[l, dd]]) for dd in range(2)])[:, None, None, :]
        cb = conv_b[l][None]
        hf = lru_scan(z, 8, 9, conv_w[l], cb, wg[0], bg[0], negc[0], None, reverse=False)
        oa = lru_scan(z, 8, 9, conv_w[l], cb, wg[1], bg[1], negc[1], hf, reverse=True)

        uq = mla_w_uq[l].reshape(MLA_Q_RANK, MLA_HEADS, MLA_NOPE + MLA_ROPE)
        wuq = jnp.pad(uq, ((0, 0), (0, 0), (0, V7X_LANES - MLA_NOPE - MLA_ROPE))).reshape(MLA_Q_RANK, -1)
        ukv = mla_w_ukv[l].reshape(MLA_KV_RANK, MLA_HEADS, MLA_NOPE + MLA_DV)
        wuk = jnp.pad(ukv[:, :, :MLA_NOPE], ((0, 0), (0, 0), (0, V7X_LANES - MLA_NOPE))).reshape(MLA_KV_RANK, -1)
        wuv = ukv[:, :, MLA_NOPE:].reshape(MLA_KV_RANK, -1)
        qd, kd, vd, qq, kq, vq, qm, km, vm = attn_prep(
            z, tabs, avg, jnp.tile(gqa_qnorm_g[l], GQA_HEADS)[None], jnp.tile(gqa_knorm_g[l], GQA_KV_HEADS)[None],
            mla_qnorm_g[l][None], mla_kvnorm_g[l][None], wuq.astype(BF16), wuk.astype(BF16), wuv.astype(BF16))
        lv = diff_lam[l]
        lam = (jnp.exp(jnp.sum(lv[0] * lv[1])) - jnp.exp(jnp.sum(lv[2] * lv[3])) + lam_init).reshape(1)
        t0 = 1 if l == depth - 1 else 0
        ob = attention(qd, kd, vd, lam, diff_subln_g[l][None], mode="diff", out_scale=1.0 - lam_init, t0=t0)
        og = attention(qq, kq, vq, zero1, ones_g, mode="pair", t0=t0)
        om = attention(qm, km, vm, zero1, ones_g, mode="mla", t0=t0)

        xs = merge(oa, ob, og, om, z, w_branch[l].astype(BF16), w_out[l].astype(BF16), xs, mods[2], t0=t0)

        kbig = jnp.stack([_block_diag(peer_keys[l, hd]) for hd in range(PEER_HEADS)]).astype(BF16)
        h2, pa, pb, pe1, pe2 = peer_route(xs, norm2_g[l][None], mods[3], mods[4], peer_wq[l].astype(BF16), kbig,
                                          t0=t0)
        n_exp = peer_u.shape[1]
        u_km = peer_u[l].astype(BF16).reshape(n_exp, d // V7X_MXU_DIM, V7X_MXU_DIM).transpose(1, 0, 2)
        vt_km = peer_v[l].astype(BF16).reshape(n_exp // V7X_MXU_DIM, V7X_MXU_DIM, d).transpose(0, 2, 1)
        xs = peer_experts(h2, u_km, vt_km, pa, pe1, pb, pe2, xs, mods[5],
                          final_norm_g[None], final=(l == depth - 1), t0=t0)
    return xs
```

```python
import functools
import math

import jax
import jax.numpy as jnp
from jax import lax
from jax.experimental import pallas as pl
from jax.experimental.pallas import tpu as pltpu

F32 = jnp.float32
BF16 = jnp.bfloat16

EPS = 1e-6
GRID_W = 64
ROPE_THETA = 10000.0
N_MOD = 6

D_RNN = 512
RNN_BLOCKS = 8
CONV_W = 4
LRU_C = 8.0

DIFF_HEADS = 4
DIFF_DK = 64
DIFF_DV = 128
GQA_HEADS = 8
GQA_KV_HEADS = 2
GQA_DH = 64
MLA_HEADS = 8
MLA_NOPE = 64
MLA_ROPE = 32
MLA_DV = 64
MLA_Q_RANK = 384
MLA_KV_RANK = 256
N_BRANCH = 4
BRANCH_W = 512

PEER_HEADS = 8
PEER_NKEYS = 128
PEER_DK = 128
PEER_TOPK = 16

V7X_LANES = 128
V7X_SUBLANES = 8
V7X_VMEM_BYTES = 64 * 1024 * 1024
V7X_MXU_DIM = 256
VMEM_LIMIT = 48 * 1024 * 1024

ROW_TILE = 256


def _cparams(sem):
    return pltpu.CompilerParams(dimension_semantics=sem, vmem_limit_bytes=VMEM_LIMIT)


def _mm_body(a_ref, w_ref, o_ref):
    o_ref[...] = jnp.dot(a_ref[...], w_ref[...], preferred_element_type=F32).astype(o_ref.dtype)


def matmul(a, w, out_dtype=F32, tm=512, tn=512):
    m, k = a.shape
    _, n = w.shape
    tm = math.gcd(tm, m)
    tn = math.gcd(tn, n)
    return pl.pallas_call(
        _mm_body,
        grid=(n // tn, m // tm),
        in_specs=[
            pl.BlockSpec((tm, k), lambda j, i: (i, 0)),
            pl.BlockSpec((k, tn), lambda j, i: (0, j)),
        ],
        out_specs=pl.BlockSpec((tm, tn), lambda j, i: (i, j)),
        out_shape=jax.ShapeDtypeStruct((m, n), out_dtype),
        compiler_params=_cparams(("parallel", "parallel")),
        name="matmul",
    )(a, w)


def _adaln_mm_body(x_ref, g_ref, sh_ref, sc_ref, w_ref, o_ref, h_ref, *, ctx_len):
    tm = x_ref.shape[1]

    @pl.when(pl.program_id(2) == 0)
    def _():
        x = x_ref[0]
        y = x * lax.rsqrt(jnp.mean(x * x, axis=-1, keepdims=True) + EPS) * g_ref[...]
        row = pl.program_id(1) * tm + lax.broadcasted_iota(jnp.int32, (tm, 1), 0)
        is_ctx = row < ctx_len
        scale = jnp.where(is_ctx, sc_ref[0, 0], sc_ref[0, 1])
        shift = jnp.where(is_ctx, sh_ref[0, 0], sh_ref[0, 1])
        h_ref[...] = (y * (1.0 + scale) + shift).astype(BF16)

    o_ref[0] = jnp.dot(h_ref[...], w_ref[...], preferred_element_type=F32).astype(o_ref.dtype)


def adaln_matmul(x, g, shift, scale, w, ctx_len, n_row_tiles=4, tn=1024):
    b, s, d = x.shape
    n = w.shape[1]
    tm = s // n_row_tiles
    assert tm * n_row_tiles == s and tm % (2 * V7X_SUBLANES) == 0 and n % tn == 0
    mod = pl.BlockSpec((1, 2, 1, d), lambda bi, ti, j: (bi, 0, 0, 0))
    return pl.pallas_call(
        functools.partial(_adaln_mm_body, ctx_len=ctx_len),
        grid=(b, n_row_tiles, n // tn),
        in_specs=[
            pl.BlockSpec((1, tm, d), lambda bi, ti, j: (bi, ti, 0)),
            pl.BlockSpec((1, d), lambda bi, ti, j: (0, 0)),
            mod, mod,
            pl.BlockSpec((d, tn), lambda bi, ti, j: (0, j)),
        ],
        out_specs=pl.BlockSpec((1, tm, tn), lambda bi, ti, j: (bi, ti, j)),
        out_shape=jax.ShapeDtypeStruct((b, s, n), BF16),
        scratch_shapes=[pltpu.VMEM((tm, d), BF16)],
        compiler_params=_cparams(("parallel", "parallel", "arbitrary")),
        name="adaln_matmul",
    )(x, g, shift, scale, w)


def _mult(x, m):
    return x if isinstance(x, int) else pl.multiple_of(x, m)


def _gelu(x):
    c = math.sqrt(2.0 / math.pi)
    return x * (0.5 * (1.0 + jnp.tanh(c * (x + 0.044715 * (x * x * x)))))


def _sigmoid(x):
    return 1.0 / (1.0 + jnp.exp(-x))


TILE_ROWS = V7X_SUBLANES
HALO_ROWS = 2 * V7X_SUBLANES


def _lru_body(x_ref, xp_ref, xn_ref, cw_ref, cb_ref, wg_ref, bg_ref, nc_ref, *rest,
              reverse, n_chunks):
    if reverse:
        ya_ref, hf_ref, o_ref, carry_ref = rest
    else:
        o_ref, carry_ref = rest
    step = pl.program_id(1)
    if reverse:
        chunk = jnp.where(step == 0, 0, n_chunks - step)
    else:
        chunk = step
    rows = ROW_TILE
    c = x_ref.shape[-1]

    @pl.when(step == 0)
    def _():
        carry_ref[...] = jnp.zeros_like(carry_ref)

    x = x_ref[0].astype(F32)
    xx = jnp.concatenate([xp_ref[0].astype(F32), x, xn_ref[0].astype(F32)], axis=0)
    n_xx = rows + 2 * HALO_ROWS
    row = lax.broadcasted_iota(jnp.int32, (rows, c), 0)
    seg_start = jnp.logical_or(chunk == 0, chunk == 1)
    seg_end = jnp.logical_or(chunk == 0, chunk == n_chunks - 1)

    def shifted(d):
        return pltpu.roll(xx, (n_xx - d) % n_xx, axis=0)[HALO_ROWS:HALO_ROWS + rows]

    x_m1 = jnp.where(jnp.logical_and(seg_start, row == 0), 0.0, shifted(-1))
    x_p1 = jnp.where(jnp.logical_and(seg_end, row >= rows - 1), 0.0, shifted(1))
    x_p2 = jnp.where(jnp.logical_and(seg_end, row >= rows - 2), 0.0, shifted(2))
    cw = cw_ref[...]
    u = cw[0:1] * x_m1 + cw[1:2] * x + cw[2:3] * x_p1 + cw[3:4] * x_p2 + cb_ref[...]

    z = jnp.dot(u.astype(BF16), wg_ref[0], preferred_element_type=F32) + bg_ref[0]
    r = _sigmoid(z[:, :c])
    gi = _sigmoid(z[:, c:])
    a = jnp.exp(nc_ref[0] * r)
    bv = jnp.sqrt(1.0 - a * a) * (gi * u)

    sub = row % TILE_ROWS
    for dstep in (1, 2, 4):
        if reverse:
            sh = (rows - dstep) % rows
            keep = sub <= TILE_ROWS - 1 - dstep
        else:
            sh = dstep
            keep = sub >= dstep
        a_sh = jnp.where(keep, pltpu.roll(a, sh, axis=0), 1.0)
        b_sh = jnp.where(keep, pltpu.roll(bv, sh, axis=0), 0.0)
        bv = a * b_sh + bv
        a = a * a_sh

    carry = carry_ref[...]
    n_tiles = rows // TILE_ROWS
    order = range(n_tiles - 1, -1, -1) if reverse else range(n_tiles)
    hs = [None] * n_tiles
    for t in order:
        sl = slice(t * TILE_ROWS, (t + 1) * TILE_ROWS)
        h = bv[sl] + a[sl] * carry
        hs[t] = h
        edge = h[0:1] if reverse else h[TILE_ROWS - 1:TILE_ROWS]
        carry = jnp.broadcast_to(edge, (TILE_ROWS, c))
    carry_ref[...] = carry
    h_all = jnp.concatenate(hs, axis=0)
    if reverse:
        o_ref[0] = (_gelu(ya_ref[0].astype(F32)) * (hf_ref[0] + h_all)).astype(o_ref.dtype)
    else:
        o_ref[0] = h_all


def lru_scan(z, xa_col, ya_col, conv_w, conv_b, wg, bg, negc, hf, *, reverse):
    b, s, _ = z.shape
    c = D_RNN
    n_chunks = s // ROW_TILE
    per = ROW_TILE // HALO_ROWS
    n8 = s // HALO_ROWS
    if reverse:
        cidx = lambda st: jnp.where(st == 0, 0, n_chunks - st)
    else:
        cidx = lambda st: st
    in_specs = [
        pl.BlockSpec((1, ROW_TILE, c), lambda bi, st: (bi, cidx(st), xa_col)),
        pl.BlockSpec((1, HALO_ROWS, c), lambda bi, st: (bi, jnp.maximum(cidx(st) * per - 1, 0), xa_col)),
        pl.BlockSpec((1, HALO_ROWS, c), lambda bi, st: (bi, jnp.minimum((cidx(st) + 1) * per, n8 - 1), xa_col)),
        pl.BlockSpec((CONV_W, c), lambda bi, st: (0, 0)),
        pl.BlockSpec((1, c), lambda bi, st: (0, 0)),
        pl.BlockSpec((1, c, 2 * c), lambda bi, st: (0, 0, 0)),
        pl.BlockSpec((1, 1, 2 * c), lambda bi, st: (0, 0, 0)),
        pl.BlockSpec((1, 1, c), lambda bi, st: (0, 0, 0)),
    ]
    args = [z, z, z, conv_w, conv_b, wg, bg, negc]
    if reverse:
        in_specs += [
            pl.BlockSpec((1, ROW_TILE, c), lambda bi, st: (bi, cidx(st), ya_col)),
            pl.BlockSpec((1, ROW_TILE, c), lambda bi, st: (bi, cidx(st), 0)),
        ]
        args += [z, hf]
    return pl.pallas_call(
        functools.partial(_lru_body, reverse=reverse, n_chunks=n_chunks),
        grid=(b, n_chunks),
        in_specs=in_specs,
        out_specs=pl.BlockSpec((1, ROW_TILE, c), lambda bi, st: (bi, cidx(st), 0)),
        out_shape=jax.ShapeDtypeStruct((b, s, c), BF16 if reverse else F32),
        scratch_shapes=[pltpu.VMEM((TILE_ROWS, c), F32)],
        compiler_params=_cparams(("parallel", "arbitrary")),
        name="lru_rev" if reverse else "lru_fwd",
    )(*args)


def _tile_lanes(t, n):
    return jnp.concatenate([t] * n, axis=1) if n > 1 else t


def _rope(x, c, s1, s2, r):
    w = x.shape[-1]
    n = w // V7X_LANES
    return (x * _tile_lanes(c, n) + pltpu.roll(x, w - r, axis=1) * _tile_lanes(s1, n)
            + pltpu.roll(x, r, axis=1) * _tile_lanes(s2, n))


def _split_dot(x, m_ref):
    hi = x.astype(BF16)
    lo = (x - hi.astype(F32)).astype(BF16)
    m = m_ref[...]
    return jnp.dot(hi, m, preferred_element_type=F32) + jnp.dot(lo, m, preferred_element_type=F32)


def _with_ones(v):
    ones = jnp.ones((v.shape[0], V7X_LANES), v.dtype)
    parts = []
    for c in range(v.shape[1] // V7X_LANES):
        parts += [v[:, c * V7X_LANES:(c + 1) * V7X_LANES], ones]
    return jnp.concatenate(parts, axis=1)


def _rms(x, g):
    return x * lax.rsqrt(jnp.mean(x * x, axis=-1, keepdims=True) + EPS) * g


def _prep_body(qb_ref, kb_ref, vb_ref, qg_ref, tail_ref,
               c64_ref, s164_ref, s264_ref, cm_ref, s1m_ref, s2m_ref,
               avg_ref, gq_ref, gk_ref, gcq_ref, gckv_ref, wuq_ref, wuk_ref, wuv_ref,
               qd_ref, kd_ref, vd_ref, qq_ref, kq_ref, vq_ref, qm_ref, km_ref, vm_ref):
    c64, s164, s264 = c64_ref[...], s164_ref[...], s264_ref[...]
    cm, s1m, s2m = cm_ref[...], s1m_ref[...], s2m_ref[...]
    half64 = DIFF_DK // 2
    qd_ref[0] = (_rope(qb_ref[0].astype(F32), c64, s164, s264, half64) * (DIFF_DK ** -0.5)).astype(BF16)
    kd_ref[0] = _rope(kb_ref[0].astype(F32), c64, s164, s264, half64).astype(BF16)
    vd_ref[0] = _with_ones(vb_ref[0].astype(BF16))
    qg = qg_ref[0].astype(F32)
    ms = _split_dot(qg * qg, avg_ref)
    qn = qg * lax.rsqrt(ms + EPS) * gq_ref[...]
    qq_ref[0] = (_rope(qn, c64, s164, s264, GQA_DH // 2) * (GQA_DH ** -0.5)).astype(BF16)
    tail = tail_ref[0].astype(F32)
    kg = tail[:, 0:128]
    msk = _split_dot(kg * kg, avg_ref.at[0:128, 0:128])
    kn = _rope(kg * lax.rsqrt(msk + EPS) * gk_ref[...], c64, s164, s264, GQA_DH // 2).astype(BF16)
    vg = tail[:, 128:256].astype(BF16)
    kq_ref[0] = jnp.concatenate([kn[:, 0:64], kn[:, 0:64], kn[:, 64:128], kn[:, 64:128]], axis=1)
    vq_ref[0] = _with_ones(jnp.concatenate([vg[:, 0:64], vg[:, 0:64], vg[:, 64:128], vg[:, 64:128]], axis=1))
    cq = tail[:, 256:256 + MLA_Q_RANK]
    ckv = tail[:, 640:640 + MLA_KV_RANK]
    kr = tail[:, 896:1024]
    qf = jnp.dot(_rms(cq, gcq_ref[...]).astype(BF16), wuq_ref[...], preferred_element_type=F32)
    scale = (MLA_NOPE + MLA_ROPE) ** -0.5
    qm_ref[0] = (_rope(qf, cm, s1m, s2m, MLA_ROPE // 2) * scale).astype(BF16)
    ckvn = _rms(ckv, gckv_ref[...]).astype(BF16)
    kf = jnp.dot(ckvn, wuk_ref[...], preferred_element_type=F32)
    krr = _rope(kr, cm, s1m, s2m, MLA_ROPE // 2)
    km_ref[0] = (kf + _tile_lanes(krr, MLA_HEADS)).astype(BF16)
    vm_ref[0] = _with_ones(jnp.dot(ckvn, wuv_ref[...], preferred_element_type=F32).astype(BF16))


def attn_prep(z, tabs, avg, gq, gk, gcq, gckv, wuq, wuk, wuv):
    b, s, _ = z.shape
    nt = s // ROW_TILE
    zspec = lambda w, idx: pl.BlockSpec((1, ROW_TILE, w), lambda ti, bi: (bi, ti, idx))
    tab = pl.BlockSpec((ROW_TILE, V7X_LANES), lambda ti, bi: (ti, 0))
    full = lambda a: pl.BlockSpec(a.shape, lambda ti, bi: (0,) * a.ndim)
    ospec = lambda w: pl.BlockSpec((1, ROW_TILE, w), lambda ti, bi: (bi, ti, 0))
    oshape = lambda w: jax.ShapeDtypeStruct((b, s, w), BF16)
    widths = (512, 512, 1024, 512, 256, 512, 1024, 1024, 1024)
    return pl.pallas_call(
        _prep_body,
        grid=(nt, b),
        in_specs=[zspec(512, 10), zspec(512, 11), zspec(512, 12), zspec(512, 13), zspec(1024, 7)]
        + [tab] * 6 + [full(a) for a in (avg, gq, gk, gcq, gckv, wuq, wuk, wuv)],
        out_specs=[ospec(w) for w in widths],
        out_shape=[oshape(w) for w in widths],
        compiler_params=_cparams(("parallel", "parallel")),
        name="attn_prep",
    )(z, z, z, z, z, *tabs, avg, gq, gk, gcq, gckv, wuq, wuk, wuv)


ATTN_GROUPS = {"pair": 2, "diff": 2, "mla": 2}


def _attn_body(lam_ref, q_ref, k_ref, v_ref, g_ref, o_ref, *, mode, out_scale, t0, groups):
    ti = pl.program_id(2)
    lane = lax.broadcasted_iota(jnp.int32, (ROW_TILE, V7X_LANES), 1)
    lo_half = lane < (V7X_LANES // 2)

    ln = V7X_LANES

    def run(n_keys):
        for grp in range(groups):
            outs = []
            for j in range(2):
                if mode == "mla":
                    c0 = (2 * grp + j) * ln
                    qj = q_ref[0, :, c0:c0 + ln]
                    kj = k_ref[0, 0:n_keys, c0:c0 + ln]
                else:
                    q = q_ref[0, :, grp * ln:(grp + 1) * ln]
                    keep = lo_half if j == 0 else jnp.logical_not(lo_half)
                    qj = jnp.where(keep, q.astype(F32), 0.0).astype(BF16)
                    kc = (grp // 2 if mode == "pair" else grp) * ln
                    kj = k_ref[0, 0:n_keys, kc:kc + ln]
                vc = (grp // 2 if mode == "pair" else grp) * 2 * ln
                hk = (n_keys // (2 * ln)) * ln
                tn = (((1,), (1,)), ((), ()))
                s = jnp.concatenate(
                    [lax.dot_general(qj, kj[0:hk], tn, preferred_element_type=F32).astype(BF16),
                     lax.dot_general(qj, kj[hk:], tn, preferred_element_type=F32).astype(BF16)], axis=1)
                m = jnp.max(s, axis=-1, keepdims=True)
                p = jnp.exp(s - m)
                o = (jnp.dot(p[:, 0:hk], v_ref[0, 0:hk, vc:vc + 2 * ln], preferred_element_type=F32)
                     + jnp.dot(p[:, hk:], v_ref[0, hk:n_keys, vc:vc + 2 * ln], preferred_element_type=F32))
                outs.append(o[:, 0:ln] / o[:, ln:2 * ln])
            if mode == "diff":
                o = outs[0] - lam_ref[0] * outs[1]
                o = o * lax.rsqrt(jnp.mean(o * o, axis=-1, keepdims=True) + EPS) * g_ref[...] * out_scale
            else:
                o = jnp.where(lo_half, outs[0], outs[1])
            o_ref[0, :, grp * ln:(grp + 1) * ln] = o.astype(o_ref.dtype)

    if t0 == 0:
        @pl.when(ti == 0)
        def _():
            run(ROW_TILE)

        @pl.when(ti > 0)
        def _():
            run(k_ref.shape[1])
    else:
        run(k_ref.shape[1])


def attention(q, k, v, lam, g, *, mode, out_scale=1.0, t0=0):
    b, s, wq = q.shape
    groups = ATTN_GROUPS[mode]
    steps = 4 // groups
    qw = wq // steps
    kw = k.shape[-1] // steps
    vw = v.shape[-1] // steps
    ow = groups * V7X_LANES
    nt = s // ROW_TILE - t0
    return pl.pallas_call(
        functools.partial(_attn_body, mode=mode, out_scale=out_scale, t0=t0, groups=groups),
        grid=(b, steps, nt),
        in_specs=[
            pl.BlockSpec(memory_space=pltpu.SMEM),
            pl.BlockSpec((1, ROW_TILE, qw), lambda bi, gi, ti: (bi, ti + t0, gi)),
            pl.BlockSpec((1, s, kw), lambda bi, gi, ti: (bi, 0, gi)),
            pl.BlockSpec((1, s, vw), lambda bi, gi, ti: (bi, 0, gi)),
            pl.BlockSpec((1, V7X_LANES), lambda bi, gi, ti: (0, 0)),
        ],
        out_specs=pl.BlockSpec((1, ROW_TILE, ow), lambda bi, gi, ti: (bi, ti, gi)),
        out_shape=jax.ShapeDtypeStruct((b, nt * ROW_TILE, 4 * V7X_LANES), BF16),
        compiler_params=_cparams(("parallel", "parallel", "arbitrary")),
        name="attn_" + mode,
    )(lam, q, k, v, g)


def _merge_body(oa_ref, ob_ref, og_ref, om_ref, zg_ref, wb_ref, wo_ref, x_ref, gate_ref, o_ref):
    d = x_ref.shape[-1]
    merged = None
    for k, o_k in enumerate((oa_ref, ob_ref, og_ref, om_ref)):
        t = jnp.dot(o_k[0], wb_ref[k], preferred_element_type=F32)
        t = t * _sigmoid(zg_ref[0, :, k * d:(k + 1) * d].astype(F32))
        merged = t if merged is None else merged + t
    y = jnp.dot(merged.astype(BF16), wo_ref[...], preferred_element_type=F32)
    o_ref[0] = x_ref[0] + gate_ref[0, 0] * y


def merge(oa, ob, og, om, z, wb, wo, x, gate, t0=0):
    b, s, d = x.shape
    nt = s // ROW_TILE - t0
    full = lambda w: pl.BlockSpec((1, ROW_TILE, w), lambda bi, ti: (bi, ti + t0, 0))
    part = pl.BlockSpec((1, ROW_TILE, BRANCH_W), lambda bi, ti: (bi, ti, 0))
    return pl.pallas_call(
        _merge_body,
        grid=(b, nt),
        in_specs=[full(BRANCH_W), part, part, part,
                  full(N_BRANCH * d),
                  pl.BlockSpec(wb.shape, lambda bi, ti: (0, 0, 0)),
                  pl.BlockSpec(wo.shape, lambda bi, ti: (0, 0)),
                  full(d),
                  pl.BlockSpec((1, 1, 1, d), lambda bi, ti: (bi, jnp.minimum(ti + t0, 1), 0, 0))],
        out_specs=pl.BlockSpec((1, ROW_TILE, d), lambda bi, ti: (bi, ti, 0)),
        out_shape=jax.ShapeDtypeStruct((b, nt * ROW_TILE, d), F32),
        compiler_params=_cparams(("parallel", "parallel")),
        name="merge",
    )(oa, ob, og, om, z, wb, wo, x, gate)


NEG_BIG = float(jnp.finfo(jnp.float32).min)


def _oddeven_merge(lo, hi, r):
    step = r * 2
    if step < hi - lo:
        yield from _oddeven_merge(lo, hi, step)
        yield from _oddeven_merge(lo + r, hi, step)
        yield from [(i, i + r) for i in range(lo + r, hi - r, step)]
    else:
        yield (lo, lo + r)


def _oddeven_sort(lo, hi):
    if hi - lo >= 1:
        mid = lo + (hi - lo) // 2
        yield from _oddeven_sort(lo, mid)
        yield from _oddeven_sort(mid + 1, hi)
        yield from _oddeven_merge(lo, hi, 1)


def _bitonic_merge(n):
    pairs, s = [], n // 2
    while s >= 1:
        pairs += [(i, i + s) for i in range(n) if (i // s) % 2 == 0]
        s //= 2
    return pairs


_SORT16 = tuple(_oddeven_sort(0, PEER_TOPK - 1))
_BITONIC16 = tuple(_bitonic_merge(PEER_TOPK))


def _route_body(x_ref, g_ref, sh_ref, sc_ref, wq_ref, kb_ref,
                h_ref, a_ref, b_ref, e1_ref, e2_ref, st_ref):
    x = x_ref[0]
    ms = jnp.mean(x * x, axis=-1, keepdims=True)
    hf = x * lax.rsqrt(ms + EPS) * g_ref[...] * (1.0 + sc_ref[0, 0]) + sh_ref[0, 0]
    h_ref[0] = hf.T.astype(BF16)
    h = hf.astype(BF16)
    q = jnp.dot(h, wq_ref[...], preferred_element_type=F32).astype(BF16)
    nk = PEER_NKEYS
    for hd in range(PEER_HEADS):
        st_ref[hd * 2 * nk:(hd + 1) * 2 * nk, :] = lax.dot_general(
            kb_ref[hd], q[:, hd * PEER_DK:(hd + 1) * PEER_DK], (((1,), (1,)), ((), ())),
            preferred_element_type=F32)

    def exchange(t, pairs):
        for i, j in pairs:
            t[i], t[j] = jnp.maximum(t[i], t[j]), jnp.minimum(t[i], t[j])

    def merge_top(a, b):
        t = [jnp.maximum(a[k], b[PEER_TOPK - 1 - k]) for k in range(PEER_TOPK)]
        exchange(t, _BITONIC16)
        return t

    def across_sublanes(t):
        for sh in (4, 2, 1):
            t = merge_top(t, [pltpu.roll(x, sh, axis=0) for x in t])
        return t

    def top16(s):
        t = [s[k * TILE_ROWS:(k + 1) * TILE_ROWS] for k in range(nk // TILE_ROWS)]
        exchange(t, _SORT16)
        return across_sublanes(t)

    row8 = lax.broadcasted_iota(jnp.int32, (TILE_ROWS, V7X_LANES), 0)
    n_q = jnp.zeros((TILE_ROWS, V7X_LANES), jnp.int32)
    for p in range(TILE_ROWS):
        n_q = jnp.where(row8 == p, PEER_TOPK // (p + 1), n_q)

    def rank_rows(t, lo):
        out = t[lo]
        for r in range(1, TILE_ROWS):
            out = jnp.where(row8 == r, t[lo + r], out)
        return out

    def head(it, carry):
        hd = it // (ROW_TILE // V7X_LANES)
        lt = it % (ROW_TILE // V7X_LANES)
        base = pl.multiple_of(hd * (2 * nk), 2 * nk)
        tl = pl.ds(pl.multiple_of(lt * V7X_LANES, V7X_LANES), V7X_LANES)
        s1 = st_ref[pl.ds(base, nk), tl]
        s2 = st_ref[pl.ds(base + nk, nk), tl]
        ta = top16(s1)
        tb = top16(s2)
        lo = rank_rows(ta, 0)
        cand = [jnp.where(n_q > q, lo + tb[q], NEG_BIG) for q in range(PEER_TOPK)]
        cand[PEER_TOPK - 1] = jnp.maximum(cand[PEER_TOPK - 1], rank_rows(ta, TILE_ROWS) + tb[0])
        exchange(cand, _BITONIC16)
        cand = across_sublanes(cand)
        top = cand[0][0:1]
        tau = cand[PEER_TOPK - 1][0:1]
        zsum = jnp.zeros_like(top)
        for k in range(PEER_TOPK):
            zsum = zsum + jnp.exp(cand[k][0:1] - top)
        a_ref[hd, :, tl] = tau - s1
        e1_ref[hd, :, tl] = jnp.exp(s1 - ta[0][0:1]) / zsum
        b_ref[hd, lt] = s2
        e2_ref[hd, lt] = jnp.exp(s2 - tb[0][0:1])
        return carry

    lax.fori_loop(0, PEER_HEADS * (ROW_TILE // V7X_LANES), head, 0)


def peer_route(x, g, shift, scale, wq, kbig, t0=0):
    b, s, d = x.shape
    nt = s // ROW_TILE
    m = b * s
    seg = lambda bi, ti: (bi, jnp.minimum(ti + t0, 1), 0, 0)
    flat = lambda bi, ti: (0, 0, bi * nt + ti)
    kspec = pl.BlockSpec((PEER_HEADS, PEER_NKEYS, ROW_TILE), flat)
    kshape = jax.ShapeDtypeStruct((PEER_HEADS, PEER_NKEYS, m), F32)
    lt_per = ROW_TILE // V7X_LANES
    tspec = pl.BlockSpec((PEER_HEADS, lt_per, PEER_NKEYS, V7X_LANES), lambda bi, ti: (0, bi * nt + ti, 0, 0))
    tshape = jax.ShapeDtypeStruct((PEER_HEADS, m // V7X_LANES, PEER_NKEYS, V7X_LANES), F32)
    return pl.pallas_call(
        _route_body,
        grid=(b, nt),
        in_specs=[
            pl.BlockSpec((1, ROW_TILE, d), lambda bi, ti: (bi, ti, 0)),
            pl.BlockSpec((1, d), lambda bi, ti: (0, 0)),
            pl.BlockSpec((1, 1, 1, d), seg),
            pl.BlockSpec((1, 1, 1, d), seg),
            pl.BlockSpec(wq.shape, lambda bi, ti: (0, 0)),
            pl.BlockSpec(kbig.shape, lambda bi, ti: (0, 0, 0)),
        ],
        out_specs=[pl.BlockSpec((1, d, ROW_TILE), lambda bi, ti: (bi * nt + ti, 0, 0)),
                   kspec, tspec, kspec, tspec],
        out_shape=[jax.ShapeDtypeStruct((m // ROW_TILE, d, ROW_TILE), BF16), kshape, tshape, kshape, tshape],
        scratch_shapes=[pltpu.VMEM((2 * PEER_HEADS * PEER_NKEYS, ROW_TILE), F32)],
        compiler_params=_cparams(("parallel", "parallel")),
        name="peer_route",
    )(x, g, shift, scale, wq, kbig)


PEER_TOK = 2 * ROW_TILE
PEER_ECH = 1024


def _expert_body(h_ref, u0_ref, u_ref, vt_ref, a_ref, e1_ref, b_ref, e2_ref, x_ref, g0_ref, g1_ref,
                 fg_ref, o_ref, acc_ref, w0_ref, w1_ref, uf0_ref, uf1_ref, thb_ref, e1b_ref, *,
                 final, n_ch):
    e = pl.program_id(1)
    nk = PEER_NKEYS
    n_il = PEER_ECH // nk
    sub = TILE_ROWS
    ib, jb = 4, nk // 2
    n_ib, n_jb = n_il // ib, nk // jb
    tn = (((1,), (1,)), ((), ()))
    mq = V7X_MXU_DIM
    uf_refs = (uf0_ref, uf1_ref)
    w_refs = (w0_ref, w1_ref)

    def gate_block(g, cur):
        uf_ref, w_ref = uf_refs[cur], w_refs[cur]
        lt = g // (n_ib * n_jb)
        i0 = ((g // n_jb) % n_ib) * ib
        j0 = (g % n_jb) * jb
        gates = [None] * ib
        for hd in range(PEER_HEADS):
            bj = b_ref[hd, lt, j0:j0 + jb, :].reshape(jb // sub, sub, V7X_LANES)
            e2j = e2_ref[hd, lt, j0:j0 + jb, :].reshape(jb // sub, sub, V7X_LANES)
            for k in range(ib):
                th = thb_ref[lt, hd * n_il + i0 + k][None]
                e1 = e1b_ref[lt, hd * n_il + i0 + k][None]
                t = jnp.where(bj >= th, e1 * e2j, 0.0)
                gates[k] = t if gates[k] is None else gates[k] + t
        lpm = mq // V7X_LANES
        wl = pl.ds(_mult((lt % lpm) * V7X_LANES, V7X_LANES), V7X_LANES)
        for k in range(ib):
            rows = pl.ds((i0 + k) * nk + j0, jb)
            act = _gelu(uf_ref[lt, rows, :]) * gates[k].reshape(jb, V7X_LANES)
            w_ref[lt // lpm, rows, wl] = act.astype(BF16)

    def k_major(ref, r0):
        return jnp.concatenate([ref[k, pl.ds(r0, mq), :] for k in range(ref.shape[0])], axis=1)

    def next_u(it, oth):
        r0 = _mult((it // 2) * mq, mq)
        res = jnp.dot(k_major(u_ref, r0), h_ref[it % 2], preferred_element_type=F32)
        for k in range(mq // V7X_LANES):
            uf_refs[oth][(it % 2) * (mq // V7X_LANES) + k, pl.ds(r0, mq), :] = (
                res[:, k * V7X_LANES:(k + 1) * V7X_LANES])

    def prev_out(it, oth):
        r0 = _mult((it // 2) * mq, mq)
        acc_ref[it % 2, pl.ds(r0, mq), :] += jnp.dot(
            k_major(vt_ref, r0), w_refs[oth][it % 2], preferred_element_type=F32)

    n_it = (PEER_ECH // mq) * (PEER_TOK // mq)
    per_it = (PEER_TOK // V7X_LANES) * n_ib * n_jb // n_it

    def run_chunk(with_prev, cur):
        for lt in range(PEER_TOK // V7X_LANES):
            tl = slice(lt * V7X_LANES, (lt + 1) * V7X_LANES)
            for hd in range(PEER_HEADS):
                for il in range(n_il):
                    thb_ref[lt, hd * n_il + il] = jnp.broadcast_to(a_ref[hd, il:il + 1, tl], (sub, V7X_LANES))
                    e1b_ref[lt, hd * n_il + il] = jnp.broadcast_to(e1_ref[hd, il:il + 1, tl], (sub, V7X_LANES))

        def step(it, carry):
            for k in range(per_it):
                gate_block(it * per_it + k, cur)
            next_u(it, 1 - cur)
            if with_prev:
                prev_out(it, 1 - cur)
            return carry

        for it in range(n_it):
            step(it, 0)

    @pl.when(e == 0)
    def _():
        acc_ref[...] = jnp.zeros_like(acc_ref)
        u_all = jnp.concatenate([u0_ref[k] for k in range(u0_ref.shape[0])], axis=1)
        lpm = mq // V7X_LANES
        for c in range(PEER_TOK // mq):
            u_first = jnp.dot(u_all, h_ref[c], preferred_element_type=F32)
            for k in range(lpm):
                uf0_ref[c * lpm + k] = u_first[:, k * V7X_LANES:(k + 1) * V7X_LANES]
        run_chunk(False, 0)

    for par in range(2):
        @pl.when(jnp.logical_and(jnp.logical_and(e > 0, e < n_ch), e % 2 == par))
        def _():
            run_chunk(True, par)

    @pl.when(e == n_ch)
    def _():
        vt_all = jnp.concatenate([vt_ref[k] for k in range(vt_ref.shape[0])], axis=1)
        for half, gref in enumerate((g0_ref, g1_ref)):
            y = acc_ref[half] + jnp.dot(vt_all, w_refs[(n_ch - 1) % 2][half], preferred_element_type=F32)
            sl = slice(half * ROW_TILE, (half + 1) * ROW_TILE)
            xn = x_ref[sl, :] + gref[0, 0] * y.T
            if final:
                xn = xn * lax.rsqrt(jnp.mean(xn * xn, axis=-1, keepdims=True) + EPS) * fg_ref[...]
            o_ref[sl, :] = xn


def peer_experts(h, u_tab, vt_tab, a, e1, bm, e2, x, gate, fg, *, final, t0=0):
    b, s, d = x.shape
    nt = s // ROW_TILE
    m = b * s
    n_exp = u_tab.shape[1]
    i_per = PEER_ECH // PEER_NKEYS

    def gidx(half):
        def f(i, e):
            t = 2 * i + half
            return (t // nt, jnp.minimum(t % nt + t0, 1), 0, 0)
        return f

    n_ch = n_exp // PEER_ECH
    last = n_ch - 1
    mq = V7X_MXU_DIM
    n_lt = PEER_TOK // V7X_LANES
    assert mq == ROW_TILE and PEER_TOK == 2 * mq
    out = pl.pallas_call(
        functools.partial(_expert_body, final=final, n_ch=n_ch),
        grid=(m // PEER_TOK, n_ch + 1),
        in_specs=[
            pl.BlockSpec((PEER_TOK // mq, d, mq), lambda i, e: (i, 0, 0)),
            pl.BlockSpec((d // mq, PEER_ECH, mq), lambda i, e: (0, 0, 0)),
            pl.BlockSpec((d // mq, PEER_ECH, mq), lambda i, e: (0, jnp.minimum(e + 1, last), 0)),
            pl.BlockSpec((PEER_ECH // mq, d, mq), lambda i, e: (jnp.maximum(e - 1, 0), 0, 0)),
            pl.BlockSpec((PEER_HEADS, i_per, PEER_TOK), lambda i, e: (0, jnp.minimum(e, last), i)),
            pl.BlockSpec((PEER_HEADS, i_per, PEER_TOK), lambda i, e: (0, jnp.minimum(e, last), i)),
            pl.BlockSpec((PEER_HEADS, PEER_TOK // V7X_LANES, PEER_NKEYS, V7X_LANES), lambda i, e: (0, i, 0, 0)),
            pl.BlockSpec((PEER_HEADS, PEER_TOK // V7X_LANES, PEER_NKEYS, V7X_LANES), lambda i, e: (0, i, 0, 0)),
            pl.BlockSpec((PEER_TOK, d), lambda i, e: (i, 0)),
            pl.BlockSpec((1, 1, 1, d), gidx(0)),
            pl.BlockSpec((1, 1, 1, d), gidx(1)),
            pl.BlockSpec((1, d), lambda i, e: (0, 0)),
        ],
        out_specs=pl.BlockSpec((PEER_TOK, d), lambda i, e: (i, 0)),
        out_shape=jax.ShapeDtypeStruct((m, d), F32),
        scratch_shapes=[pltpu.VMEM((PEER_TOK // mq, d, mq), F32),
                        pltpu.VMEM((PEER_TOK // mq, PEER_ECH, mq), BF16),
                        pltpu.VMEM((PEER_TOK // mq, PEER_ECH, mq), BF16),
                        pltpu.VMEM((n_lt, PEER_ECH, V7X_LANES), F32),
                        pltpu.VMEM((n_lt, PEER_ECH, V7X_LANES), F32),
                        pltpu.VMEM((n_lt, PEER_HEADS * i_per, TILE_ROWS, V7X_LANES), F32),
                        pltpu.VMEM((n_lt, PEER_HEADS * i_per, TILE_ROWS, V7X_LANES), F32)],
        compiler_params=_cparams(("parallel", "arbitrary")),
        name="peer_experts",
    )(h, u_tab, u_tab, vt_tab, a, e1, bm, e2, x.reshape(m, d), gate, gate, fg)
    return out.reshape(b, s, d)


def _axial_tables(rows, rope_dim, ctx_len, lane0):
    n_freq = rope_dim // 4
    half = rope_dim // 2
    inv_freq = ROPE_THETA ** (-jnp.arange(n_freq, dtype=F32) / n_freq)
    r = jnp.repeat(jnp.arange(rows, dtype=F32), GRID_W)
    col = jnp.tile(jnp.arange(GRID_W, dtype=F32), rows)
    ang = jnp.concatenate([r[:, None] * inv_freq, col[:, None] * inv_freq], axis=-1)
    cos, sin = jnp.cos(ang), jnp.sin(ang)
    n = cos.shape[0]
    period = 64 if rope_dim == 64 else V7X_LANES
    c_blk = jnp.ones((n, period), F32).at[:, lane0:lane0 + rope_dim].set(jnp.concatenate([cos, cos], -1))
    s1_blk = jnp.zeros((n, period), F32).at[:, lane0:lane0 + half].set(-sin)
    s2_blk = jnp.zeros((n, period), F32).at[:, lane0 + half:lane0 + rope_dim].set(sin)
    reps = V7X_LANES // period
    out = []
    for blk, fill in ((c_blk, 1.0), (s1_blk, 0.0), (s2_blk, 0.0)):
        t = jnp.tile(blk, (1, reps))
        out.append(jnp.concatenate([jnp.full((ctx_len, V7X_LANES), fill, F32), t], axis=0))
    return out


def _block_diag(w):
    n, a, b = w.shape
    eye = jnp.eye(n, dtype=w.dtype)
    return (eye[:, None, :, None] * w[:, :, None, :]).reshape(n * a, n * b)


def kernel(x, c, ctx, c_ctx, w_mod, b_mod, norm1_g, norm2_g, w_in, conv_w, conv_b, lru_wa, lru_ba,
           lru_wi, lru_bi, lru_lambda, diff_lam, diff_subln_g, gqa_qnorm_g, gqa_knorm_g, mla_qnorm_g,
           mla_w_uq, mla_kvnorm_g, mla_w_ukv, w_branch, w_out, peer_wq, peer_keys, peer_u, peer_v,
           final_norm_g):
    bsz, seq, d = x.shape
    ctx_len = ctx.shape[1]
    depth = w_in.shape[0]
    assert ctx_len == ROW_TILE and seq % PEER_TOK == 0 and seq % GRID_W == 0
    rows = seq // GRID_W
    s_all = ctx_len + seq
    xs = jnp.concatenate([ctx, x], axis=1)

    tabs = _axial_tables(rows, DIFF_DK, ctx_len, 0) + _axial_tables(rows, MLA_ROPE, ctx_len, MLA_NOPE)
    grp = jnp.arange(BRANCH_W) // GQA_DH
    avg = (grp[:, None] == grp[None, :]).astype(BF16) * (1.0 / GQA_DH)
    sc_in = jnp.zeros((16, d), F32).at[:bsz].set(jax.nn.silu(c)).at[bsz].set(jax.nn.silu(c_ctx)).astype(BF16)
    zero1 = jnp.zeros((1,), F32)
    ones_g = jnp.ones((1, V7X_LANES), F32)

    for l in range(depth):
        lam_init = 0.8 - 0.6 * math.exp(-0.3 * l)
        mod_all = matmul(sc_in, w_mod[l].astype(BF16)) + b_mod[l]
        mod_b = mod_all[:bsz].reshape(bsz, N_MOD, d)
        mod_c = jnp.broadcast_to(mod_all[bsz].reshape(1, N_MOD, d), (bsz, N_MOD, d))
        mods = [jnp.stack([mod_c[:, k], mod_b[:, k]], axis=1)[:, :, None, :] for k in range(N_MOD)]

        wl = w_in[l]
        o = 0
        parts = []
        for w in (512, 512, 512, 512, 512, 512, 128, 128, 384, 256, 32, 4096):
            parts.append(wl[:, o:o + w])
            o += w
        xa, ya, qb, kb, vb, qg, kg, vg, cq, ckv, kr, zg = parts
        kr_blk = jnp.zeros((d, V7X_LANES), F32).at[:, MLA_NOPE:MLA_NOPE + MLA_ROPE].set(kr)
        w_cat = jnp.concatenate([zg, xa, ya, qb, kb, vb, qg, kg, vg, cq, ckv, kr_blk], axis=1).astype(BF16)

        z = adaln_matmul(xs, norm1_g[l][None], mods[0], mods[1], w_cat, ctx_len)

        negc = (-LRU_C * jax.nn.softplus(-lru_lambda[l]))[:, None, None, :]
        wg = jnp.stack([jnp.concatenate([_block_diag(lru_wa[l, dd]), _block_diag(lru_wi[l, dd])], axis=1)
                        for dd in range(2)]).astype(BF16)[:, None]
        bg = jnp.stack([jnp.concatenate([lru_ba[l, dd], lru_bi[l, dd]]) for dd in range(2)])[:, None, None, :]
        cb = conv_b[l][None]
        hf = lru_scan(z, 8, 9, conv_w[l], cb, wg[0], bg[0], negc[0], None, reverse=False)
        oa = lru_scan(z, 8, 9, conv_w[l], cb, wg[1], bg[1], negc[1], hf, reverse=True)

        uq = mla_w_uq[l].reshape(MLA_Q_RANK, MLA_HEADS, MLA_NOPE + MLA_ROPE)
        wuq = jnp.pad(uq, ((0, 0), (0, 0), (0, V7X_LANES - MLA_NOPE - MLA_ROPE))).reshape(MLA_Q_RANK, -1)
        ukv = mla_w_ukv[l].reshape(MLA_KV_RANK, MLA_HEADS, MLA_NOPE + MLA_DV)
        wuk = jnp.pad(ukv[:, :, :MLA_NOPE], ((0, 0), (0, 0), (0, V7X_LANES - MLA_NOPE))).reshape(MLA_KV_RANK, -1)
        wuv = ukv[:, :, MLA_NOPE:].reshape(MLA_KV_RANK, -1)
        qd, kd, vd, qq, kq, vq, qm, km, vm = attn_prep(
            z, tabs, avg, jnp.tile(gqa_qnorm_g[l], GQA_HEADS)[None], jnp.tile(gqa_knorm_g[l], GQA_KV_HEADS)[None],
            mla_qnorm_g[l][None], mla_kvnorm_g[l][None], wuq.astype(BF16), wuk.astype(BF16), wuv.astype(BF16))
        lv = diff_lam[l]
        lam = (jnp.exp(jnp.sum(lv[0] * lv[1])) - jnp.exp(jnp.sum(lv[2] * lv[3])) + lam_init).reshape(1)
        t0 = 1 if l == depth - 1 else 0
        ob = attention(qd, kd, vd, lam, diff_subln_g[l][None], mode="diff", out_scale=1.0 - lam_init, t0=t0)
        og = attention(qq, kq, vq, zero1, ones_g, mode="pair", t0=t0)
        om = attention(qm, km, vm, zero1, ones_g, mode="mla", t0=t0)

        xs = merge(oa, ob, og, om, z, w_branch[l].astype(BF16), w_out[l].astype(BF16), xs, mods[2], t0=t0)

        kbig = jnp.stack([_block_diag(peer_keys[l, hd]) for hd in range(PEER_HEADS)]).astype(BF16)
        h2, pa, pb, pe1, pe2 = peer_route(xs, norm2_g[l][None], mods[3], mods[4], peer_wq[l].astype(BF16), kbig,
                                          t0=t0)
        n_exp = peer_u.shape[1]
        u_km = peer_u[l].astype(BF16).reshape(n_exp, d // V7X_MXU_DIM, V7X_MXU_DIM).transpose(1, 0, 2)
        vt_km = peer_v[l].astype(BF16).reshape(n_exp // V7X_MXU_DIM, V7X_MXU_DIM, d).transpose(0, 2, 1)
        xs = peer_experts(h2, u_km, vt_km, pa, pe1, pb, pe2, xs, mods[5],
                          final_norm_g[None], final=(l == depth - 1), t0=t0)
    return xs
```

```python
import functools
import math

import jax
import jax.numpy as jnp
from jax import lax
from jax.experimental import pallas as pl
from jax.experimental.pallas import tpu as pltpu

F32 = jnp.float32
BF16 = jnp.bfloat16

EPS = 1e-6
GRID_W = 64
ROPE_THETA = 10000.0
N_MOD = 6

D_RNN = 512
RNN_BLOCKS = 8
CONV_W = 4
LRU_C = 8.0

DIFF_HEADS = 4
DIFF_DK = 64
DIFF_DV = 128
GQA_HEADS = 8
GQA_KV_HEADS = 2
GQA_DH = 64
MLA_HEADS = 8
MLA_NOPE = 64
MLA_ROPE = 32
MLA_DV = 64
MLA_Q_RANK = 384
MLA_KV_RANK = 256
N_BRANCH = 4
BRANCH_W = 512

PEER_HEADS = 8
PEER_NKEYS = 128
PEER_DK = 128
PEER_TOPK = 16

V7X_LANES = 128
V7X_SUBLANES = 8
V7X_VMEM_BYTES = 64 * 1024 * 1024
V7X_MXU_DIM = 256
VMEM_LIMIT = 48 * 1024 * 1024

ROW_TILE = 256


def _cparams(sem):
    return pltpu.CompilerParams(dimension_semantics=sem, vmem_limit_bytes=VMEM_LIMIT)


def _mm_body(a_ref, w_ref, o_ref):
    o_ref[...] = jnp.dot(a_ref[...], w_ref[...], preferred_element_type=F32).astype(o_ref.dtype)


def matmul(a, w, out_dtype=F32, tm=512, tn=512):
    m, k = a.shape
    _, n = w.shape
    tm = math.gcd(tm, m)
    tn = math.gcd(tn, n)
    return pl.pallas_call(
        _mm_body,
        grid=(n // tn, m // tm),
        in_specs=[
            pl.BlockSpec((tm, k), lambda j, i: (i, 0)),
            pl.BlockSpec((k, tn), lambda j, i: (0, j)),
        ],
        out_specs=pl.BlockSpec((tm, tn), lambda j, i: (i, j)),
        out_shape=jax.ShapeDtypeStruct((m, n), out_dtype),
        compiler_params=_cparams(("parallel", "parallel")),
        name="matmul",
    )(a, w)


def _adaln_mm_body(x_ref, g_ref, sh_ref, sc_ref, w_ref, o_ref, h_ref, *, ctx_len):
    tm = x_ref.shape[1]

    @pl.when(pl.program_id(2) == 0)
    def _():
        x = x_ref[0]
        y = x * lax.rsqrt(jnp.mean(x * x, axis=-1, keepdims=True) + EPS) * g_ref[...]
        row = pl.program_id(1) * tm + lax.broadcasted_iota(jnp.int32, (tm, 1), 0)
        is_ctx = row < ctx_len
        scale = jnp.where(is_ctx, sc_ref[0, 0], sc_ref[0, 1])
        shift = jnp.where(is_ctx, sh_ref[0, 0], sh_ref[0, 1])
        h_ref[...] = (y * (1.0 + scale) + shift).astype(BF16)

    o_ref[0] = jnp.dot(h_ref[...], w_ref[...], preferred_element_type=F32).astype(o_ref.dtype)


def adaln_matmul(x, g, shift, scale, w, ctx_len, n_row_tiles=4, tn=1024):
    b, s, d = x.shape
    n = w.shape[1]
    tm = s // n_row_tiles
    assert tm * n_row_tiles == s and tm % (2 * V7X_SUBLANES) == 0 and n % tn == 0
    mod = pl.BlockSpec((1, 2, 1, d), lambda bi, ti, j: (bi, 0, 0, 0))
    return pl.pallas_call(
        functools.partial(_adaln_mm_body, ctx_len=ctx_len),
        grid=(b, n_row_tiles, n // tn),
        in_specs=[
            pl.BlockSpec((1, tm, d), lambda bi, ti, j: (bi, ti, 0)),
            pl.BlockSpec((1, d), lambda bi, ti, j: (0, 0)),
            mod, mod,
            pl.BlockSpec((d, tn), lambda bi, ti, j: (0, j)),
        ],
        out_specs=pl.BlockSpec((1, tm, tn), lambda bi, ti, j: (bi, ti, j)),
        out_shape=jax.ShapeDtypeStruct((b, s, n), BF16),
        scratch_shapes=[pltpu.VMEM((tm, d), BF16)],
        compiler_params=_cparams(("parallel", "parallel", "arbitrary")),
        name="adaln_matmul",
    )(x, g, shift, scale, w)


def _mult(x, m):
    return x if isinstance(x, int) else pl.multiple_of(x, m)


def _gelu2(x):
    c = math.sqrt(2.0 / math.pi)
    return x * (1.0 + jnp.tanh(x * (c + (0.044715 * c) * (x * x))))


def _gelu(x):
    return 0.5 * _gelu2(x)


def _sigmoid(x):
    return 1.0 / (1.0 + jnp.exp(-x))


TILE_ROWS = V7X_SUBLANES
HALO_ROWS = 2 * V7X_SUBLANES


def _lru_body(x_ref, xp_ref, xn_ref, cw_ref, cb_ref, wg_ref, bg_ref, nc_ref, *rest,
              reverse, n_chunks):
    if reverse:
        ya_ref, hf_ref, o_ref, carry_ref = rest
    else:
        o_ref, carry_ref = rest
    step = pl.program_id(1)
    if reverse:
        chunk = jnp.where(step == 0, 0, n_chunks - step)
    else:
        chunk = step
    rows = ROW_TILE
    c = x_ref.shape[-1]

    @pl.when(step == 0)
    def _():
        carry_ref[...] = jnp.zeros_like(carry_ref)

    x = x_ref[0].astype(F32)
    xx = jnp.concatenate([xp_ref[0].astype(F32), x, xn_ref[0].astype(F32)], axis=0)
    n_xx = rows + 2 * HALO_ROWS
    row = lax.broadcasted_iota(jnp.int32, (rows, c), 0)
    seg_start = jnp.logical_or(chunk == 0, chunk == 1)
    seg_end = jnp.logical_or(chunk == 0, chunk == n_chunks - 1)

    def shifted(d):
        return pltpu.roll(xx, (n_xx - d) % n_xx, axis=0)[HALO_ROWS:HALO_ROWS + rows]

    x_m1 = jnp.where(jnp.logical_and(seg_start, row == 0), 0.0, shifted(-1))
    x_p1 = jnp.where(jnp.logical_and(seg_end, row >= rows - 1), 0.0, shifted(1))
    x_p2 = jnp.where(jnp.logical_and(seg_end, row >= rows - 2), 0.0, shifted(2))
    cw = cw_ref[...]
    u = cw[0:1] * x_m1 + cw[1:2] * x + cw[2:3] * x_p1 + cw[3:4] * x_p2 + cb_ref[...]

    z = jnp.dot(u.astype(BF16), wg_ref[0], preferred_element_type=F32) + bg_ref[0]
    r = _sigmoid(z[:, :c])
    gi = _sigmoid(z[:, c:])
    a = jnp.exp(nc_ref[0] * r)
    bv = jnp.sqrt(1.0 - a * a) * (gi * u)

    sub = row % TILE_ROWS
    for dstep in (1, 2, 4):
        if reverse:
            sh = (rows - dstep) % rows
            keep = sub <= TILE_ROWS - 1 - dstep
        else:
            sh = dstep
            keep = sub >= dstep
        a_sh = jnp.where(keep, pltpu.roll(a, sh, axis=0), 1.0)
        b_sh = jnp.where(keep, pltpu.roll(bv, sh, axis=0), 0.0)
        bv = a * b_sh + bv
        a = a * a_sh

    carry = carry_ref[...]
    n_tiles = rows // TILE_ROWS
    order = range(n_tiles - 1, -1, -1) if reverse else range(n_tiles)
    hs = [None] * n_tiles
    for t in order:
        sl = slice(t * TILE_ROWS, (t + 1) * TILE_ROWS)
        h = bv[sl] + a[sl] * carry
        hs[t] = h
        edge = h[0:1] if reverse else h[TILE_ROWS - 1:TILE_ROWS]
        carry = jnp.broadcast_to(edge, (TILE_ROWS, c))
    carry_ref[...] = carry
    h_all = jnp.concatenate(hs, axis=0)
    if reverse:
        o_ref[0] = (_gelu(ya_ref[0].astype(F32)) * (hf_ref[0] + h_all)).astype(o_ref.dtype)
    else:
        o_ref[0] = h_all


def lru_scan(z, xa_col, ya_col, conv_w, conv_b, wg, bg, negc, hf, *, reverse):
    b, s, _ = z.shape
    c = D_RNN
    n_chunks = s // ROW_TILE
    per = ROW_TILE // HALO_ROWS
    n8 = s // HALO_ROWS
    if reverse:
        cidx = lambda st: jnp.where(st == 0, 0, n_chunks - st)
    else:
        cidx = lambda st: st
    in_specs = [
        pl.BlockSpec((1, ROW_TILE, c), lambda bi, st: (bi, cidx(st), xa_col)),
        pl.BlockSpec((1, HALO_ROWS, c), lambda bi, st: (bi, jnp.maximum(cidx(st) * per - 1, 0), xa_col)),
        pl.BlockSpec((1, HALO_ROWS, c), lambda bi, st: (bi, jnp.minimum((cidx(st) + 1) * per, n8 - 1), xa_col)),
        pl.BlockSpec((CONV_W, c), lambda bi, st: (0, 0)),
        pl.BlockSpec((1, c), lambda bi, st: (0, 0)),
        pl.BlockSpec((1, c, 2 * c), lambda bi, st: (0, 0, 0)),
        pl.BlockSpec((1, 1, 2 * c), lambda bi, st: (0, 0, 0)),
        pl.BlockSpec((1, 1, c), lambda bi, st: (0, 0, 0)),
    ]
    args = [z, z, z, conv_w, conv_b, wg, bg, negc]
    if reverse:
        in_specs += [
            pl.BlockSpec((1, ROW_TILE, c), lambda bi, st: (bi, cidx(st), ya_col)),
            pl.BlockSpec((1, ROW_TILE, c), lambda bi, st: (bi, cidx(st), 0)),
        ]
        args += [z, hf]
    return pl.pallas_call(
        functools.partial(_lru_body, reverse=reverse, n_chunks=n_chunks),
        grid=(b, n_chunks),
        in_specs=in_specs,
        out_specs=pl.BlockSpec((1, ROW_TILE, c), lambda bi, st: (bi, cidx(st), 0)),
        out_shape=jax.ShapeDtypeStruct((b, s, c), BF16 if reverse else F32),
        scratch_shapes=[pltpu.VMEM((TILE_ROWS, c), F32)],
        compiler_params=_cparams(("parallel", "arbitrary")),
        name="lru_rev" if reverse else "lru_fwd",
    )(*args)


def _tile_lanes(t, n):
    return jnp.concatenate([t] * n, axis=1) if n > 1 else t


def _rope(x, c, s1, s2, r):
    w = x.shape[-1]
    n = w // V7X_LANES
    return (x * _tile_lanes(c, n) + pltpu.roll(x, w - r, axis=1) * _tile_lanes(s1, n)
            + pltpu.roll(x, r, axis=1) * _tile_lanes(s2, n))


def _split_dot(x, m_ref):
    hi = x.astype(BF16)
    lo = (x - hi.astype(F32)).astype(BF16)
    m = m_ref[...]
    return jnp.dot(hi, m, preferred_element_type=F32) + jnp.dot(lo, m, preferred_element_type=F32)


def _with_ones(v):
    ones = jnp.ones((v.shape[0], V7X_LANES), v.dtype)
    parts = []
    for c in range(v.shape[1] // V7X_LANES):
        parts += [v[:, c * V7X_LANES:(c + 1) * V7X_LANES], ones]
    return jnp.concatenate(parts, axis=1)


def _rms(x, g):
    return x * lax.rsqrt(jnp.mean(x * x, axis=-1, keepdims=True) + EPS) * g


def _prep_body(qb_ref, kb_ref, vb_ref, qg_ref, tail_ref,
               c64_ref, s164_ref, s264_ref, cm_ref, s1m_ref, s2m_ref,
               avg_ref, gq_ref, gk_ref, gcq_ref, gckv_ref, wuq_ref, wuk_ref, wuv_ref,
               qd_ref, kd_ref, vd_ref, qq_ref, kq_ref, vq_ref, qm_ref, km_ref, vm_ref):
    c64, s164, s264 = c64_ref[...], s164_ref[...], s264_ref[...]
    cm, s1m, s2m = cm_ref[...], s1m_ref[...], s2m_ref[...]
    half64 = DIFF_DK // 2
    qd_ref[0] = (_rope(qb_ref[0].astype(F32), c64, s164, s264, half64) * (DIFF_DK ** -0.5)).astype(BF16)
    kd_ref[0] = _rope(kb_ref[0].astype(F32), c64, s164, s264, half64).astype(BF16)
    vd_ref[0] = _with_ones(vb_ref[0].astype(BF16))
    qg = qg_ref[0].astype(F32)
    ms = _split_dot(qg * qg, avg_ref)
    qn = qg * lax.rsqrt(ms + EPS) * gq_ref[...]
    qq_ref[0] = (_rope(qn, c64, s164, s264, GQA_DH // 2) * (GQA_DH ** -0.5)).astype(BF16)
    tail = tail_ref[0].astype(F32)
    kg = tail[:, 0:128]
    msk = _split_dot(kg * kg, avg_ref.at[0:128, 0:128])
    kn = _rope(kg * lax.rsqrt(msk + EPS) * gk_ref[...], c64, s164, s264, GQA_DH // 2).astype(BF16)
    vg = tail[:, 128:256].astype(BF16)
    kq_ref[0] = jnp.concatenate([kn[:, 0:64], kn[:, 0:64], kn[:, 64:128], kn[:, 64:128]], axis=1)
    vq_ref[0] = _with_ones(jnp.concatenate([vg[:, 0:64], vg[:, 0:64], vg[:, 64:128], vg[:, 64:128]], axis=1))
    cq = tail[:, 256:256 + MLA_Q_RANK]
    ckv = tail[:, 640:640 + MLA_KV_RANK]
    kr = tail[:, 896:1024]
    qf = jnp.dot(_rms(cq, gcq_ref[...]).astype(BF16), wuq_ref[...], preferred_element_type=F32)
    scale = (MLA_NOPE + MLA_ROPE) ** -0.5
    qm_ref[0] = (_rope(qf, cm, s1m, s2m, MLA_ROPE // 2) * scale).astype(BF16)
    ckvn = _rms(ckv, gckv_ref[...]).astype(BF16)
    kf = jnp.dot(ckvn, wuk_ref[...], preferred_element_type=F32)
    krr = _rope(kr, cm, s1m, s2m, MLA_ROPE // 2)
    km_ref[0] = (kf + _tile_lanes(krr, MLA_HEADS)).astype(BF16)
    vm_ref[0] = _with_ones(jnp.dot(ckvn, wuv_ref[...], preferred_element_type=F32).astype(BF16))


def attn_prep(z, tabs, avg, gq, gk, gcq, gckv, wuq, wuk, wuv):
    b, s, _ = z.shape
    nt = s // ROW_TILE
    zspec = lambda w, idx: pl.BlockSpec((1, ROW_TILE, w), lambda ti, bi: (bi, ti, idx))
    tab = pl.BlockSpec((ROW_TILE, V7X_LANES), lambda ti, bi: (ti, 0))
    full = lambda a: pl.BlockSpec(a.shape, lambda ti, bi: (0,) * a.ndim)
    ospec = lambda w: pl.BlockSpec((1, ROW_TILE, w), lambda ti, bi: (bi, ti, 0))
    oshape = lambda w: jax.ShapeDtypeStruct((b, s, w), BF16)
    widths = (512, 512, 1024, 512, 256, 512, 1024, 1024, 1024)
    return pl.pallas_call(
        _prep_body,
        grid=(nt, b),
        in_specs=[zspec(512, 10), zspec(512, 11), zspec(512, 12), zspec(512, 13), zspec(1024, 7)]
        + [tab] * 6 + [full(a) for a in (avg, gq, gk, gcq, gckv, wuq, wuk, wuv)],
        out_specs=[ospec(w) for w in widths],
        out_shape=[oshape(w) for w in widths],
        compiler_params=_cparams(("parallel", "parallel")),
        name="attn_prep",
    )(z, z, z, z, z, *tabs, avg, gq, gk, gcq, gckv, wuq, wuk, wuv)


ATTN_GROUPS = {"pair": 2, "diff": 2, "mla": 2}


def _attn_body(lam_ref, q_ref, k_ref, v_ref, g_ref, o_ref, *, mode, out_scale, t0, groups):
    ti = pl.program_id(2)
    lane = lax.broadcasted_iota(jnp.int32, (ROW_TILE, V7X_LANES), 1)
    lo_half = lane < (V7X_LANES // 2)

    ln = V7X_LANES

    def run(n_keys):
        for grp in range(groups):
            outs = []
            for j in range(2):
                if mode == "mla":
                    c0 = (2 * grp + j) * ln
                    qj = q_ref[0, :, c0:c0 + ln]
                    kj = k_ref[0, 0:n_keys, c0:c0 + ln]
                else:
                    q = q_ref[0, :, grp * ln:(grp + 1) * ln]
                    keep = lo_half if j == 0 else jnp.logical_not(lo_half)
                    qj = jnp.where(keep, q.astype(F32), 0.0).astype(BF16)
                    kc = (grp // 2 if mode == "pair" else grp) * ln
                    kj = k_ref[0, 0:n_keys, kc:kc + ln]
                vc = (grp // 2 if mode == "pair" else grp) * 2 * ln
                hk = (n_keys // (2 * ln)) * ln
                tn = (((1,), (1,)), ((), ()))
                s = jnp.concatenate(
                    [lax.dot_general(qj, kj[0:hk], tn, preferred_element_type=F32).astype(BF16),
                     lax.dot_general(qj, kj[hk:], tn, preferred_element_type=F32).astype(BF16)], axis=1)
                m = jnp.max(s, axis=-1, keepdims=True)
                p = jnp.exp(s - m)
                o = (jnp.dot(p[:, 0:hk], v_ref[0, 0:hk, vc:vc + 2 * ln], preferred_element_type=F32)
                     + jnp.dot(p[:, hk:], v_ref[0, hk:n_keys, vc:vc + 2 * ln], preferred_element_type=F32))
                outs.append(o[:, 0:ln] / o[:, ln:2 * ln])
            if mode == "diff":
                o = outs[0] - lam_ref[0] * outs[1]
                o = o * lax.rsqrt(jnp.mean(o * o, axis=-1, keepdims=True) + EPS) * g_ref[...] * out_scale
            else:
                o = jnp.where(lo_half, outs[0], outs[1])
            o_ref[0, :, grp * ln:(grp + 1) * ln] = o.astype(o_ref.dtype)

    if t0 == 0:
        @pl.when(ti == 0)
        def _():
            run(ROW_TILE)

        @pl.when(ti > 0)
        def _():
            run(k_ref.shape[1])
    else:
        run(k_ref.shape[1])


def attention(q, k, v, lam, g, *, mode, out_scale=1.0, t0=0):
    b, s, wq = q.shape
    groups = ATTN_GROUPS[mode]
    steps = 4 // groups
    qw = wq // steps
    kw = k.shape[-1] // steps
    vw = v.shape[-1] // steps
    ow = groups * V7X_LANES
    nt = s // ROW_TILE - t0
    return pl.pallas_call(
        functools.partial(_attn_body, mode=mode, out_scale=out_scale, t0=t0, groups=groups),
        grid=(b, steps, nt),
        in_specs=[
            pl.BlockSpec(memory_space=pltpu.SMEM),
            pl.BlockSpec((1, ROW_TILE, qw), lambda bi, gi, ti: (bi, ti + t0, gi)),
            pl.BlockSpec((1, s, kw), lambda bi, gi, ti: (bi, 0, gi)),
            pl.BlockSpec((1, s, vw), lambda bi, gi, ti: (bi, 0, gi)),
            pl.BlockSpec((1, V7X_LANES), lambda bi, gi, ti: (0, 0)),
        ],
        out_specs=pl.BlockSpec((1, ROW_TILE, ow), lambda bi, gi, ti: (bi, ti, gi)),
        out_shape=jax.ShapeDtypeStruct((b, nt * ROW_TILE, 4 * V7X_LANES), BF16),
        compiler_params=_cparams(("parallel", "parallel", "arbitrary")),
        name="attn_" + mode,
    )(lam, q, k, v, g)


def _merge_body(oa_ref, ob_ref, og_ref, om_ref, zg_ref, wb_ref, wo_ref, x_ref, gate_ref, o_ref):
    d = x_ref.shape[-1]
    merged = None
    for k, o_k in enumerate((oa_ref, ob_ref, og_ref, om_ref)):
        t = jnp.dot(o_k[0], wb_ref[k], preferred_element_type=F32)
        t = t * _sigmoid(zg_ref[0, :, k * d:(k + 1) * d].astype(F32))
        merged = t if merged is None else merged + t
    y = jnp.dot(merged.astype(BF16), wo_ref[...], preferred_element_type=F32)
    o_ref[0] = x_ref[0] + gate_ref[0, 0] * y


def merge(oa, ob, og, om, z, wb, wo, x, gate, t0=0):
    b, s, d = x.shape
    nt = s // ROW_TILE - t0
    full = lambda w: pl.BlockSpec((1, ROW_TILE, w), lambda bi, ti: (bi, ti + t0, 0))
    part = pl.BlockSpec((1, ROW_TILE, BRANCH_W), lambda bi, ti: (bi, ti, 0))
    return pl.pallas_call(
        _merge_body,
        grid=(b, nt),
        in_specs=[full(BRANCH_W), part, part, part,
                  full(N_BRANCH * d),
                  pl.BlockSpec(wb.shape, lambda bi, ti: (0, 0, 0)),
                  pl.BlockSpec(wo.shape, lambda bi, ti: (0, 0)),
                  full(d),
                  pl.BlockSpec((1, 1, 1, d), lambda bi, ti: (bi, jnp.minimum(ti + t0, 1), 0, 0))],
        out_specs=pl.BlockSpec((1, ROW_TILE, d), lambda bi, ti: (bi, ti, 0)),
        out_shape=jax.ShapeDtypeStruct((b, nt * ROW_TILE, d), F32),
        compiler_params=_cparams(("parallel", "parallel")),
        name="merge",
    )(oa, ob, og, om, z, wb, wo, x, gate)


NEG_BIG = float(jnp.finfo(jnp.float32).min)


def _oddeven_merge(lo, hi, r):
    step = r * 2
    if step < hi - lo:
        yield from _oddeven_merge(lo, hi, step)
        yield from _oddeven_merge(lo + r, hi, step)
        yield from [(i, i + r) for i in range(lo + r, hi - r, step)]
    else:
        yield (lo, lo + r)


def _oddeven_sort(lo, hi):
    if hi - lo >= 1:
        mid = lo + (hi - lo) // 2
        yield from _oddeven_sort(lo, mid)
        yield from _oddeven_sort(mid + 1, hi)
        yield from _oddeven_merge(lo, hi, 1)


def _bitonic_merge(n):
    pairs, s = [], n // 2
    while s >= 1:
        pairs += [(i, i + s) for i in range(n) if (i // s) % 2 == 0]
        s //= 2
    return pairs


_SORT16 = tuple(_oddeven_sort(0, PEER_TOPK - 1))
_BITONIC16 = tuple(_bitonic_merge(PEER_TOPK))


def _route_body(x_ref, g_ref, sh_ref, sc_ref, wq_ref, kb_ref,
                h_ref, a_ref, b_ref, e1_ref, e2_ref, st_ref):
    x = x_ref[0]
    ms = jnp.mean(x * x, axis=-1, keepdims=True)
    hf = x * lax.rsqrt(ms + EPS) * g_ref[...] * (1.0 + sc_ref[0, 0]) + sh_ref[0, 0]
    h_ref[0] = hf.T.astype(BF16)
    h = hf.astype(BF16)
    q = jnp.dot(h, wq_ref[...], preferred_element_type=F32).astype(BF16)
    nk = PEER_NKEYS
    for hd in range(PEER_HEADS):
        st_ref[hd * 2 * nk:(hd + 1) * 2 * nk, :] = lax.dot_general(
            kb_ref[hd], q[:, hd * PEER_DK:(hd + 1) * PEER_DK], (((1,), (1,)), ((), ())),
            preferred_element_type=F32)

    def exchange(t, pairs):
        for i, j in pairs:
            t[i], t[j] = jnp.maximum(t[i], t[j]), jnp.minimum(t[i], t[j])

    def merge_top(a, b):
        t = [jnp.maximum(a[k], b[PEER_TOPK - 1 - k]) for k in range(PEER_TOPK)]
        exchange(t, _BITONIC16)
        return t

    def across_sublanes(t):
        for sh in (4, 2, 1):
            t = merge_top(t, [pltpu.roll(x, sh, axis=0) for x in t])
        return t

    def top16(s):
        t = [s[k * TILE_ROWS:(k + 1) * TILE_ROWS] for k in range(nk // TILE_ROWS)]
        exchange(t, _SORT16)
        return across_sublanes(t)

    row8 = lax.broadcasted_iota(jnp.int32, (TILE_ROWS, V7X_LANES), 0)
    n_q = jnp.zeros((TILE_ROWS, V7X_LANES), jnp.int32)
    for p in range(TILE_ROWS):
        n_q = jnp.where(row8 == p, PEER_TOPK // (p + 1), n_q)

    def rank_rows(t, lo):
        out = t[lo]
        for r in range(1, TILE_ROWS):
            out = jnp.where(row8 == r, t[lo + r], out)
        return out

    def head(it, carry):
        hd = it // (ROW_TILE // V7X_LANES)
        lt = it % (ROW_TILE // V7X_LANES)
        base = pl.multiple_of(hd * (2 * nk), 2 * nk)
        tl = pl.ds(pl.multiple_of(lt * V7X_LANES, V7X_LANES), V7X_LANES)
        s1 = st_ref[pl.ds(base, nk), tl]
        s2 = st_ref[pl.ds(base + nk, nk), tl]
        ta = top16(s1)
        tb = top16(s2)
        lo = rank_rows(ta, 0)
        cand = [jnp.where(n_q > q, lo + tb[q], NEG_BIG) for q in range(PEER_TOPK)]
        cand[PEER_TOPK - 1] = jnp.maximum(cand[PEER_TOPK - 1], rank_rows(ta, TILE_ROWS) + tb[0])
        exchange(cand, _BITONIC16)
        cand = across_sublanes(cand)
        top = cand[0][0:1]
        tau = cand[PEER_TOPK - 1][0:1]
        zsum = jnp.zeros_like(top)
        for k in range(PEER_TOPK):
            zsum = zsum + jnp.exp(cand[k][0:1] - top)
        a_ref[hd, :, tl] = tau - s1
        e1_ref[hd, :, tl] = jnp.exp(s1 - ta[0][0:1]) * (0.5 / zsum)
        b_ref[hd, lt] = s2
        e2_ref[hd, lt] = jnp.exp(s2 - tb[0][0:1])
        return carry

    lax.fori_loop(0, PEER_HEADS * (ROW_TILE // V7X_LANES), head, 0)


def peer_route(x, g, shift, scale, wq, kbig, t0=0):
    b, s, d = x.shape
    nt = s // ROW_TILE
    m = b * s
    seg = lambda bi, ti: (bi, jnp.minimum(ti + t0, 1), 0, 0)
    flat = lambda bi, ti: (0, 0, bi * nt + ti)
    kspec = pl.BlockSpec((PEER_HEADS, PEER_NKEYS, ROW_TILE), flat)
    kshape = jax.ShapeDtypeStruct((PEER_HEADS, PEER_NKEYS, m), F32)
    lt_per = ROW_TILE // V7X_LANES
    tspec = pl.BlockSpec((PEER_HEADS, lt_per, PEER_NKEYS, V7X_LANES), lambda bi, ti: (0, bi * nt + ti, 0, 0))
    tshape = jax.ShapeDtypeStruct((PEER_HEADS, m // V7X_LANES, PEER_NKEYS, V7X_LANES), F32)
    return pl.pallas_call(
        _route_body,
        grid=(b, nt),
        in_specs=[
            pl.BlockSpec((1, ROW_TILE, d), lambda bi, ti: (bi, ti, 0)),
            pl.BlockSpec((1, d), lambda bi, ti: (0, 0)),
            pl.BlockSpec((1, 1, 1, d), seg),
            pl.BlockSpec((1, 1, 1, d), seg),
            pl.BlockSpec(wq.shape, lambda bi, ti: (0, 0)),
            pl.BlockSpec(kbig.shape, lambda bi, ti: (0, 0, 0)),
        ],
        out_specs=[pl.BlockSpec((1, d, ROW_TILE), lambda bi, ti: (bi * nt + ti, 0, 0)),
                   kspec, tspec, kspec, tspec],
        out_shape=[jax.ShapeDtypeStruct((m // ROW_TILE, d, ROW_TILE), BF16), kshape, tshape, kshape, tshape],
        scratch_shapes=[pltpu.VMEM((2 * PEER_HEADS * PEER_NKEYS, ROW_TILE), F32)],
        compiler_params=_cparams(("parallel", "parallel")),
        name="peer_route",
    )(x, g, shift, scale, wq, kbig)


PEER_TOK = 2 * ROW_TILE
PEER_ECH = 1024


def _expert_body(h_ref, u0_ref, u_ref, vt_ref, a_ref, e1_ref, b_ref, e2_ref, x_ref, g0_ref, g1_ref,
                 fg_ref, o_ref, acc_ref, w0_ref, w1_ref, uf0_ref, uf1_ref, thb_ref, e1b_ref, *,
                 final, n_ch):
    e = pl.program_id(1)
    nk = PEER_NKEYS
    n_il = PEER_ECH // nk
    sub = TILE_ROWS
    ib, jb = 8, nk // 4
    n_ib, n_jb = n_il // ib, nk // jb
    tn = (((1,), (1,)), ((), ()))
    mq = V7X_MXU_DIM
    uf_refs = (uf0_ref, uf1_ref)
    w_refs = (w0_ref, w1_ref)

    def gate_block(g, cur):
        uf_ref, w_ref = uf_refs[cur], w_refs[cur]
        lt = g // (n_ib * n_jb)
        i0 = ((g // n_jb) % n_ib) * ib
        j0 = (g % n_jb) * jb
        gates = [None] * ib
        for hd in range(PEER_HEADS):
            bj = b_ref[hd, lt, j0:j0 + jb, :].reshape(jb // sub, sub, V7X_LANES)
            e2j = e2_ref[hd, lt, j0:j0 + jb, :].reshape(jb // sub, sub, V7X_LANES)
            for k in range(ib):
                th = thb_ref[lt, hd * n_il + i0 + k][None]
                e1 = e1b_ref[lt, hd * n_il + i0 + k][None]
                t = jnp.where(bj >= th, e1 * e2j, 0.0)
                gates[k] = t if gates[k] is None else gates[k] + t
        lpm = mq // V7X_LANES
        wl = pl.ds(_mult((lt % lpm) * V7X_LANES, V7X_LANES), V7X_LANES)
        for k in range(ib):
            rows = pl.ds((i0 + k) * nk + j0, jb)
            act = _gelu2(uf_ref[lt, rows, :]) * gates[k].reshape(jb, V7X_LANES)
            w_ref[lt // lpm, rows, wl] = act.astype(BF16)

    def k_major(ref, r0):
        return jnp.concatenate([ref[k, pl.ds(r0, mq), :] for k in range(ref.shape[0])], axis=1)

    def next_u(it, oth):
        r0 = _mult((it // 2) * mq, mq)
        res = jnp.dot(k_major(u_ref, r0), h_ref[it % 2], preferred_element_type=F32)
        for k in range(mq // V7X_LANES):
            uf_refs[oth][(it % 2) * (mq // V7X_LANES) + k, pl.ds(r0, mq), :] = (
                res[:, k * V7X_LANES:(k + 1) * V7X_LANES])

    def prev_out(it, oth):
        r0 = _mult((it // 2) * mq, mq)
        acc_ref[it % 2, pl.ds(r0, mq), :] += jnp.dot(
            k_major(vt_ref, r0), w_refs[oth][it % 2], preferred_element_type=F32)

    n_it = (PEER_ECH // mq) * (PEER_TOK // mq)
    per_it = (PEER_TOK // V7X_LANES) * n_ib * n_jb // n_it

    def run_chunk(with_prev, cur):
        for lt in range(PEER_TOK // V7X_LANES):
            tl = slice(lt * V7X_LANES, (lt + 1) * V7X_LANES)
            for hd in range(PEER_HEADS):
                for il in range(n_il):
                    thb_ref[lt, hd * n_il + il] = jnp.broadcast_to(a_ref[hd, il:il + 1, tl], (sub, V7X_LANES))
                    e1b_ref[lt, hd * n_il + il] = jnp.broadcast_to(e1_ref[hd, il:il + 1, tl], (sub, V7X_LANES))

        lpm = mq // V7X_LANES

        def step(it, carry):
            for k in range(per_it):
                gate_block(it * per_it + k, cur)
            next_u(it, 1 - cur)
            if with_prev:
                prev_out(it, 1 - cur)
            elif it < PEER_ECH // mq:
                r0 = it * mq
                res = jnp.dot(k_major(u0_ref, r0), h_ref[1], preferred_element_type=F32)
                for k in range(lpm):
                    uf0_ref[lpm + k, pl.ds(r0, mq), :] = res[:, k * V7X_LANES:(k + 1) * V7X_LANES]
            return carry

        for it in range(n_it):
            step(it, 0)

    @pl.when(e == 0)
    def _():
        acc_ref[...] = jnp.zeros_like(acc_ref)
        u_all = jnp.concatenate([u0_ref[k] for k in range(u0_ref.shape[0])], axis=1)
        u_first = jnp.dot(u_all, h_ref[0], preferred_element_type=F32)
        for k in range(mq // V7X_LANES):
            uf0_ref[k] = u_first[:, k * V7X_LANES:(k + 1) * V7X_LANES]
        run_chunk(False, 0)

    for par in range(2):
        @pl.when(jnp.logical_and(jnp.logical_and(e > 0, e < n_ch), e % 2 == par))
        def _():
            run_chunk(True, par)

    @pl.when(e == n_ch)
    def _():
        vt_all = jnp.concatenate([vt_ref[k] for k in range(vt_ref.shape[0])], axis=1)
        for half, gref in enumerate((g0_ref, g1_ref)):
            y = acc_ref[half] + jnp.dot(vt_all, w_refs[(n_ch - 1) % 2][half], preferred_element_type=F32)
            sl = slice(half * ROW_TILE, (half + 1) * ROW_TILE)
            xn = x_ref[sl, :] + gref[0, 0] * y.T
            if final:
                xn = xn * lax.rsqrt(jnp.mean(xn * xn, axis=-1, keepdims=True) + EPS) * fg_ref[...]
            o_ref[sl, :] = xn


def peer_experts(h, u_tab, vt_tab, a, e1, bm, e2, x, gate, fg, *, final, t0=0):
    b, s, d = x.shape
    nt = s // ROW_TILE
    m = b * s
    n_exp = u_tab.shape[1]
    i_per = PEER_ECH // PEER_NKEYS

    def gidx(half):
        def f(i, e):
            t = 2 * i + half
            return (t // nt, jnp.minimum(t % nt + t0, 1), 0, 0)
        return f

    n_ch = n_exp // PEER_ECH
    last = n_ch - 1
    mq = V7X_MXU_DIM
    n_lt = PEER_TOK // V7X_LANES
    assert mq == ROW_TILE and PEER_TOK == 2 * mq
    out = pl.pallas_call(
        functools.partial(_expert_body, final=final, n_ch=n_ch),
        grid=(m // PEER_TOK, n_ch + 1),
        in_specs=[
            pl.BlockSpec((PEER_TOK // mq, d, mq), lambda i, e: (i, 0, 0)),
            pl.BlockSpec((d // mq, PEER_ECH, mq), lambda i, e: (0, 0, 0)),
            pl.BlockSpec((d // mq, PEER_ECH, mq), lambda i, e: (0, jnp.minimum(e + 1, last), 0)),
            pl.BlockSpec((PEER_ECH // mq, d, mq), lambda i, e: (jnp.maximum(e - 1, 0), 0, 0)),
            pl.BlockSpec((PEER_HEADS, i_per, PEER_TOK), lambda i, e: (0, jnp.minimum(e, last), i)),
            pl.BlockSpec((PEER_HEADS, i_per, PEER_TOK), lambda i, e: (0, jnp.minimum(e, last), i)),
            pl.BlockSpec((PEER_HEADS, PEER_TOK // V7X_LANES, PEER_NKEYS, V7X_LANES), lambda i, e: (0, i, 0, 0)),
            pl.BlockSpec((PEER_HEADS, PEER_TOK // V7X_LANES, PEER_NKEYS, V7X_LANES), lambda i, e: (0, i, 0, 0)),
            pl.BlockSpec((PEER_TOK, d), lambda i, e: (i, 0)),
            pl.BlockSpec((1, 1, 1, d), gidx(0)),
            pl.BlockSpec((1, 1, 1, d), gidx(1)),
            pl.BlockSpec((1, d), lambda i, e: (0, 0)),
        ],
        out_specs=pl.BlockSpec((PEER_TOK, d), lambda i, e: (i, 0)),
        out_shape=jax.ShapeDtypeStruct((m, d), F32),
        scratch_shapes=[pltpu.VMEM((PEER_TOK // mq, d, mq), F32),
                        pltpu.VMEM((PEER_TOK // mq, PEER_ECH, mq), BF16),
                        pltpu.VMEM((PEER_TOK // mq, PEER_ECH, mq), BF16),
                        pltpu.VMEM((n_lt, PEER_ECH, V7X_LANES), F32),
                        pltpu.VMEM((n_lt, PEER_ECH, V7X_LANES), F32),
                        pltpu.VMEM((n_lt, PEER_HEADS * i_per, TILE_ROWS, V7X_LANES), F32),
                        pltpu.VMEM((n_lt, PEER_HEADS * i_per, TILE_ROWS, V7X_LANES), F32)],
        compiler_params=_cparams(("parallel", "arbitrary")),
        name="peer_experts",
    )(h, u_tab, u_tab, vt_tab, a, e1, bm, e2, x.reshape(m, d), gate, gate, fg)
    return out.reshape(b, s, d)


def _axial_tables(rows, rope_dim, ctx_len, lane0):
    n_freq = rope_dim // 4
    half = rope_dim // 2
    inv_freq = ROPE_THETA ** (-jnp.arange(n_freq, dtype=F32) / n_freq)
    r = jnp.repeat(jnp.arange(rows, dtype=F32), GRID_W)
    col = jnp.tile(jnp.arange(GRID_W, dtype=F32), rows)
    ang = jnp.concatenate([r[:, None] * inv_freq, col[:, None] * inv_freq], axis=-1)
    cos, sin = jnp.cos(ang), jnp.sin(ang)
    n = cos.shape[0]
    period = 64 if rope_dim == 64 else V7X_LANES
    c_blk = jnp.ones((n, period), F32).at[:, lane0:lane0 + rope_dim].set(jnp.concatenate([cos, cos], -1))
    s1_blk = jnp.zeros((n, period), F32).at[:, lane0:lane0 + half].set(-sin)
    s2_blk = jnp.zeros((n, period), F32).at[:, lane0 + half:lane0 + rope_dim].set(sin)
    reps = V7X_LANES // period
    out = []
    for blk, fill in ((c_blk, 1.0), (s1_blk, 0.0), (s2_blk, 0.0)):
        t = jnp.tile(blk, (1, reps))
        out.append(jnp.concatenate([jnp.full((ctx_len, V7X_LANES), fill, F32), t], axis=0))
    return out


def _block_diag(w):
    n, a, b = w.shape
    eye = jnp.eye(n, dtype=w.dtype)
    return (eye[:, None, :, None] * w[:, :, None, :]).reshape(n * a, n * b)


def kernel(x, c, ctx, c_ctx, w_mod, b_mod, norm1_g, norm2_g, w_in, conv_w, conv_b, lru_wa, lru_ba,
           lru_wi, lru_bi, lru_lambda, diff_lam, diff_subln_g, gqa_qnorm_g, gqa_knorm_g, mla_qnorm_g,
           mla_w_uq, mla_kvnorm_g, mla_w_ukv, w_branch, w_out, peer_wq, peer_keys, peer_u, peer_v,
           final_norm_g):
    bsz, seq, d = x.shape
    ctx_len = ctx.shape[1]
    depth = w_in.shape[0]
    assert ctx_len == ROW_TILE and seq % PEER_TOK == 0 and seq % GRID_W == 0
    rows = seq // GRID_W
    s_all = ctx_len + seq
    xs = jnp.concatenate([ctx, x], axis=1)

    tabs = _axial_tables(rows, DIFF_DK, ctx_len, 0) + _axial_tables(rows, MLA_ROPE, ctx_len, MLA_NOPE)
    grp = jnp.arange(BRANCH_W) // GQA_DH
    avg = (grp[:, None] == grp[None, :]).astype(BF16) * (1.0 / GQA_DH)
    sc_in = jnp.zeros((16, d), F32).at[:bsz].set(jax.nn.silu(c)).at[bsz].set(jax.nn.silu(c_ctx)).astype(BF16)
    zero1 = jnp.zeros((1,), F32)
    ones_g = jnp.ones((1, V7X_LANES), F32)

    for l in range(depth):
        lam_init = 0.8 - 0.6 * math.exp(-0.3 * l)
        mod_all = matmul(sc_in, w_mod[l].astype(BF16)) + b_mod[l]
        mod_b = mod_all[:bsz].reshape(bsz, N_MOD, d)
        mod_c = jnp.broadcast_to(mod_all[bsz].reshape(1, N_MOD, d), (bsz, N_MOD, d))
        mods = [jnp.stack([mod_c[:, k], mod_b[:, k]], axis=1)[:, :, None, :] for k in range(N_MOD)]

        wl = w_in[l]
        o = 0
        parts = []
        for w in (512, 512, 512, 512, 512, 512, 128, 128, 384, 256, 32, 4096):
            parts.append(wl[:, o:o + w])
            o += w
        xa, ya, qb, kb, vb, qg, kg, vg, cq, ckv, kr, zg = parts
        kr_blk = jnp.zeros((d, V7X_LANES), F32).at[:, MLA_NOPE:MLA_NOPE + MLA_ROPE].set(kr)
        w_cat = jnp.concatenate([zg, xa, ya, qb, kb, vb, qg, kg, vg, cq, ckv, kr_blk], axis=1).astype(BF16)

        z = adaln_matmul(xs, norm1_g[l][None], mods[0], mods[1], w_cat, ctx_len)

        negc = (-LRU_C * jax.nn.softplus(-lru_lambda[l]))[:, None, None, :]
        wg = jnp.stack([jnp.concatenate([_block_diag(lru_wa[l, dd]), _block_diag(lru_wi[l, dd])], axis=1)
                        for dd in range(2)]).astype(BF16)[:, None]
        bg = jnp.stack([jnp.concatenate([lru_ba[l, dd], lru_bi[l, dd]]) for dd in range(2)])[:, None, None, :]
        cb = conv_b[l][None]
        hf = lru_scan(z, 8, 9, conv_w[l], cb, wg[0], bg[0], negc[0], None, reverse=False)
        oa = lru_scan(z, 8, 9, conv_w[l], cb, wg[1], bg[1], negc[1], hf, reverse=True)

        uq = mla_w_uq[l].reshape(MLA_Q_RANK, MLA_HEADS, MLA_NOPE + MLA_ROPE)
        wuq = jnp.pad(uq, ((0, 0), (0, 0), (0, V7X_LANES - MLA_NOPE - MLA_ROPE))).reshape(MLA_Q_RANK, -1)
        ukv = mla_w_ukv[l].reshape(MLA_KV_RANK, MLA_HEADS, MLA_NOPE + MLA_DV)
        wuk = jnp.pad(ukv[:, :, :MLA_NOPE], ((0, 0), (0, 0), (0, V7X_LANES - MLA_NOPE))).reshape(MLA_KV_RANK, -1)
        wuv = ukv[:, :, MLA_NOPE:].reshape(MLA_KV_RANK, -1)
        qd, kd, vd, qq, kq, vq, qm, km, vm = attn_prep(
            z, tabs, avg, jnp.tile(gqa_qnorm_g[l], GQA_HEADS)[None], jnp.tile(gqa_knorm_g[l], GQA_KV_HEADS)[None],
            mla_qnorm_g[l][None], mla_kvnorm_g[l][None], wuq.astype(BF16), wuk.astype(BF16), wuv.astype(BF16))
        lv = diff_lam[l]
        lam = (jnp.exp(jnp.sum(lv[0] * lv[1])) - jnp.exp(jnp.sum(lv[2] * lv[3])) + lam_init).reshape(1)
        t0 = 1 if l == depth - 1 else 0
        ob = attention(qd, kd, vd, lam, diff_subln_g[l][None], mode="diff", out_scale=1.0 - lam_init, t0=t0)
        og = attention(qq, kq, vq, zero1, ones_g, mode="pair", t0=t0)
        om = attention(qm, km, vm, zero1, ones_g, mode="mla", t0=t0)

        xs = merge(oa, ob, og, om, z, w_branch[l].astype(BF16), w_out[l].astype(BF16), xs, mods[2], t0=t0)

        kbig = jnp.stack([_block_diag(peer_keys[l, hd]) for hd in range(PEER_HEADS)]).astype(BF16)
        h2, pa, pb, pe1, pe2 = peer_route(xs, norm2_g[l][None], mods[3], mods[4], peer_wq[l].astype(BF16), kbig,
                                          t0=t0)
        n_exp = peer_u.shape[1]
        u_km = peer_u[l].astype(BF16).reshape(n_exp, d // V7X_MXU_DIM, V7X_MXU_DIM).transpose(1, 0, 2)
        vt_km = peer_v[l].astype(BF16).reshape(n_exp // V7X_MXU_DIM, V7X_MXU_DIM, d).transpose(0, 2, 1)
        xs = peer_experts(h2, u_km, vt_km, pa, pe1, pb, pe2, xs, mods[5],
                          final_norm_g[None], final=(l == depth - 1), t0=t0)
    return xs
```

```python
import functools
import math

import jax
import jax.numpy as jnp
from jax import lax
from jax.experimental import pallas as pl
from jax.experimental.pallas import tpu as pltpu

F32 = jnp.float32
BF16 = jnp.bfloat16

EPS = 1e-6
GRID_W = 64
ROPE_THETA = 10000.0
N_MOD = 6

D_RNN = 512
CONV_W = 4
LRU_C = 8.0

DIFF_DK = 64
GQA_HEADS = 8
GQA_KV_HEADS = 2
GQA_DH = 64
MLA_HEADS = 8
MLA_NOPE = 64
MLA_ROPE = 32
MLA_DV = 64
MLA_Q_RANK = 384
MLA_KV_RANK = 256
N_BRANCH = 4
BRANCH_W = 512

PEER_HEADS = 8
PEER_NKEYS = 128
PEER_DK = 128
PEER_TOPK = 16

V7X_LANES = 128
V7X_SUBLANES = 8
V7X_VMEM_BYTES = 64 * 1024 * 1024
V7X_MXU_DIM = 256
VMEM_LIMIT = V7X_VMEM_BYTES * 3 // 4

ROW_TILE = 256


def _cparams(sem):
    return pltpu.CompilerParams(dimension_semantics=sem, vmem_limit_bytes=VMEM_LIMIT)


def _mm_body(a_ref, w_ref, o_ref):
    o_ref[...] = jnp.dot(a_ref[...], w_ref[...], preferred_element_type=F32).astype(o_ref.dtype)


def matmul(a, w, out_dtype=F32, tm=512, tn=512):
    m, k = a.shape
    _, n = w.shape
    tm = math.gcd(tm, m)
    tn = math.gcd(tn, n)
    return pl.pallas_call(
        _mm_body,
        grid=(n // tn, m // tm),
        in_specs=[
            pl.BlockSpec((tm, k), lambda j, i: (i, 0)),
            pl.BlockSpec((k, tn), lambda j, i: (0, j)),
        ],
        out_specs=pl.BlockSpec((tm, tn), lambda j, i: (i, j)),
        out_shape=jax.ShapeDtypeStruct((m, n), out_dtype),
        compiler_params=_cparams(("parallel", "parallel")),
        name="matmul",
    )(a, w)


def _adaln_mm_body(x_ref, g_ref, sh_ref, sc_ref, w_ref, o_ref, h_ref, *, ctx_len):
    tm = x_ref.shape[1]

    @pl.when(pl.program_id(2) == 0)
    def _():
        x = x_ref[0]
        y = x * lax.rsqrt(jnp.mean(x * x, axis=-1, keepdims=True) + EPS) * g_ref[...]
        row = pl.program_id(1) * tm + lax.broadcasted_iota(jnp.int32, (tm, 1), 0)
        is_ctx = row < ctx_len
        scale = jnp.where(is_ctx, sc_ref[0, 0], sc_ref[0, 1])
        shift = jnp.where(is_ctx, sh_ref[0, 0], sh_ref[0, 1])
        h_ref[...] = (y * (1.0 + scale) + shift).astype(BF16)

    o_ref[0] = jnp.dot(h_ref[...], w_ref[...], preferred_element_type=F32).astype(o_ref.dtype)


def adaln_matmul(x, g, shift, scale, w, ctx_len, n_row_tiles=4, tn=1024):
    b, s, d = x.shape
    n = w.shape[1]
    tm = s // n_row_tiles
    assert tm * n_row_tiles == s and tm % (2 * V7X_SUBLANES) == 0 and n % tn == 0
    mod = pl.BlockSpec((1, 2, 1, d), lambda bi, ti, j: (bi, 0, 0, 0))
    return pl.pallas_call(
        functools.partial(_adaln_mm_body, ctx_len=ctx_len),
        grid=(b, n_row_tiles, n // tn),
        in_specs=[
            pl.BlockSpec((1, tm, d), lambda bi, ti, j: (bi, ti, 0)),
            pl.BlockSpec((1, d), lambda bi, ti, j: (0, 0)),
            mod, mod,
            pl.BlockSpec((d, tn), lambda bi, ti, j: (0, j)),
        ],
        out_specs=pl.BlockSpec((1, tm, tn), lambda bi, ti, j: (bi, ti, j)),
        out_shape=jax.ShapeDtypeStruct((b, s, n), BF16),
        scratch_shapes=[pltpu.VMEM((tm, d), BF16)],
        compiler_params=_cparams(("parallel", "parallel", "arbitrary")),
        name="adaln_matmul",
    )(x, g, shift, scale, w)


def _mult(x, m):
    return x if isinstance(x, int) else pl.multiple_of(x, m)


def _gelu2(x):
    c = math.sqrt(2.0 / math.pi)
    return x * (1.0 + jnp.tanh(x * (c + (0.044715 * c) * (x * x))))


def _gelu(x):
    return 0.5 * _gelu2(x)


def _sigmoid(x):
    return 0.5 + 0.5 * jnp.tanh(0.5 * x)


TILE_ROWS = V7X_SUBLANES
HALO_ROWS = 2 * V7X_SUBLANES


def _lru_body(x_ref, xp_ref, xn_ref, cw_ref, cb_ref, wg_ref, bg_ref, nc_ref, *rest,
              reverse, n_chunks):
    if reverse:
        ya_ref, hf_ref, o_ref, carry_ref = rest
    else:
        o_ref, carry_ref = rest
    step = pl.program_id(1)
    if reverse:
        chunk = jnp.where(step == 0, 0, n_chunks - step)
    else:
        chunk = step
    rows = ROW_TILE
    c = x_ref.shape[-1]

    @pl.when(step == 0)
    def _():
        carry_ref[...] = jnp.zeros_like(carry_ref)

    x = x_ref[0].astype(F32)
    xx = jnp.concatenate([xp_ref[0].astype(F32), x, xn_ref[0].astype(F32)], axis=0)
    n_xx = rows + 2 * HALO_ROWS
    row = lax.broadcasted_iota(jnp.int32, (rows, c), 0)
    seg_start = jnp.logical_or(chunk == 0, chunk == 1)
    seg_end = jnp.logical_or(chunk == 0, chunk == n_chunks - 1)

    def shifted(d):
        return pltpu.roll(xx, (n_xx - d) % n_xx, axis=0)[HALO_ROWS:HALO_ROWS + rows]

    x_m1 = jnp.where(jnp.logical_and(seg_start, row == 0), 0.0, shifted(-1))
    x_p1 = jnp.where(jnp.logical_and(seg_end, row >= rows - 1), 0.0, shifted(1))
    x_p2 = jnp.where(jnp.logical_and(seg_end, row >= rows - 2), 0.0, shifted(2))
    cw = cw_ref[...]
    u = cw[0:1] * x_m1 + cw[1:2] * x + cw[2:3] * x_p1 + cw[3:4] * x_p2 + cb_ref[...]

    z = jnp.dot(u.astype(BF16), wg_ref[0], preferred_element_type=F32) + bg_ref[0]
    r = _sigmoid(z[:, :c])
    gi = _sigmoid(z[:, c:])
    a = jnp.exp(nc_ref[0] * r)
    bv = jnp.sqrt(1.0 - a * a) * (gi * u)

    sub = row % TILE_ROWS
    for dstep in (1, 2, 4):
        if reverse:
            sh = (rows - dstep) % rows
            keep = sub <= TILE_ROWS - 1 - dstep
        else:
            sh = dstep
            keep = sub >= dstep
        a_sh = jnp.where(keep, pltpu.roll(a, sh, axis=0), 1.0)
        b_sh = jnp.where(keep, pltpu.roll(bv, sh, axis=0), 0.0)
        bv = a * b_sh + bv
        a = a * a_sh

    carry = carry_ref[...]
    n_tiles = rows // TILE_ROWS
    order = range(n_tiles - 1, -1, -1) if reverse else range(n_tiles)
    hs = [None] * n_tiles
    for t in order:
        sl = slice(t * TILE_ROWS, (t + 1) * TILE_ROWS)
        h = bv[sl] + a[sl] * carry
        hs[t] = h
        edge = h[0:1] if reverse else h[TILE_ROWS - 1:TILE_ROWS]
        carry = jnp.broadcast_to(edge, (TILE_ROWS, c))
    carry_ref[...] = carry
    h_all = jnp.concatenate(hs, axis=0)
    if reverse:
        o_ref[0] = (_gelu(ya_ref[0].astype(F32)) * (hf_ref[0] + h_all)).astype(o_ref.dtype)
    else:
        o_ref[0] = h_all


def lru_scan(z, xa_col, ya_col, conv_w, conv_b, wg, bg, negc, hf, *, reverse):
    b, s, _ = z.shape
    c = D_RNN
    n_chunks = s // ROW_TILE
    per = ROW_TILE // HALO_ROWS
    n8 = s // HALO_ROWS
    if reverse:
        cidx = lambda st: jnp.where(st == 0, 0, n_chunks - st)
    else:
        cidx = lambda st: st
    in_specs = [
        pl.BlockSpec((1, ROW_TILE, c), lambda bi, st: (bi, cidx(st), xa_col)),
        pl.BlockSpec((1, HALO_ROWS, c), lambda bi, st: (bi, jnp.maximum(cidx(st) * per - 1, 0), xa_col)),
        pl.BlockSpec((1, HALO_ROWS, c), lambda bi, st: (bi, jnp.minimum((cidx(st) + 1) * per, n8 - 1), xa_col)),
        pl.BlockSpec((CONV_W, c), lambda bi, st: (0, 0)),
        pl.BlockSpec((1, c), lambda bi, st: (0, 0)),
        pl.BlockSpec((1, c, 2 * c), lambda bi, st: (0, 0, 0)),
        pl.BlockSpec((1, 1, 2 * c), lambda bi, st: (0, 0, 0)),
        pl.BlockSpec((1, 1, c), lambda bi, st: (0, 0, 0)),
    ]
    args = [z, z, z, conv_w, conv_b, wg, bg, negc]
    if reverse:
        in_specs += [
            pl.BlockSpec((1, ROW_TILE, c), lambda bi, st: (bi, cidx(st), ya_col)),
            pl.BlockSpec((1, ROW_TILE, c), lambda bi, st: (bi, cidx(st), 0)),
        ]
        args += [z, hf]
    return pl.pallas_call(
        functools.partial(_lru_body, reverse=reverse, n_chunks=n_chunks),
        grid=(b, n_chunks),
        in_specs=in_specs,
        out_specs=pl.BlockSpec((1, ROW_TILE, c), lambda bi, st: (bi, cidx(st), 0)),
        out_shape=jax.ShapeDtypeStruct((b, s, c), BF16 if reverse else F32),
        scratch_shapes=[pltpu.VMEM((TILE_ROWS, c), F32)],
        compiler_params=_cparams(("parallel", "arbitrary")),
        name="lru_rev" if reverse else "lru_fwd",
    )(*args)


def _tile_lanes(t, n):
    return jnp.concatenate([t] * n, axis=1) if n > 1 else t


def _rope(x, c, s1, s2, r):
    w = x.shape[-1]
    n = w // V7X_LANES
    return (x * _tile_lanes(c, n) + pltpu.roll(x, w - r, axis=1) * _tile_lanes(s1, n)
            + pltpu.roll(x, r, axis=1) * _tile_lanes(s2, n))


def _split_dot(x, m_ref):
    hi = x.astype(BF16)
    lo = (x - hi.astype(F32)).astype(BF16)
    m = m_ref[...]
    return jnp.dot(hi, m, preferred_element_type=F32) + jnp.dot(lo, m, preferred_element_type=F32)


def _with_ones(v):
    ones = jnp.ones((v.shape[0], V7X_LANES), v.dtype)
    parts = []
    for c in range(v.shape[1] // V7X_LANES):
        parts += [v[:, c * V7X_LANES:(c + 1) * V7X_LANES], ones]
    return jnp.concatenate(parts, axis=1)


def _rms(x, g):
    return x * lax.rsqrt(jnp.mean(x * x, axis=-1, keepdims=True) + EPS) * g


def _prep_body(qb_ref, kb_ref, vb_ref, qg_ref, tail_ref,
               c64_ref, s164_ref, s264_ref, cm_ref, s1m_ref, s2m_ref,
               avg_ref, gq_ref, gk_ref, gcq_ref, gckv_ref, wuq_ref, wuk_ref, wuv_ref,
               qd_ref, kd_ref, vd_ref, qq_ref, kq_ref, vq_ref, qm_ref, km_ref, vm_ref):
    c64, s164, s264 = c64_ref[...], s164_ref[...], s264_ref[...]
    cm, s1m, s2m = cm_ref[...], s1m_ref[...], s2m_ref[...]
    half64 = DIFF_DK // 2
    qd_ref[0] = (_rope(qb_ref[0].astype(F32), c64, s164, s264, half64) * (DIFF_DK ** -0.5)).astype(BF16)
    kd_ref[0] = _rope(kb_ref[0].astype(F32), c64, s164, s264, half64).astype(BF16)
    vd_ref[0] = _with_ones(vb_ref[0].astype(BF16))
    qg = qg_ref[0].astype(F32)
    ms = _split_dot(qg * qg, avg_ref)
    qn = qg * lax.rsqrt(ms + EPS) * gq_ref[...]
    qq_ref[0] = (_rope(qn, c64, s164, s264, GQA_DH // 2) * (GQA_DH ** -0.5)).astype(BF16)
    tail = tail_ref[0].astype(F32)
    kg = tail[:, 0:128]
    msk = _split_dot(kg * kg, avg_ref.at[0:128, 0:128])
    kn = _rope(kg * lax.rsqrt(msk + EPS) * gk_ref[...], c64, s164, s264, GQA_DH // 2).astype(BF16)
    vg = tail[:, 128:256].astype(BF16)
    kq_ref[0] = jnp.concatenate([kn[:, 0:64], kn[:, 0:64], kn[:, 64:128], kn[:, 64:128]], axis=1)
    vq_ref[0] = _with_ones(jnp.concatenate([vg[:, 0:64], vg[:, 0:64], vg[:, 64:128], vg[:, 64:128]], axis=1))
    cq = tail[:, 256:256 + MLA_Q_RANK]
    ckv = tail[:, 640:640 + MLA_KV_RANK]
    kr = tail[:, 896:1024]
    qf = jnp.dot(_rms(cq, gcq_ref[...]).astype(BF16), wuq_ref[...], preferred_element_type=F32)
    scale = (MLA_NOPE + MLA_ROPE) ** -0.5
    qm_ref[0] = (_rope(qf, cm, s1m, s2m, MLA_ROPE // 2) * scale).astype(BF16)
    ckvn = _rms(ckv, gckv_ref[...]).astype(BF16)
    kf = jnp.dot(ckvn, wuk_ref[...], preferred_element_type=F32)
    krr = _rope(kr, cm, s1m, s2m, MLA_ROPE // 2)
    km_ref[0] = (kf + _tile_lanes(krr, MLA_HEADS)).astype(BF16)
    vm_ref[0] = _with_ones(jnp.dot(ckvn, wuv_ref[...], preferred_element_type=F32).astype(BF16))


def attn_prep(z, tabs, avg, gq, gk, gcq, gckv, wuq, wuk, wuv):
    b, s, _ = z.shape
    nt = s // ROW_TILE
    zspec = lambda w, idx: pl.BlockSpec((1, ROW_TILE, w), lambda ti, bi: (bi, ti, idx))
    tab = pl.BlockSpec((ROW_TILE, V7X_LANES), lambda ti, bi: (ti, 0))
    full = lambda a: pl.BlockSpec(a.shape, lambda ti, bi: (0,) * a.ndim)
    ospec = lambda w: pl.BlockSpec((1, ROW_TILE, w), lambda ti, bi: (bi, ti, 0))
    oshape = lambda w: jax.ShapeDtypeStruct((b, s, w), BF16)
    widths = (512, 512, 1024, 512, 256, 512, 1024, 1024, 1024)
    return pl.pallas_call(
        _prep_body,
        grid=(nt, b),
        in_specs=[zspec(512, 10), zspec(512, 11), zspec(512, 12), zspec(512, 13), zspec(1024, 7)]
        + [tab] * 6 + [full(a) for a in (avg, gq, gk, gcq, gckv, wuq, wuk, wuv)],
        out_specs=[ospec(w) for w in widths],
        out_shape=[oshape(w) for w in widths],
        compiler_params=_cparams(("parallel", "parallel")),
        name="attn_prep",
    )(z, z, z, z, z, *tabs, avg, gq, gk, gcq, gckv, wuq, wuk, wuv)


ATTN_GROUPS = {"pair": 2, "diff": 2, "mla": 2}


def _attn_body(lam_ref, q_ref, k_ref, v_ref, g_ref, o_ref, *, mode, out_scale, t0, groups):
    ti = pl.program_id(2)
    lane = lax.broadcasted_iota(jnp.int32, (ROW_TILE, V7X_LANES), 1)
    lo_half = lane < (V7X_LANES // 2)

    ln = V7X_LANES

    def run(n_keys):
        for grp in range(groups):
            outs = []
            for j in range(2):
                if mode == "mla":
                    c0 = (2 * grp + j) * ln
                    qj = q_ref[0, :, c0:c0 + ln]
                    kj = k_ref[0, 0:n_keys, c0:c0 + ln]
                else:
                    q = q_ref[0, :, grp * ln:(grp + 1) * ln]
                    keep = lo_half if j == 0 else jnp.logical_not(lo_half)
                    qj = jnp.where(keep, q.astype(F32), 0.0).astype(BF16)
                    kc = (grp // 2 if mode == "pair" else grp) * ln
                    kj = k_ref[0, 0:n_keys, kc:kc + ln]
                vc = (grp // 2 if mode == "pair" else grp) * 2 * ln
                hk = (n_keys // (2 * ln)) * ln
                tn = (((1,), (1,)), ((), ()))
                s = jnp.concatenate(
                    [lax.dot_general(qj, kj[0:hk], tn, preferred_element_type=F32).astype(BF16),
                     lax.dot_general(qj, kj[hk:], tn, preferred_element_type=F32).astype(BF16)], axis=1)
                m = jnp.max(s, axis=-1, keepdims=True)
                p = jnp.exp(s - m)
                o = (jnp.dot(p[:, 0:hk], v_ref[0, 0:hk, vc:vc + 2 * ln], preferred_element_type=F32)
                     + jnp.dot(p[:, hk:], v_ref[0, hk:n_keys, vc:vc + 2 * ln], preferred_element_type=F32))
                outs.append(o[:, 0:ln] / o[:, ln:2 * ln])
            if mode == "diff":
                o = outs[0] - lam_ref[0] * outs[1]
                o = o * lax.rsqrt(jnp.mean(o * o, axis=-1, keepdims=True) + EPS) * g_ref[...] * out_scale
            else:
                o = jnp.where(lo_half, outs[0], outs[1])
            o_ref[0, :, grp * ln:(grp + 1) * ln] = o.astype(o_ref.dtype)

    if t0 == 0:
        @pl.when(ti == 0)
        def _():
            run(ROW_TILE)

        @pl.when(ti > 0)
        def _():
            run(k_ref.shape[1])
    else:
        run(k_ref.shape[1])


def attention(q, k, v, lam, g, *, mode, out_scale=1.0, t0=0):
    b, s, wq = q.shape
    groups = ATTN_GROUPS[mode]
    steps = 4 // groups
    qw = wq // steps
    kw = k.shape[-1] // steps
    vw = v.shape[-1] // steps
    ow = groups * V7X_LANES
    nt = s // ROW_TILE - t0
    return pl.pallas_call(
        functools.partial(_attn_body, mode=mode, out_scale=out_scale, t0=t0, groups=groups),
        grid=(b, steps, nt),
        in_specs=[
            pl.BlockSpec(memory_space=pltpu.SMEM),
            pl.BlockSpec((1, ROW_TILE, qw), lambda bi, gi, ti: (bi, ti + t0, gi)),
            pl.BlockSpec((1, s, kw), lambda bi, gi, ti: (bi, 0, gi)),
            pl.BlockSpec((1, s, vw), lambda bi, gi, ti: (bi, 0, gi)),
            pl.BlockSpec((1, V7X_LANES), lambda bi, gi, ti: (0, 0)),
        ],
        out_specs=pl.BlockSpec((1, ROW_TILE, ow), lambda bi, gi, ti: (bi, ti, gi)),
        out_shape=jax.ShapeDtypeStruct((b, nt * ROW_TILE, 4 * V7X_LANES), BF16),
        compiler_params=_cparams(("parallel", "parallel", "arbitrary")),
        name="attn_" + mode,
    )(lam, q, k, v, g)


def _merge_body(oa_ref, ob_ref, og_ref, om_ref, zg_ref, wb_ref, wo_ref, x_ref, gate_ref, o_ref):
    d = x_ref.shape[-1]
    merged = None
    for k, o_k in enumerate((oa_ref, ob_ref, og_ref, om_ref)):
        t = jnp.dot(o_k[0], wb_ref[k], preferred_element_type=F32)
        t = t * _sigmoid(zg_ref[0, :, k * d:(k + 1) * d].astype(F32))
        merged = t if merged is None else merged + t
    y = jnp.dot(merged.astype(BF16), wo_ref[...], preferred_element_type=F32)
    o_ref[0] = x_ref[0] + gate_ref[0, 0] * y


def merge(oa, ob, og, om, z, wb, wo, x, gate, t0=0):
    b, s, d = x.shape
    nt = s // ROW_TILE - t0
    full = lambda w: pl.BlockSpec((1, ROW_TILE, w), lambda bi, ti: (bi, ti + t0, 0))
    part = pl.BlockSpec((1, ROW_TILE, BRANCH_W), lambda bi, ti: (bi, ti, 0))
    return pl.pallas_call(
        _merge_body,
        grid=(b, nt),
        in_specs=[full(BRANCH_W), part, part, part,
                  full(N_BRANCH * d),
                  pl.BlockSpec(wb.shape, lambda bi, ti: (0, 0, 0)),
                  pl.BlockSpec(wo.shape, lambda bi, ti: (0, 0)),
                  full(d),
                  pl.BlockSpec((1, 1, 1, d), lambda bi, ti: (bi, jnp.minimum(ti + t0, 1), 0, 0))],
        out_specs=pl.BlockSpec((1, ROW_TILE, d), lambda bi, ti: (bi, ti, 0)),
        out_shape=jax.ShapeDtypeStruct((b, nt * ROW_TILE, d), F32),
        compiler_params=_cparams(("parallel", "parallel")),
        name="merge",
    )(oa, ob, og, om, z, wb, wo, x, gate)


NEG_BIG = float(jnp.finfo(jnp.float32).min)


def _oddeven_merge(lo, hi, r):
    step = r * 2
    if step < hi - lo:
        yield from _oddeven_merge(lo, hi, step)
        yield from _oddeven_merge(lo + r, hi, step)
        yield from [(i, i + r) for i in range(lo + r, hi - r, step)]
    else:
        yield (lo, lo + r)


def _oddeven_sort(lo, hi):
    if hi - lo >= 1:
        mid = lo + (hi - lo) // 2
        yield from _oddeven_sort(lo, mid)
        yield from _oddeven_sort(mid + 1, hi)
        yield from _oddeven_merge(lo, hi, 1)


def _bitonic_merge(n):
    pairs, s = [], n // 2
    while s >= 1:
        pairs += [(i, i + s) for i in range(n) if (i // s) % 2 == 0]
        s //= 2
    return pairs


_SORT16 = tuple(_oddeven_sort(0, PEER_TOPK - 1))
_BITONIC16 = tuple(_bitonic_merge(PEER_TOPK))


def _route_body(x_ref, g_ref, sh_ref, sc_ref, wq_ref, kb_ref,
                h_ref, a_ref, b_ref, e1_ref, e2_ref, st_ref):
    x = x_ref[0]
    ms = jnp.mean(x * x, axis=-1, keepdims=True)
    hf = x * lax.rsqrt(ms + EPS) * g_ref[...] * (1.0 + sc_ref[0, 0]) + sh_ref[0, 0]
    h_ref[0] = hf.T.astype(BF16)
    h = hf.astype(BF16)
    q = jnp.dot(h, wq_ref[...], preferred_element_type=F32).astype(BF16)
    nk = PEER_NKEYS
    for hd in range(PEER_HEADS):
        st_ref[hd * 2 * nk:(hd + 1) * 2 * nk, :] = lax.dot_general(
            kb_ref[hd], q[:, hd * PEER_DK:(hd + 1) * PEER_DK], (((1,), (1,)), ((), ())),
            preferred_element_type=F32)

    def exchange(t, pairs):
        for i, j in pairs:
            t[i], t[j] = jnp.maximum(t[i], t[j]), jnp.minimum(t[i], t[j])

    def merge_top(a, b):
        t = [jnp.maximum(a[k], b[PEER_TOPK - 1 - k]) for k in range(PEER_TOPK)]
        exchange(t, _BITONIC16)
        return t

    def across_sublanes(t):
        for sh in (4, 2, 1):
            t = merge_top(t, [pltpu.roll(x, sh, axis=0) for x in t])
        return t

    def top16(s):
        t = [s[k * TILE_ROWS:(k + 1) * TILE_ROWS] for k in range(nk // TILE_ROWS)]
        exchange(t, _SORT16)
        return across_sublanes(t)

    row8 = lax.broadcasted_iota(jnp.int32, (TILE_ROWS, V7X_LANES), 0)
    n_q = jnp.zeros((TILE_ROWS, V7X_LANES), jnp.int32)
    for p in range(TILE_ROWS):
        n_q = jnp.where(row8 == p, PEER_TOPK // (p + 1), n_q)

    def rank_rows(t, lo):
        out = t[lo]
        for r in range(1, TILE_ROWS):
            out = jnp.where(row8 == r, t[lo + r], out)
        return out

    def head(it, carry):
        hd = it // (ROW_TILE // V7X_LANES)
        lt = it % (ROW_TILE // V7X_LANES)
        base = pl.multiple_of(hd * (2 * nk), 2 * nk)
        tl = pl.ds(pl.multiple_of(lt * V7X_LANES, V7X_LANES), V7X_LANES)
        s1 = st_ref[pl.ds(base, nk), tl]
        s2 = st_ref[pl.ds(base + nk, nk), tl]
        ta = top16(s1)
        tb = top16(s2)
        lo = rank_rows(ta, 0)
        cand = [jnp.where(n_q > q, lo + tb[q], NEG_BIG) for q in range(PEER_TOPK)]
        cand[PEER_TOPK - 1] = jnp.maximum(cand[PEER_TOPK - 1], rank_rows(ta, TILE_ROWS) + tb[0])
        exchange(cand, _BITONIC16)
        cand = across_sublanes(cand)
        top = cand[0][0:1]
        tau = cand[PEER_TOPK - 1][0:1]
        zsum = jnp.zeros_like(top)
        for k in range(PEER_TOPK):
            zsum = zsum + jnp.exp(cand[k][0:1] - top)
        a_ref[hd, :, tl] = tau - s1
        e1_ref[hd, :, tl] = jnp.exp(s1 - ta[0][0:1]) * (0.5 / zsum)
        b_ref[hd, lt] = s2
        e2_ref[hd, lt] = jnp.exp(s2 - tb[0][0:1])
        return carry

    lax.fori_loop(0, PEER_HEADS * (ROW_TILE // V7X_LANES), head, 0)


def peer_route(x, g, shift, scale, wq, kbig, t0=0):
    b, s, d = x.shape
    nt = s // ROW_TILE
    m = b * s
    seg = lambda bi, ti: (bi, jnp.minimum(ti + t0, 1), 0, 0)
    flat = lambda bi, ti: (0, 0, bi * nt + ti)
    kspec = pl.BlockSpec((PEER_HEADS, PEER_NKEYS, ROW_TILE), flat)
    kshape = jax.ShapeDtypeStruct((PEER_HEADS, PEER_NKEYS, m), F32)
    lt_per = ROW_TILE // V7X_LANES
    tspec = pl.BlockSpec((PEER_HEADS, lt_per, PEER_NKEYS, V7X_LANES), lambda bi, ti: (0, bi * nt + ti, 0, 0))
    tshape = jax.ShapeDtypeStruct((PEER_HEADS, m // V7X_LANES, PEER_NKEYS, V7X_LANES), F32)
    return pl.pallas_call(
        _route_body,
        grid=(b, nt),
        in_specs=[
            pl.BlockSpec((1, ROW_TILE, d), lambda bi, ti: (bi, ti, 0)),
            pl.BlockSpec((1, d), lambda bi, ti: (0, 0)),
            pl.BlockSpec((1, 1, 1, d), seg),
            pl.BlockSpec((1, 1, 1, d), seg),
            pl.BlockSpec(wq.shape, lambda bi, ti: (0, 0)),
            pl.BlockSpec(kbig.shape, lambda bi, ti: (0, 0, 0)),
        ],
        out_specs=[pl.BlockSpec((1, d, ROW_TILE), lambda bi, ti: (bi * nt + ti, 0, 0)),
                   kspec, tspec, kspec, tspec],
        out_shape=[jax.ShapeDtypeStruct((m // ROW_TILE, d, ROW_TILE), BF16), kshape, tshape, kshape, tshape],
        scratch_shapes=[pltpu.VMEM((2 * PEER_HEADS * PEER_NKEYS, ROW_TILE), F32)],
        compiler_params=_cparams(("parallel", "parallel")),
        name="peer_route",
    )(x, g, shift, scale, wq, kbig)


PEER_TOK = 2 * ROW_TILE
PEER_ECH = 1024


def _expert_body(h_ref, u0_ref, u_ref, vt_ref, a_ref, e1_ref, b_ref, e2_ref, x_ref, g0_ref, g1_ref,
                 fg_ref, o_ref, acc_ref, w0_ref, w1_ref, uf0_ref, uf1_ref, thb_ref, e1b_ref, *,
                 final, n_ch):
    e = pl.program_id(1)
    nk = PEER_NKEYS
    n_il = PEER_ECH // nk
    sub = TILE_ROWS
    ib, jb = 8, nk // 4
    n_ib, n_jb = n_il // ib, nk // jb
    mq = V7X_MXU_DIM
    uf_refs = (uf0_ref, uf1_ref)
    w_refs = (w0_ref, w1_ref)

    def gate_block(g, cur):
        uf_ref, w_ref = uf_refs[cur], w_refs[cur]
        lt = g // (n_ib * n_jb)
        i0 = ((g // n_jb) % n_ib) * ib
        j0 = (g % n_jb) * jb
        gates = [None] * ib
        for hd in range(PEER_HEADS):
            bj = b_ref[hd, lt, j0:j0 + jb, :].reshape(jb // sub, sub, V7X_LANES)
            e2j = e2_ref[hd, lt, j0:j0 + jb, :].reshape(jb // sub, sub, V7X_LANES)
            for k in range(ib):
                th = thb_ref[lt, hd * n_il + i0 + k][None]
                e1 = e1b_ref[lt, hd * n_il + i0 + k][None]
                t = jnp.where(bj >= th, e1 * e2j, 0.0)
                gates[k] = t if gates[k] is None else gates[k] + t
        lpm = mq // V7X_LANES
        wl = pl.ds(_mult((lt % lpm) * V7X_LANES, V7X_LANES), V7X_LANES)
        for k in range(ib):
            rows = pl.ds((i0 + k) * nk + j0, jb)
            act = _gelu2(uf_ref[lt, rows, :]) * gates[k].reshape(jb, V7X_LANES)
            w_ref[lt // lpm, rows, wl] = act.astype(BF16)

    def k_major(ref, r0):
        return jnp.concatenate([ref[k, pl.ds(r0, mq), :] for k in range(ref.shape[0])], axis=1)

    def next_u(it, oth):
        r0 = _mult((it // 2) * mq, mq)
        res = jnp.dot(k_major(u_ref, r0), h_ref[it % 2], preferred_element_type=F32)
        for k in range(mq // V7X_LANES):
            uf_refs[oth][(it % 2) * (mq // V7X_LANES) + k, pl.ds(r0, mq), :] = (
                res[:, k * V7X_LANES:(k + 1) * V7X_LANES])

    def prev_out(it, oth):
        r0 = _mult((it // 2) * mq, mq)
        acc_ref[it % 2, pl.ds(r0, mq), :] += jnp.dot(
            k_major(vt_ref, r0), w_refs[oth][it % 2], preferred_element_type=F32)

    n_it = (PEER_ECH // mq) * (PEER_TOK // mq)
    per_it = (PEER_TOK // V7X_LANES) * n_ib * n_jb // n_it

    def run_chunk(with_prev, cur):
        for lt in range(PEER_TOK // V7X_LANES):
            tl = slice(lt * V7X_LANES, (lt + 1) * V7X_LANES)
            for hd in range(PEER_HEADS):
                for il in range(n_il):
                    thb_ref[lt, hd * n_il + il] = jnp.broadcast_to(a_ref[hd, il:il + 1, tl], (sub, V7X_LANES))
                    e1b_ref[lt, hd * n_il + il] = jnp.broadcast_to(e1_ref[hd, il:il + 1, tl], (sub, V7X_LANES))

        lpm = mq // V7X_LANES

        def step(it, carry):
            for k in range(per_it):
                gate_block(it * per_it + k, cur)
            next_u(it, 1 - cur)
            if with_prev:
                prev_out(it, 1 - cur)
            elif it < PEER_ECH // mq:
                r0 = it * mq
                res = jnp.dot(k_major(u0_ref, r0), h_ref[1], preferred_element_type=F32)
                for k in range(lpm):
                    uf0_ref[lpm + k, pl.ds(r0, mq), :] = res[:, k * V7X_LANES:(k + 1) * V7X_LANES]
            return carry

        for it in range(n_it):
            step(it, 0)

    @pl.when(e == 0)
    def _():
        acc_ref[...] = jnp.zeros_like(acc_ref)
        u_all = jnp.concatenate([u0_ref[k] for k in range(u0_ref.shape[0])], axis=1)
        u_first = jnp.dot(u_all, h_ref[0], preferred_element_type=F32)
        for k in range(mq // V7X_LANES):
            uf0_ref[k] = u_first[:, k * V7X_LANES:(k + 1) * V7X_LANES]
        run_chunk(False, 0)

    for par in range(2):
        @pl.when(jnp.logical_and(jnp.logical_and(e > 0, e < n_ch), e % 2 == par))
        def _():
            run_chunk(True, par)

    @pl.when(e == n_ch)
    def _():
        vt_all = jnp.concatenate([vt_ref[k] for k in range(vt_ref.shape[0])], axis=1)
        for half, gref in enumerate((g0_ref, g1_ref)):
            y = acc_ref[half] + jnp.dot(vt_all, w_refs[(n_ch - 1) % 2][half], preferred_element_type=F32)
            sl = slice(half * ROW_TILE, (half + 1) * ROW_TILE)
            xn = x_ref[sl, :] + gref[0, 0] * y.T
            if final:
                xn = xn * lax.rsqrt(jnp.mean(xn * xn, axis=-1, keepdims=True) + EPS) * fg_ref[...]
            o_ref[sl, :] = xn


def peer_experts(h, u_tab, vt_tab, a, e1, bm, e2, x, gate, fg, *, final, t0=0):
    b, s, d = x.shape
    nt = s // ROW_TILE
    m = b * s
    n_exp = u_tab.shape[1]
    i_per = PEER_ECH // PEER_NKEYS

    def gidx(half):
        def f(i, e):
            t = 2 * i + half
            return (t // nt, jnp.minimum(t % nt + t0, 1), 0, 0)
        return f

    n_ch = n_exp // PEER_ECH
    last = n_ch - 1
    mq = V7X_MXU_DIM
    n_lt = PEER_TOK // V7X_LANES
    assert mq == ROW_TILE and PEER_TOK == 2 * mq
    out = pl.pallas_call(
        functools.partial(_expert_body, final=final, n_ch=n_ch),
        grid=(m // PEER_TOK, n_ch + 1),
        in_specs=[
            pl.BlockSpec((PEER_TOK // mq, d, mq), lambda i, e: (i, 0, 0)),
            pl.BlockSpec((d // mq, PEER_ECH, mq), lambda i, e: (0, 0, 0)),
            pl.BlockSpec((d // mq, PEER_ECH, mq), lambda i, e: (0, jnp.minimum(e + 1, last), 0)),
            pl.BlockSpec((PEER_ECH // mq, d, mq), lambda i, e: (jnp.maximum(e - 1, 0), 0, 0)),
            pl.BlockSpec((PEER_HEADS, i_per, PEER_TOK), lambda i, e: (0, jnp.minimum(e, last), i)),
            pl.BlockSpec((PEER_HEADS, i_per, PEER_TOK), lambda i, e: (0, jnp.minimum(e, last), i)),
            pl.BlockSpec((PEER_HEADS, PEER_TOK // V7X_LANES, PEER_NKEYS, V7X_LANES), lambda i, e: (0, i, 0, 0)),
            pl.BlockSpec((PEER_HEADS, PEER_TOK // V7X_LANES, PEER_NKEYS, V7X_LANES), lambda i, e: (0, i, 0, 0)),
            pl.BlockSpec((PEER_TOK, d), lambda i, e: (i, 0)),
            pl.BlockSpec((1, 1, 1, d), gidx(0)),
            pl.BlockSpec((1, 1, 1, d), gidx(1)),
            pl.BlockSpec((1, d), lambda i, e: (0, 0)),
        ],
        out_specs=pl.BlockSpec((PEER_TOK, d), lambda i, e: (i, 0)),
        out_shape=jax.ShapeDtypeStruct((m, d), F32),
        scratch_shapes=[pltpu.VMEM((PEER_TOK // mq, d, mq), F32),
                        pltpu.VMEM((PEER_TOK // mq, PEER_ECH, mq), BF16),
                        pltpu.VMEM((PEER_TOK // mq, PEER_ECH, mq), BF16),
                        pltpu.VMEM((n_lt, PEER_ECH, V7X_LANES), F32),
                        pltpu.VMEM((n_lt, PEER_ECH, V7X_LANES), F32),
                        pltpu.VMEM((n_lt, PEER_HEADS * i_per, TILE_ROWS, V7X_LANES), F32),
                        pltpu.VMEM((n_lt, PEER_HEADS * i_per, TILE_ROWS, V7X_LANES), F32)],
        compiler_params=_cparams(("parallel", "arbitrary")),
        name="peer_experts",
    )(h, u_tab, u_tab, vt_tab, a, e1, bm, e2, x.reshape(m, d), gate, gate, fg)
    return out.reshape(b, s, d)


def _axial_tables(rows, rope_dim, ctx_len, lane0):
    n_freq = rope_dim // 4
    half = rope_dim // 2
    inv_freq = ROPE_THETA ** (-jnp.arange(n_freq, dtype=F32) / n_freq)
    r = jnp.repeat(jnp.arange(rows, dtype=F32), GRID_W)
    col = jnp.tile(jnp.arange(GRID_W, dtype=F32), rows)
    ang = jnp.concatenate([r[:, None] * inv_freq, col[:, None] * inv_freq], axis=-1)
    cos, sin = jnp.cos(ang), jnp.sin(ang)
    n = cos.shape[0]
    period = 64 if rope_dim == 64 else V7X_LANES
    c_blk = jnp.ones((n, period), F32).at[:, lane0:lane0 + rope_dim].set(jnp.concatenate([cos, cos], -1))
    s1_blk = jnp.zeros((n, period), F32).at[:, lane0:lane0 + half].set(-sin)
    s2_blk = jnp.zeros((n, period), F32).at[:, lane0 + half:lane0 + rope_dim].set(sin)
    reps = V7X_LANES // period
    out = []
    for blk, fill in ((c_blk, 1.0), (s1_blk, 0.0), (s2_blk, 0.0)):
        t = jnp.tile(blk, (1, reps))
        out.append(jnp.concatenate([jnp.full((ctx_len, V7X_LANES), fill, F32), t], axis=0))
    return out


def _block_diag(w):
    n, a, b = w.shape
    eye = jnp.eye(n, dtype=w.dtype)
    return (eye[:, None, :, None] * w[:, :, None, :]).reshape(n * a, n * b)


def kernel(x, c, ctx, c_ctx, w_mod, b_mod, norm1_g, norm2_g, w_in, conv_w, conv_b, lru_wa, lru_ba,
           lru_wi, lru_bi, lru_lambda, diff_lam, diff_subln_g, gqa_qnorm_g, gqa_knorm_g, mla_qnorm_g,
           mla_w_uq, mla_kvnorm_g, mla_w_ukv, w_branch, w_out, peer_wq, peer_keys, peer_u, peer_v,
           final_norm_g):
    bsz, seq, d = x.shape
    ctx_len = ctx.shape[1]
    depth = w_in.shape[0]
    assert ctx_len == ROW_TILE and seq % PEER_TOK == 0 and seq % GRID_W == 0
    rows = seq // GRID_W
    s_all = ctx_len + seq
    xs = jnp.concatenate([ctx, x], axis=1)

    tabs = _axial_tables(rows, DIFF_DK, ctx_len, 0) + _axial_tables(rows, MLA_ROPE, ctx_len, MLA_NOPE)
    grp = jnp.arange(BRANCH_W) // GQA_DH
    avg = (grp[:, None] == grp[None, :]).astype(BF16) * (1.0 / GQA_DH)
    sc_in = jnp.zeros((16, d), F32).at[:bsz].set(jax.nn.silu(c)).at[bsz].set(jax.nn.silu(c_ctx)).astype(BF16)
    zero1 = jnp.zeros((1,), F32)
    ones_g = jnp.ones((1, V7X_LANES), F32)

    for l in range(depth):
        lam_init = 0.8 - 0.6 * math.exp(-0.3 * l)
        mod_all = matmul(sc_in, w_mod[l].astype(BF16)) + b_mod[l]
        mod_b = mod_all[:bsz].reshape(bsz, N_MOD, d)
        mod_c = jnp.broadcast_to(mod_all[bsz].reshape(1, N_MOD, d), (bsz, N_MOD, d))
        mods = [jnp.stack([mod_c[:, k], mod_b[:, k]], axis=1)[:, :, None, :] for k in range(N_MOD)]

        wl = w_in[l]
        o = 0
        parts = []
        for w in (512, 512, 512, 512, 512, 512, 128, 128, 384, 256, 32, 4096):
            parts.append(wl[:, o:o + w])
            o += w
        xa, ya, qb, kb, vb, qg, kg, vg, cq, ckv, kr, zg = parts
        kr_blk = jnp.zeros((d, V7X_LANES), F32).at[:, MLA_NOPE:MLA_NOPE + MLA_ROPE].set(kr)
        w_cat = jnp.concatenate([zg, xa, ya, qb, kb, vb, qg, kg, vg, cq, ckv, kr_blk], axis=1).astype(BF16)

        z = adaln_matmul(xs, norm1_g[l][None], mods[0], mods[1], w_cat, ctx_len)

        negc = (-LRU_C * jax.nn.softplus(-lru_lambda[l]))[:, None, None, :]
        wg = jnp.stack([jnp.concatenate([_block_diag(lru_wa[l, dd]), _block_diag(lru_wi[l, dd])], axis=1)
                        for dd in range(2)]).astype(BF16)[:, None]
        bg = jnp.stack([jnp.concatenate([lru_ba[l, dd], lru_bi[l, dd]]) for dd in range(2)])[:, None, None, :]
        cb = conv_b[l][None]
        hf = lru_scan(z, 8, 9, conv_w[l], cb, wg[0], bg[0], negc[0], None, reverse=False)
        oa = lru_scan(z, 8, 9, conv_w[l], cb, wg[1], bg[1], negc[1], hf, reverse=True)

        uq = mla_w_uq[l].reshape(MLA_Q_RANK, MLA_HEADS, MLA_NOPE + MLA_ROPE)
        wuq = jnp.pad(uq, ((0, 0), (0, 0), (0, V7X_LANES - MLA_NOPE - MLA_ROPE))).reshape(MLA_Q_RANK, -1)
        ukv = mla_w_ukv[l].reshape(MLA_KV_RANK, MLA_HEADS, MLA_NOPE + MLA_DV)
        wuk = jnp.pad(ukv[:, :, :MLA_NOPE], ((0, 0), (0, 0), (0, V7X_LANES - MLA_NOPE))).reshape(MLA_KV_RANK, -1)
        wuv = ukv[:, :, MLA_NOPE:].reshape(MLA_KV_RANK, -1)
        qd, kd, vd, qq, kq, vq, qm, km, vm = attn_prep(
            z, tabs, avg, jnp.tile(gqa_qnorm_g[l], GQA_HEADS)[None], jnp.tile(gqa_knorm_g[l], GQA_KV_HEADS)[None],
            mla_qnorm_g[l][None], mla_kvnorm_g[l][None], wuq.astype(BF16), wuk.astype(BF16), wuv.astype(BF16))
        lv = diff_lam[l]
        lam = (jnp.exp(jnp.sum(lv[0] * lv[1])) - jnp.exp(jnp.sum(lv[2] * lv[3])) + lam_init).reshape(1)
        t0 = 1 if l == depth - 1 else 0
        ob = attention(qd, kd, vd, lam, diff_subln_g[l][None], mode="diff", out_scale=1.0 - lam_init, t0=t0)
        og = attention(qq, kq, vq, zero1, ones_g, mode="pair", t0=t0)
        om = attention(qm, km, vm, zero1, ones_g, mode="mla", t0=t0)

        xs = merge(oa, ob, og, om, z, w_branch[l].astype(BF16), w_out[l].astype(BF16), xs, mods[2], t0=t0)

        kbig = jnp.stack([_block_diag(peer_keys[l, hd]) for hd in range(PEER_HEADS)]).astype(BF16)
        h2, pa, pb, pe1, pe2 = peer_route(xs, norm2_g[l][None], mods[3], mods[4], peer_wq[l].astype(BF16), kbig,
                                          t0=t0)
        n_exp = peer_u.shape[1]
        u_km = peer_u[l].astype(BF16).reshape(n_exp, d // V7X_MXU_DIM, V7X_MXU_DIM).transpose(1, 0, 2)
        vt_km = peer_v[l].astype(BF16).reshape(n_exp // V7X_MXU_DIM, V7X_MXU_DIM, d).transpose(0, 2, 1)
        xs = peer_experts(h2, u_km, vt_km, pa, pe1, pb, pe2, xs, mods[5],
                          final_norm_g[None], final=(l == depth - 1), t0=t0)
    return xs
```

```python
import functools
import math

import jax
import jax.numpy as jnp
from jax import lax
from jax.experimental import pallas as pl
from jax.experimental.pallas import tpu as pltpu

F32 = jnp.float32
BF16 = jnp.bfloat16

EPS = 1e-6
GRID_W = 64
ROPE_THETA = 10000.0
N_MOD = 6

D_RNN = 512
CONV_W = 4
LRU_C = 8.0

DIFF_DK = 64
GQA_HEADS = 8
GQA_KV_HEADS = 2
GQA_DH = 64
MLA_HEADS = 8
MLA_NOPE = 64
MLA_ROPE = 32
MLA_DV = 64
MLA_Q_RANK = 384
MLA_KV_RANK = 256
N_BRANCH = 4
BRANCH_W = 512

PEER_HEADS = 8
PEER_NKEYS = 128
PEER_DK = 128
PEER_TOPK = 16

V7X_LANES = 128
V7X_SUBLANES = 8
V7X_VMEM_BYTES = 64 * 1024 * 1024
V7X_MXU_DIM = 256
VMEM_LIMIT = V7X_VMEM_BYTES * 3 // 4

ROW_TILE = 256


def _cparams(sem):
    return pltpu.CompilerParams(dimension_semantics=sem, vmem_limit_bytes=VMEM_LIMIT)


def _mm_body(a_ref, w_ref, o_ref):
    o_ref[...] = jnp.dot(a_ref[...], w_ref[...], preferred_element_type=F32).astype(o_ref.dtype)


def matmul(a, w, out_dtype=F32, tm=512, tn=512):
    m, k = a.shape
    _, n = w.shape
    tm = math.gcd(tm, m)
    tn = math.gcd(tn, n)
    return pl.pallas_call(
        _mm_body,
        grid=(n // tn, m // tm),
        in_specs=[
            pl.BlockSpec((tm, k), lambda j, i: (i, 0)),
            pl.BlockSpec((k, tn), lambda j, i: (0, j)),
        ],
        out_specs=pl.BlockSpec((tm, tn), lambda j, i: (i, j)),
        out_shape=jax.ShapeDtypeStruct((m, n), out_dtype),
        compiler_params=_cparams(("parallel", "parallel")),
        name="matmul",
    )(a, w)


def _adaln_mm_body(x_ref, g_ref, sh_ref, sc_ref, w_ref, o_ref, h_ref, *, ctx_len):
    tm = x_ref.shape[1]

    @pl.when(pl.program_id(2) == 0)
    def _():
        x = x_ref[0]
        y = x * lax.rsqrt(jnp.mean(x * x, axis=-1, keepdims=True) + EPS) * g_ref[...]
        row = pl.program_id(1) * tm + lax.broadcasted_iota(jnp.int32, (tm, 1), 0)
        is_ctx = row < ctx_len
        scale = jnp.where(is_ctx, sc_ref[0, 0], sc_ref[0, 1])
        shift = jnp.where(is_ctx, sh_ref[0, 0], sh_ref[0, 1])
        h_ref[...] = (y * (1.0 + scale) + shift).astype(BF16)

    o_ref[0] = jnp.dot(h_ref[...], w_ref[...], preferred_element_type=F32).astype(o_ref.dtype)


def adaln_matmul(x, g, shift, scale, w, ctx_len, n_row_tiles=4, tn=1024):
    b, s, d = x.shape
    n = w.shape[1]
    tm = s // n_row_tiles
    assert tm * n_row_tiles == s and tm % (2 * V7X_SUBLANES) == 0 and n % tn == 0
    mod = pl.BlockSpec((1, 2, 1, d), lambda bi, ti, j: (bi, 0, 0, 0))
    return pl.pallas_call(
        functools.partial(_adaln_mm_body, ctx_len=ctx_len),
        grid=(b, n_row_tiles, n // tn),
        in_specs=[
            pl.BlockSpec((1, tm, d), lambda bi, ti, j: (bi, ti, 0)),
            pl.BlockSpec((1, d), lambda bi, ti, j: (0, 0)),
            mod, mod,
            pl.BlockSpec((d, tn), lambda bi, ti, j: (0, j)),
        ],
        out_specs=pl.BlockSpec((1, tm, tn), lambda bi, ti, j: (bi, ti, j)),
        out_shape=jax.ShapeDtypeStruct((b, s, n), BF16),
        scratch_shapes=[pltpu.VMEM((tm, d), BF16)],
        compiler_params=_cparams(("parallel", "parallel", "arbitrary")),
        name="adaln_matmul",
    )(x, g, shift, scale, w)


def _mult(x, m):
    return x if isinstance(x, int) else pl.multiple_of(x, m)


def _gelu2(x):
    c = math.sqrt(2.0 / math.pi)
    return x * (1.0 + jnp.tanh(x * (c + (0.044715 * c) * (x * x))))


def _gelu(x):
    return 0.5 * _gelu2(x)


def _sigmoid(x):
    return 0.5 + 0.5 * jnp.tanh(0.5 * x)


TILE_ROWS = V7X_SUBLANES
HALO_ROWS = 2 * V7X_SUBLANES


def _lru_body(x_ref, xp_ref, xn_ref, cw_ref, cb_ref, wg_ref, bg_ref, nc_ref, *rest,
              reverse, n_chunks):
    if reverse:
        ya_ref, hf_ref, o_ref, carry_ref = rest
    else:
        o_ref, carry_ref = rest
    step = pl.program_id(1)
    if reverse:
        chunk = jnp.where(step == 0, 0, n_chunks - step)
    else:
        chunk = step
    rows = ROW_TILE
    c = x_ref.shape[-1]

    @pl.when(step == 0)
    def _():
        carry_ref[...] = jnp.zeros_like(carry_ref)

    x = x_ref[0].astype(F32)
    xx = jnp.concatenate([xp_ref[0].astype(F32), x, xn_ref[0].astype(F32)], axis=0)
    n_xx = rows + 2 * HALO_ROWS
    row = lax.broadcasted_iota(jnp.int32, (rows, c), 0)
    seg_start = jnp.logical_or(chunk == 0, chunk == 1)
    seg_end = jnp.logical_or(chunk == 0, chunk == n_chunks - 1)

    def shifted(d):
        return pltpu.roll(xx, (n_xx - d) % n_xx, axis=0)[HALO_ROWS:HALO_ROWS + rows]

    x_m1 = jnp.where(jnp.logical_and(seg_start, row == 0), 0.0, shifted(-1))
    x_p1 = jnp.where(jnp.logical_and(seg_end, row >= rows - 1), 0.0, shifted(1))
    x_p2 = jnp.where(jnp.logical_and(seg_end, row >= rows - 2), 0.0, shifted(2))
    cw = cw_ref[...]
    u = cw[0:1] * x_m1 + cw[1:2] * x + cw[2:3] * x_p1 + cw[3:4] * x_p2 + cb_ref[...]

    z = jnp.dot(u.astype(BF16), wg_ref[0], preferred_element_type=F32) + bg_ref[0]
    r = _sigmoid(z[:, :c])
    gi = _sigmoid(z[:, c:])
    a = jnp.exp(nc_ref[0] * r)
    bv = jnp.sqrt(1.0 - a * a) * (gi * u)

    sub = row % TILE_ROWS
    for dstep in (1, 2, 4):
        if reverse:
            sh = (rows - dstep) % rows
            keep = sub <= TILE_ROWS - 1 - dstep
        else:
            sh = dstep
            keep = sub >= dstep
        a_sh = jnp.where(keep, pltpu.roll(a, sh, axis=0), 1.0)
        b_sh = jnp.where(keep, pltpu.roll(bv, sh, axis=0), 0.0)
        bv = a * b_sh + bv
        a = a * a_sh

    carry = carry_ref[...]
    n_tiles = rows // TILE_ROWS
    order = range(n_tiles - 1, -1, -1) if reverse else range(n_tiles)
    hs = [None] * n_tiles
    for t in order:
        sl = slice(t * TILE_ROWS, (t + 1) * TILE_ROWS)
        h = bv[sl] + a[sl] * carry
        hs[t] = h
        edge = h[0:1] if reverse else h[TILE_ROWS - 1:TILE_ROWS]
        carry = jnp.broadcast_to(edge, (TILE_ROWS, c))
    carry_ref[...] = carry
    h_all = jnp.concatenate(hs, axis=0)
    if reverse:
        o_ref[0] = (_gelu(ya_ref[0].astype(F32)) * (hf_ref[0] + h_all)).astype(o_ref.dtype)
    else:
        o_ref[0] = h_all


def lru_scan(z, xa_col, ya_col, conv_w, conv_b, wg, bg, negc, hf, *, reverse):
    b, s, _ = z.shape
    c = D_RNN
    n_chunks = s // ROW_TILE
    per = ROW_TILE // HALO_ROWS
    n8 = s // HALO_ROWS
    if reverse:
        cidx = lambda st: jnp.where(st == 0, 0, n_chunks - st)
    else:
        cidx = lambda st: st
    in_specs = [
        pl.BlockSpec((1, ROW_TILE, c), lambda bi, st: (bi, cidx(st), xa_col)),
        pl.BlockSpec((1, HALO_ROWS, c), lambda bi, st: (bi, jnp.maximum(cidx(st) * per - 1, 0), xa_col)),
        pl.BlockSpec((1, HALO_ROWS, c), lambda bi, st: (bi, jnp.minimum((cidx(st) + 1) * per, n8 - 1), xa_col)),
        pl.BlockSpec((CONV_W, c), lambda bi, st: (0, 0)),
        pl.BlockSpec((1, c), lambda bi, st: (0, 0)),
        pl.BlockSpec((1, c, 2 * c), lambda bi, st: (0, 0, 0)),
        pl.BlockSpec((1, 1, 2 * c), lambda bi, st: (0, 0, 0)),
        pl.BlockSpec((1, 1, c), lambda bi, st: (0, 0, 0)),
    ]
    args = [z, z, z, conv_w, conv_b, wg, bg, negc]
    if reverse:
        in_specs += [
            pl.BlockSpec((1, ROW_TILE, c), lambda bi, st: (bi, cidx(st), ya_col)),
            pl.BlockSpec((1, ROW_TILE, c), lambda bi, st: (bi, cidx(st), 0)),
        ]
        args += [z, hf]
    return pl.pallas_call(
        functools.partial(_lru_body, reverse=reverse, n_chunks=n_chunks),
        grid=(b, n_chunks),
        in_specs=in_specs,
        out_specs=pl.BlockSpec((1, ROW_TILE, c), lambda bi, st: (bi, cidx(st), 0)),
        out_shape=jax.ShapeDtypeStruct((b, s, c), BF16 if reverse else F32),
        scratch_shapes=[pltpu.VMEM((TILE_ROWS, c), F32)],
        compiler_params=_cparams(("parallel", "arbitrary")),
        name="lru_rev" if reverse else "lru_fwd",
    )(*args)


def _tile_lanes(t, n):
    return jnp.concatenate([t] * n, axis=1) if n > 1 else t


def _rope(x, c, s1, s2, r):
    w = x.shape[-1]
    n = w // V7X_LANES
    return (x * _tile_lanes(c, n) + pltpu.roll(x, w - r, axis=1) * _tile_lanes(s1, n)
            + pltpu.roll(x, r, axis=1) * _tile_lanes(s2, n))


def _split_dot(x, m_ref):
    hi = x.astype(BF16)
    lo = (x - hi.astype(F32)).astype(BF16)
    m = m_ref[...]
    return jnp.dot(hi, m, preferred_element_type=F32) + jnp.dot(lo, m, preferred_element_type=F32)


def _with_ones(v):
    ones = jnp.ones((v.shape[0], V7X_LANES), v.dtype)
    parts = []
    for c in range(v.shape[1] // V7X_LANES):
        parts += [v[:, c * V7X_LANES:(c + 1) * V7X_LANES], ones]
    return jnp.concatenate(parts, axis=1)


def _rms(x, g):
    return x * lax.rsqrt(jnp.mean(x * x, axis=-1, keepdims=True) + EPS) * g


def _prep_body(qb_ref, kb_ref, vb_ref, qg_ref, tail_ref,
               c64_ref, s164_ref, s264_ref, cm_ref, s1m_ref, s2m_ref,
               avg_ref, gq_ref, gk_ref, gcq_ref, gckv_ref, wuq_ref, wuk_ref, wuv_ref,
               qd_ref, kd_ref, vd_ref, qq_ref, kq_ref, vq_ref, qm_ref, km_ref, vm_ref):
    c64, s164, s264 = c64_ref[...], s164_ref[...], s264_ref[...]
    cm, s1m, s2m = cm_ref[...], s1m_ref[...], s2m_ref[...]
    half64 = DIFF_DK // 2
    qd_ref[0] = (_rope(qb_ref[0].astype(F32), c64, s164, s264, half64) * (DIFF_DK ** -0.5)).astype(BF16)
    kd_ref[0] = _rope(kb_ref[0].astype(F32), c64, s164, s264, half64).astype(BF16)
    vd_ref[0] = _with_ones(vb_ref[0].astype(BF16))
    qg = qg_ref[0].astype(F32)
    ms = _split_dot(qg * qg, avg_ref)
    qn = qg * lax.rsqrt(ms + EPS) * gq_ref[...]
    qq_ref[0] = (_rope(qn, c64, s164, s264, GQA_DH // 2) * (GQA_DH ** -0.5)).astype(BF16)
    tail = tail_ref[0].astype(F32)
    kg = tail[:, 0:128]
    msk = _split_dot(kg * kg, avg_ref.at[0:128, 0:128])
    kn = _rope(kg * lax.rsqrt(msk + EPS) * gk_ref[...], c64, s164, s264, GQA_DH // 2).astype(BF16)
    vg = tail[:, 128:256].astype(BF16)
    kq_ref[0] = jnp.concatenate([kn[:, 0:64], kn[:, 0:64], kn[:, 64:128], kn[:, 64:128]], axis=1)
    vq_ref[0] = _with_ones(jnp.concatenate([vg[:, 0:64], vg[:, 0:64], vg[:, 64:128], vg[:, 64:128]], axis=1))
    cq = tail[:, 256:256 + MLA_Q_RANK]
    ckv = tail[:, 640:640 + MLA_KV_RANK]
    kr = tail[:, 896:1024]
    qf = jnp.dot(_rms(cq, gcq_ref[...]).astype(BF16), wuq_ref[...], preferred_element_type=F32)
    scale = (MLA_NOPE + MLA_ROPE) ** -0.5
    qm_ref[0] = (_rope(qf, cm, s1m, s2m, MLA_ROPE // 2) * scale).astype(BF16)
    ckvn = _rms(ckv, gckv_ref[...]).astype(BF16)
    kf = jnp.dot(ckvn, wuk_ref[...], preferred_element_type=F32)
    krr = _rope(kr, cm, s1m, s2m, MLA_ROPE // 2)
    km_ref[0] = (kf + _tile_lanes(krr, MLA_HEADS)).astype(BF16)
    vm_ref[0] = _with_ones(jnp.dot(ckvn, wuv_ref[...], preferred_element_type=F32).astype(BF16))


def attn_prep(z, tabs, avg, gq, gk, gcq, gckv, wuq, wuk, wuv):
    b, s, _ = z.shape
    nt = s // ROW_TILE
    zspec = lambda w, idx: pl.BlockSpec((1, ROW_TILE, w), lambda ti, bi: (bi, ti, idx))
    tab = pl.BlockSpec((ROW_TILE, V7X_LANES), lambda ti, bi: (ti, 0))
    full = lambda a: pl.BlockSpec(a.shape, lambda ti, bi: (0,) * a.ndim)
    ospec = lambda w: pl.BlockSpec((1, ROW_TILE, w), lambda ti, bi: (bi, ti, 0))
    oshape = lambda w: jax.ShapeDtypeStruct((b, s, w), BF16)
    widths = (512, 512, 1024, 512, 256, 512, 1024, 1024, 1024)
    return pl.pallas_call(
        _prep_body,
        grid=(nt, b),
        in_specs=[zspec(512, 10), zspec(512, 11), zspec(512, 12), zspec(512, 13), zspec(1024, 7)]
        + [tab] * 6 + [full(a) for a in (avg, gq, gk, gcq, gckv, wuq, wuk, wuv)],
        out_specs=[ospec(w) for w in widths],
        out_shape=[oshape(w) for w in widths],
        compiler_params=_cparams(("parallel", "parallel")),
        name="attn_prep",
    )(z, z, z, z, z, *tabs, avg, gq, gk, gcq, gckv, wuq, wuk, wuv)


ATTN_GROUPS = {"pair": 2, "diff": 2, "mla": 2}


def _attn_body(lam_ref, q_ref, k_ref, v_ref, g_ref, o_ref, *, mode, out_scale, t0, groups):
    ti = pl.program_id(2)
    lane = lax.broadcasted_iota(jnp.int32, (ROW_TILE, V7X_LANES), 1)
    lo_half = lane < (V7X_LANES // 2)

    ln = V7X_LANES

    def run(n_keys):
        for grp in range(groups):
            outs = []
            for j in range(2):
                if mode == "mla":
                    c0 = (2 * grp + j) * ln
                    qj = q_ref[0, :, c0:c0 + ln]
                    kj = k_ref[0, 0:n_keys, c0:c0 + ln]
                else:
                    q = q_ref[0, :, grp * ln:(grp + 1) * ln]
                    keep = lo_half if j == 0 else jnp.logical_not(lo_half)
                    qj = jnp.where(keep, q.astype(F32), 0.0).astype(BF16)
                    kc = (grp // 2 if mode == "pair" else grp) * ln
                    kj = k_ref[0, 0:n_keys, kc:kc + ln]
                vc = (grp // 2 if mode == "pair" else grp) * 2 * ln
                hk = (n_keys // (2 * ln)) * ln
                tn = (((1,), (1,)), ((), ()))
                s = jnp.concatenate(
                    [lax.dot_general(qj, kj[0:hk], tn, preferred_element_type=F32).astype(BF16),
                     lax.dot_general(qj, kj[hk:], tn, preferred_element_type=F32).astype(BF16)], axis=1)
                m = jnp.max(s, axis=-1, keepdims=True)
                p = jnp.exp(s - m)
                o = (jnp.dot(p[:, 0:hk], v_ref[0, 0:hk, vc:vc + 2 * ln], preferred_element_type=F32)
                     + jnp.dot(p[:, hk:], v_ref[0, hk:n_keys, vc:vc + 2 * ln], preferred_element_type=F32))
                outs.append(o[:, 0:ln] / o[:, ln:2 * ln])
            if mode == "diff":
                o = outs[0] - lam_ref[0] * outs[1]
                o = o * lax.rsqrt(jnp.mean(o * o, axis=-1, keepdims=True) + EPS) * g_ref[...] * out_scale
            else:
                o = jnp.where(lo_half, outs[0], outs[1])
            o_ref[0, :, grp * ln:(grp + 1) * ln] = o.astype(o_ref.dtype)

    if t0 == 0:
        @pl.when(ti == 0)
        def _():
            run(ROW_TILE)

        @pl.when(ti > 0)
        def _():
            run(k_ref.shape[1])
    else:
        run(k_ref.shape[1])


def attention(q, k, v, lam, g, *, mode, out_scale=1.0, t0=0):
    b, s, wq = q.shape
    groups = ATTN_GROUPS[mode]
    steps = 4 // groups
    qw = wq // steps
    kw = k.shape[-1] // steps
    vw = v.shape[-1] // steps
    ow = groups * V7X_LANES
    nt = s // ROW_TILE - t0
    return pl.pallas_call(
        functools.partial(_attn_body, mode=mode, out_scale=out_scale, t0=t0, groups=groups),
        grid=(b, steps, nt),
        in_specs=[
            pl.BlockSpec(memory_space=pltpu.SMEM),
            pl.BlockSpec((1, ROW_TILE, qw), lambda bi, gi, ti: (bi, ti + t0, gi)),
            pl.BlockSpec((1, s, kw), lambda bi, gi, ti: (bi, 0, gi)),
            pl.BlockSpec((1, s, vw), lambda bi, gi, ti: (bi, 0, gi)),
            pl.BlockSpec((1, V7X_LANES), lambda bi, gi, ti: (0, 0)),
        ],
        out_specs=pl.BlockSpec((1, ROW_TILE, ow), lambda bi, gi, ti: (bi, ti, gi)),
        out_shape=jax.ShapeDtypeStruct((b, nt * ROW_TILE, 4 * V7X_LANES), BF16),
        compiler_params=_cparams(("parallel", "parallel", "arbitrary")),
        name="attn_" + mode,
    )(lam, q, k, v, g)


def _merge_body(oa_ref, ob_ref, og_ref, om_ref, zg_ref, wb_ref, wo_ref, x_ref, gate_ref, o_ref):
    d = x_ref.shape[-1]
    merged = None
    for k, o_k in enumerate((oa_ref, ob_ref, og_ref, om_ref)):
        t = jnp.dot(o_k[0], wb_ref[k], preferred_element_type=F32)
        t = t * _sigmoid(zg_ref[0, :, k * d:(k + 1) * d].astype(F32))
        merged = t if merged is None else merged + t
    y = jnp.dot(merged.astype(BF16), wo_ref[...], preferred_element_type=F32)
    o_ref[0] = x_ref[0] + gate_ref[0, 0] * y


def merge(oa, ob, og, om, z, wb, wo, x, gate, t0=0):
    b, s, d = x.shape
    nt = s // ROW_TILE - t0
    full = lambda w: pl.BlockSpec((1, ROW_TILE, w), lambda bi, ti: (bi, ti + t0, 0))
    part = pl.BlockSpec((1, ROW_TILE, BRANCH_W), lambda bi, ti: (bi, ti, 0))
    return pl.pallas_call(
        _merge_body,
        grid=(b, nt),
        in_specs=[full(BRANCH_W), part, part, part,
                  full(N_BRANCH * d),
                  pl.BlockSpec(wb.shape, lambda bi, ti: (0, 0, 0)),
                  pl.BlockSpec(wo.shape, lambda bi, ti: (0, 0)),
                  full(d),
                  pl.BlockSpec((1, 1, 1, d), lambda bi, ti: (bi, jnp.minimum(ti + t0, 1), 0, 0))],
        out_specs=pl.BlockSpec((1, ROW_TILE, d), lambda bi, ti: (bi, ti, 0)),
        out_shape=jax.ShapeDtypeStruct((b, nt * ROW_TILE, d), F32),
        compiler_params=_cparams(("parallel", "parallel")),
        name="merge",
    )(oa, ob, og, om, z, wb, wo, x, gate)


NEG_BIG = float(jnp.finfo(jnp.float32).min)


def _oddeven_merge(lo, hi, r):
    step = r * 2
    if step < hi - lo:
        yield from _oddeven_merge(lo, hi, step)
        yield from _oddeven_merge(lo + r, hi, step)
        yield from [(i, i + r) for i in range(lo + r, hi - r, step)]
    else:
        yield (lo, lo + r)


def _oddeven_sort(lo, hi):
    if hi - lo >= 1:
        mid = lo + (hi - lo) // 2
        yield from _oddeven_sort(lo, mid)
        yield from _oddeven_sort(mid + 1, hi)
        yield from _oddeven_merge(lo, hi, 1)


def _bitonic_merge(n):
    pairs, s = [], n // 2
    while s >= 1:
        pairs += [(i, i + s) for i in range(n) if (i // s) % 2 == 0]
        s //= 2
    return pairs


_SORT16 = tuple(_oddeven_sort(0, PEER_TOPK - 1))
_BITONIC16 = tuple(_bitonic_merge(PEER_TOPK))


def _route_body(x_ref, g_ref, sh_ref, sc_ref, wq_ref, kb_ref,
                h_ref, a_ref, b_ref, e1_ref, e2_ref, st_ref):
    x = x_ref[0]
    ms = jnp.mean(x * x, axis=-1, keepdims=True)
    hf = x * lax.rsqrt(ms + EPS) * g_ref[...] * (1.0 + sc_ref[0, 0]) + sh_ref[0, 0]
    h_ref[0] = hf.T.astype(BF16)
    h = hf.astype(BF16)
    q = jnp.dot(h, wq_ref[...], preferred_element_type=F32).astype(BF16)
    nk = PEER_NKEYS
    for hd in range(PEER_HEADS):
        st_ref[hd * 2 * nk:(hd + 1) * 2 * nk, :] = lax.dot_general(
            kb_ref[hd], q[:, hd * PEER_DK:(hd + 1) * PEER_DK], (((1,), (1,)), ((), ())),
            preferred_element_type=F32)

    def exchange(t, pairs):
        for i, j in pairs:
            t[i], t[j] = jnp.maximum(t[i], t[j]), jnp.minimum(t[i], t[j])

    def merge_top(a, b):
        t = [jnp.maximum(a[k], b[PEER_TOPK - 1 - k]) for k in range(PEER_TOPK)]
        exchange(t, _BITONIC16)
        return t

    def across_sublanes(t):
        for sh in (4, 2, 1):
            t = merge_top(t, [pltpu.roll(x, sh, axis=0) for x in t])
        return t

    def top16(s):
        t = [s[k * TILE_ROWS:(k + 1) * TILE_ROWS] for k in range(nk // TILE_ROWS)]
        exchange(t, _SORT16)
        return across_sublanes(t)

    row8 = lax.broadcasted_iota(jnp.int32, (TILE_ROWS, V7X_LANES), 0)
    n_q = jnp.zeros((TILE_ROWS, V7X_LANES), jnp.int32)
    for p in range(TILE_ROWS):
        n_q = jnp.where(row8 == p, PEER_TOPK // (p + 1), n_q)

    def rank_rows(t, lo):
        out = t[lo]
        for r in range(1, TILE_ROWS):
            out = jnp.where(row8 == r, t[lo + r], out)
        return out

    def head(it, carry):
        hd = it // (ROW_TILE // V7X_LANES)
        lt = it % (ROW_TILE // V7X_LANES)
        base = pl.multiple_of(hd * (2 * nk), 2 * nk)
        tl = pl.ds(pl.multiple_of(lt * V7X_LANES, V7X_LANES), V7X_LANES)
        s1 = st_ref[pl.ds(base, nk), tl]
        s2 = st_ref[pl.ds(base + nk, nk), tl]
        ta = top16(s1)
        tb = top16(s2)
        lo = rank_rows(ta, 0)
        cand = [jnp.where(n_q > q, lo + tb[q], NEG_BIG) for q in range(PEER_TOPK)]
        cand[PEER_TOPK - 1] = jnp.maximum(cand[PEER_TOPK - 1], rank_rows(ta, TILE_ROWS) + tb[0])
        exchange(cand, _BITONIC16)
        cand = across_sublanes(cand)
        top = cand[0][0:1]
        tau = cand[PEER_TOPK - 1][0:1]
        zsum = jnp.zeros_like(top)
        for k in range(PEER_TOPK):
            zsum = zsum + jnp.exp(cand[k][0:1] - top)
        a_ref[hd, :, tl] = tau - s1
        e1_ref[hd, :, tl] = jnp.exp(s1 - ta[0][0:1]) * (0.5 / zsum)
        b_ref[hd, lt] = s2
        e2_ref[hd, lt] = jnp.exp(s2 - tb[0][0:1])
        return carry

    lax.fori_loop(0, PEER_HEADS * (ROW_TILE // V7X_LANES), head, 0)


def peer_route(x, g, shift, scale, wq, kbig, t0=0):
    b, s, d = x.shape
    nt = s // ROW_TILE
    m = b * s
    seg = lambda bi, ti: (bi, jnp.minimum(ti + t0, 1), 0, 0)
    flat = lambda bi, ti: (0, 0, bi * nt + ti)
    kspec = pl.BlockSpec((PEER_HEADS, PEER_NKEYS, ROW_TILE), flat)
    kshape = jax.ShapeDtypeStruct((PEER_HEADS, PEER_NKEYS, m), F32)
    lt_per = ROW_TILE // V7X_LANES
    tspec = pl.BlockSpec((PEER_HEADS, lt_per, PEER_NKEYS, V7X_LANES), lambda bi, ti: (0, bi * nt + ti, 0, 0))
    tshape = jax.ShapeDtypeStruct((PEER_HEADS, m // V7X_LANES, PEER_NKEYS, V7X_LANES), F32)
    return pl.pallas_call(
        _route_body,
        grid=(b, nt),
        in_specs=[
            pl.BlockSpec((1, ROW_TILE, d), lambda bi, ti: (bi, ti, 0)),
            pl.BlockSpec((1, d), lambda bi, ti: (0, 0)),
            pl.BlockSpec((1, 1, 1, d), seg),
            pl.BlockSpec((1, 1, 1, d), seg),
            pl.BlockSpec(wq.shape, lambda bi, ti: (0, 0)),
            pl.BlockSpec(kbig.shape, lambda bi, ti: (0, 0, 0)),
        ],
        out_specs=[pl.BlockSpec((1, d, ROW_TILE), lambda bi, ti: (bi * nt + ti, 0, 0)),
                   kspec, tspec, kspec, tspec],
        out_shape=[jax.ShapeDtypeStruct((m // ROW_TILE, d, ROW_TILE), BF16), kshape, tshape, kshape, tshape],
        scratch_shapes=[pltpu.VMEM((2 * PEER_HEADS * PEER_NKEYS, ROW_TILE), F32)],
        compiler_params=_cparams(("parallel", "parallel")),
        name="peer_route",
    )(x, g, shift, scale, wq, kbig)


PEER_TOK = 2 * ROW_TILE
PEER_ECH = 1024


def _expert_body(h_ref, u0_ref, u_ref, vt_ref, a_ref, e1_ref, b_ref, e2_ref, x_ref, g0_ref, g1_ref,
                 fg_ref, o_ref, acc_ref, w0_ref, w1_ref, uf0_ref, uf1_ref, thb_ref, e1b_ref, *,
                 final, n_ch):
    e = pl.program_id(1)
    nk = PEER_NKEYS
    n_il = PEER_ECH // nk
    sub = TILE_ROWS
    ib, jb = 8, nk // 4
    n_ib, n_jb = n_il // ib, nk // jb
    mq = V7X_MXU_DIM
    uf_refs = (uf0_ref, uf1_ref)
    w_refs = (w0_ref, w1_ref)

    def gate_block(g, cur):
        uf_ref, w_ref = uf_refs[cur], w_refs[cur]
        lt = g // (n_ib * n_jb)
        i0 = ((g // n_jb) % n_ib) * ib
        j0 = (g % n_jb) * jb
        gates = [None] * ib
        for hd in range(PEER_HEADS):
            bj = b_ref[hd, lt, j0:j0 + jb, :].reshape(jb // sub, sub, V7X_LANES)
            e2j = e2_ref[hd, lt, j0:j0 + jb, :].reshape(jb // sub, sub, V7X_LANES)
            for k in range(ib):
                th = thb_ref[lt, hd * n_il + i0 + k][None]
                e1 = e1b_ref[lt, hd * n_il + i0 + k][None]
                t = jnp.where(bj >= th, e1 * e2j, 0.0)
                gates[k] = t if gates[k] is None else gates[k] + t
        lpm = mq // V7X_LANES
        wl = pl.ds(_mult((lt % lpm) * V7X_LANES, V7X_LANES), V7X_LANES)
        for k in range(ib):
            rows = pl.ds((i0 + k) * nk + j0, jb)
            act = _gelu2(uf_ref[lt, rows, :].astype(BF16)) * gates[k].reshape(jb, V7X_LANES).astype(BF16)
            w_ref[lt // lpm, rows, wl] = act

    def k_major(ref, r0):
        return jnp.concatenate([ref[k, pl.ds(r0, mq), :] for k in range(ref.shape[0])], axis=1)

    def next_u(it, oth):
        r0 = _mult((it // 2) * mq, mq)
        res = jnp.dot(k_major(u_ref, r0), h_ref[it % 2], preferred_element_type=F32)
        for k in range(mq // V7X_LANES):
            uf_refs[oth][(it % 2) * (mq // V7X_LANES) + k, pl.ds(r0, mq), :] = (
                res[:, k * V7X_LANES:(k + 1) * V7X_LANES])

    def prev_out(it, oth):
        r0 = _mult((it // 2) * mq, mq)
        acc_ref[it % 2, pl.ds(r0, mq), :] += jnp.dot(
            k_major(vt_ref, r0), w_refs[oth][it % 2], preferred_element_type=F32)

    n_it = (PEER_ECH // mq) * (PEER_TOK // mq)
    per_it = (PEER_TOK // V7X_LANES) * n_ib * n_jb // n_it

    def run_chunk(with_prev, cur):
        for lt in range(PEER_TOK // V7X_LANES):
            tl = slice(lt * V7X_LANES, (lt + 1) * V7X_LANES)
            for hd in range(PEER_HEADS):
                for il in range(n_il):
                    thb_ref[lt, hd * n_il + il] = jnp.broadcast_to(a_ref[hd, il:il + 1, tl], (sub, V7X_LANES))
                    e1b_ref[lt, hd * n_il + il] = jnp.broadcast_to(e1_ref[hd, il:il + 1, tl], (sub, V7X_LANES))

        lpm = mq // V7X_LANES

        def step(it, carry):
            for k in range(per_it):
                gate_block(it * per_it + k, cur)
            next_u(it, 1 - cur)
            if with_prev:
                prev_out(it, 1 - cur)
            elif it < PEER_ECH // mq:
                r0 = it * mq
                res = jnp.dot(k_major(u0_ref, r0), h_ref[1], preferred_element_type=F32)
                for k in range(lpm):
                    uf0_ref[lpm + k, pl.ds(r0, mq), :] = res[:, k * V7X_LANES:(k + 1) * V7X_LANES]
            return carry

        for it in range(n_it):
            step(it, 0)

    @pl.when(e == 0)
    def _():
        acc_ref[...] = jnp.zeros_like(acc_ref)
        u_all = jnp.concatenate([u0_ref[k] for k in range(u0_ref.shape[0])], axis=1)
        u_first = jnp.dot(u_all, h_ref[0], preferred_element_type=F32)
        for k in range(mq // V7X_LANES):
            uf0_ref[k] = u_first[:, k * V7X_LANES:(k + 1) * V7X_LANES]
        run_chunk(False, 0)

    for par in range(2):
        @pl.when(jnp.logical_and(jnp.logical_and(e > 0, e < n_ch), e % 2 == par))
        def _():
            run_chunk(True, par)

    @pl.when(e == n_ch)
    def _():
        vt_all = jnp.concatenate([vt_ref[k] for k in range(vt_ref.shape[0])], axis=1)
        for half, gref in enumerate((g0_ref, g1_ref)):
            y = acc_ref[half] + jnp.dot(vt_all, w_refs[(n_ch - 1) % 2][half], preferred_element_type=F32)
            sl = slice(half * ROW_TILE, (half + 1) * ROW_TILE)
            xn = x_ref[sl, :] + gref[0, 0] * y.T
            if final:
                xn = xn * lax.rsqrt(jnp.mean(xn * xn, axis=-1, keepdims=True) + EPS) * fg_ref[...]
            o_ref[sl, :] = xn


def peer_experts(h, u_tab, vt_tab, a, e1, bm, e2, x, gate, fg, *, final, t0=0):
    b, s, d = x.shape
    nt = s // ROW_TILE
    m = b * s
    n_exp = u_tab.shape[1]
    i_per = PEER_ECH // PEER_NKEYS

    def gidx(half):
        def f(i, e):
            t = 2 * i + half
            return (t // nt, jnp.minimum(t % nt + t0, 1), 0, 0)
        return f

    n_ch = n_exp // PEER_ECH
    last = n_ch - 1
    mq = V7X_MXU_DIM
    n_lt = PEER_TOK // V7X_LANES
    assert mq == ROW_TILE and PEER_TOK == 2 * mq
    out = pl.pallas_call(
        functools.partial(_expert_body, final=final, n_ch=n_ch),
        grid=(m // PEER_TOK, n_ch + 1),
        in_specs=[
            pl.BlockSpec((PEER_TOK // mq, d, mq), lambda i, e: (i, 0, 0)),
            pl.BlockSpec((d // mq, PEER_ECH, mq), lambda i, e: (0, 0, 0)),
            pl.BlockSpec((d // mq, PEER_ECH, mq), lambda i, e: (0, jnp.minimum(e + 1, last), 0)),
            pl.BlockSpec((PEER_ECH // mq, d, mq), lambda i, e: (jnp.maximum(e - 1, 0), 0, 0)),
            pl.BlockSpec((PEER_HEADS, i_per, PEER_TOK), lambda i, e: (0, jnp.minimum(e, last), i)),
            pl.BlockSpec((PEER_HEADS, i_per, PEER_TOK), lambda i, e: (0, jnp.minimum(e, last), i)),
            pl.BlockSpec((PEER_HEADS, PEER_TOK // V7X_LANES, PEER_NKEYS, V7X_LANES), lambda i, e: (0, i, 0, 0)),
            pl.BlockSpec((PEER_HEADS, PEER_TOK // V7X_LANES, PEER_NKEYS, V7X_LANES), lambda i, e: (0, i, 0, 0)),
            pl.BlockSpec((PEER_TOK, d), lambda i, e: (i, 0)),
            pl.BlockSpec((1, 1, 1, d), gidx(0)),
            pl.BlockSpec((1, 1, 1, d), gidx(1)),
            pl.BlockSpec((1, d), lambda i, e: (0, 0)),
        ],
        out_specs=pl.BlockSpec((PEER_TOK, d), lambda i, e: (i, 0)),
        out_shape=jax.ShapeDtypeStruct((m, d), F32),
        scratch_shapes=[pltpu.VMEM((PEER_TOK // mq, d, mq), F32),
                        pltpu.VMEM((PEER_TOK // mq, PEER_ECH, mq), BF16),
                        pltpu.VMEM((PEER_TOK // mq, PEER_ECH, mq), BF16),
                        pltpu.VMEM((n_lt, PEER_ECH, V7X_LANES), F32),
                        pltpu.VMEM((n_lt, PEER_ECH, V7X_LANES), F32),
                        pltpu.VMEM((n_lt, PEER_HEADS * i_per, TILE_ROWS, V7X_LANES), F32),
                        pltpu.VMEM((n_lt, PEER_HEADS * i_per, TILE_ROWS, V7X_LANES), F32)],
        compiler_params=_cparams(("parallel", "arbitrary")),
        name="peer_experts",
    )(h, u_tab, u_tab, vt_tab, a, e1, bm, e2, x.reshape(m, d), gate, gate, fg)
    return out.reshape(b, s, d)


def _axial_tables(rows, rope_dim, ctx_len, lane0):
    n_freq = rope_dim // 4
    half = rope_dim // 2
    inv_freq = ROPE_THETA ** (-jnp.arange(n_freq, dtype=F32) / n_freq)
    r = jnp.repeat(jnp.arange(rows, dtype=F32), GRID_W)
    col = jnp.tile(jnp.arange(GRID_W, dtype=F32), rows)
    ang = jnp.concatenate([r[:, None] * inv_freq, col[:, None] * inv_freq], axis=-1)
    cos, sin = jnp.cos(ang), jnp.sin(ang)
    n = cos.shape[0]
    period = 64 if rope_dim == 64 else V7X_LANES
    c_blk = jnp.ones((n, period), F32).at[:, lane0:lane0 + rope_dim].set(jnp.concatenate([cos, cos], -1))
    s1_blk = jnp.zeros((n, period), F32).at[:, lane0:lane0 + half].set(-sin)
    s2_blk = jnp.zeros((n, period), F32).at[:, lane0 + half:lane0 + rope_dim].set(sin)
    reps = V7X_LANES // period
    out = []
    for blk, fill in ((c_blk, 1.0), (s1_blk, 0.0), (s2_blk, 0.0)):
        t = jnp.tile(blk, (1, reps))
        out.append(jnp.concatenate([jnp.full((ctx_len, V7X_LANES), fill, F32), t], axis=0))
    return out


def _block_diag(w):
    n, a, b = w.shape
    eye = jnp.eye(n, dtype=w.dtype)
    return (eye[:, None, :, None] * w[:, :, None, :]).reshape(n * a, n * b)


def kernel(x, c, ctx, c_ctx, w_mod, b_mod, norm1_g, norm2_g, w_in, conv_w, conv_b, lru_wa, lru_ba,
           lru_wi, lru_bi, lru_lambda, diff_lam, diff_subln_g, gqa_qnorm_g, gqa_knorm_g, mla_qnorm_g,
           mla_w_uq, mla_kvnorm_g, mla_w_ukv, w_branch, w_out, peer_wq, peer_keys, peer_u, peer_v,
           final_norm_g):
    bsz, seq, d = x.shape
    ctx_len = ctx.shape[1]
    depth = w_in.shape[0]
    assert ctx_len == ROW_TILE and seq % PEER_TOK == 0 and seq % GRID_W == 0
    rows = seq // GRID_W
    s_all = ctx_len + seq
    xs = jnp.concatenate([ctx, x], axis=1)

    tabs = _axial_tables(rows, DIFF_DK, ctx_len, 0) + _axial_tables(rows, MLA_ROPE, ctx_len, MLA_NOPE)
    grp = jnp.arange(BRANCH_W) // GQA_DH
    avg = (grp[:, None] == grp[None, :]).astype(BF16) * (1.0 / GQA_DH)
    sc_in = jnp.zeros((16, d), F32).at[:bsz].set(jax.nn.silu(c)).at[bsz].set(jax.nn.silu(c_ctx)).astype(BF16)
    zero1 = jnp.zeros((1,), F32)
    ones_g = jnp.ones((1, V7X_LANES), F32)

    for l in range(depth):
        lam_init = 0.8 - 0.6 * math.exp(-0.3 * l)
        mod_all = matmul(sc_in, w_mod[l].astype(BF16)) + b_mod[l]
        mod_b = mod_all[:bsz].reshape(bsz, N_MOD, d)
        mod_c = jnp.broadcast_to(mod_all[bsz].reshape(1, N_MOD, d), (bsz, N_MOD, d))
        mods = [jnp.stack([mod_c[:, k], mod_b[:, k]], axis=1)[:, :, None, :] for k in range(N_MOD)]

        wl = w_in[l]
        o = 0
        parts = []
        for w in (512, 512, 512, 512, 512, 512, 128, 128, 384, 256, 32, 4096):
            parts.append(wl[:, o:o + w])
            o += w
        xa, ya, qb, kb, vb, qg, kg, vg, cq, ckv, kr, zg = parts
        kr_blk = jnp.zeros((d, V7X_LANES), F32).at[:, MLA_NOPE:MLA_NOPE + MLA_ROPE].set(kr)
        w_cat = jnp.concatenate([zg, xa, ya, qb, kb, vb, qg, kg, vg, cq, ckv, kr_blk], axis=1).astype(BF16)

        z = adaln_matmul(xs, norm1_g[l][None], mods[0], mods[1], w_cat, ctx_len)

        negc = (-LRU_C * jax.nn.softplus(-lru_lambda[l]))[:, None, None, :]
        wg = jnp.stack([jnp.concatenate([_block_diag(lru_wa[l, dd]), _block_diag(lru_wi[l, dd])], axis=1)
                        for dd in range(2)]).astype(BF16)[:, None]
        bg = jnp.stack([jnp.concatenate([lru_ba[l, dd], lru_bi[l, dd]]) for dd in range(2)])[:, None, None, :]
        cb = conv_b[l][None]
        hf = lru_scan(z, 8, 9, conv_w[l], cb, wg[0], bg[0], negc[0], None, reverse=False)
        oa = lru_scan(z, 8, 9, conv_w[l], cb, wg[1], bg[1], negc[1], hf, reverse=True)

        uq = mla_w_uq[l].reshape(MLA_Q_RANK, MLA_HEADS, MLA_NOPE + MLA_ROPE)
        wuq = jnp.pad(uq, ((0, 0), (0, 0), (0, V7X_LANES - MLA_NOPE - MLA_ROPE))).reshape(MLA_Q_RANK, -1)
        ukv = mla_w_ukv[l].reshape(MLA_KV_RANK, MLA_HEADS, MLA_NOPE + MLA_DV)
        wuk = jnp.pad(ukv[:, :, :MLA_NOPE], ((0, 0), (0, 0), (0, V7X_LANES - MLA_NOPE))).reshape(MLA_KV_RANK, -1)
        wuv = ukv[:, :, MLA_NOPE:].reshape(MLA_KV_RANK, -1)
        qd, kd, vd, qq, kq, vq, qm, km, vm = attn_prep(
            z, tabs, avg, jnp.tile(gqa_qnorm_g[l], GQA_HEADS)[None], jnp.tile(gqa_knorm_g[l], GQA_KV_HEADS)[None],
            mla_qnorm_g[l][None], mla_kvnorm_g[l][None], wuq.astype(BF16), wuk.astype(BF16), wuv.astype(BF16))
        lv = diff_lam[l]
        lam = (jnp.exp(jnp.sum(lv[0] * lv[1])) - jnp.exp(jnp.sum(lv[2] * lv[3])) + lam_init).reshape(1)
        t0 = 1 if l == depth - 1 else 0
        ob = attention(qd, kd, vd, lam, diff_subln_g[l][None], mode="diff", out_scale=1.0 - lam_init, t0=t0)
        og = attention(qq, kq, vq, zero1, ones_g, mode="pair", t0=t0)
        om = attention(qm, km, vm, zero1, ones_g, mode="mla", t0=t0)

        xs = merge(oa, ob, og, om, z, w_branch[l].astype(BF16), w_out[l].astype(BF16), xs, mods[2], t0=t0)

        kbig = jnp.stack([_block_diag(peer_keys[l, hd]) for hd in range(PEER_HEADS)]).astype(BF16)
        h2, pa, pb, pe1, pe2 = peer_route(xs, norm2_g[l][None], mods[3], mods[4], peer_wq[l].astype(BF16), kbig,
                                          t0=t0)
        n_exp = peer_u.shape[1]
        u_km = peer_u[l].astype(BF16).reshape(n_exp, d // V7X_MXU_DIM, V7X_MXU_DIM).transpose(1, 0, 2)
        vt_km = peer_v[l].astype(BF16).reshape(n_exp // V7X_MXU_DIM, V7X_MXU_DIM, d).transpose(0, 2, 1)
        xs = peer_experts(h2, u_km, vt_km, pa, pe1, pb, pe2, xs, mods[5],
                          final_norm_g[None], final=(l == depth - 1), t0=t0)
    return xs
```

```python
import functools
import math

import jax
import jax.numpy as jnp
from jax import lax
from jax.experimental import pallas as pl
from jax.experimental.pallas import tpu as pltpu

F32 = jnp.float32
BF16 = jnp.bfloat16

EPS = 1e-6
GRID_W = 64
ROPE_THETA = 10000.0
N_MOD = 6

D_RNN = 512
CONV_W = 4
LRU_C = 8.0

DIFF_DK = 64
GQA_HEADS = 8
GQA_KV_HEADS = 2
GQA_DH = 64
MLA_HEADS = 8
MLA_NOPE = 64
MLA_ROPE = 32
MLA_DV = 64
MLA_Q_RANK = 384
MLA_KV_RANK = 256
N_BRANCH = 4
BRANCH_W = 512

PEER_HEADS = 8
PEER_NKEYS = 128
PEER_DK = 128
PEER_TOPK = 16

V7X_LANES = 128
V7X_SUBLANES = 8
V7X_VMEM_BYTES = 64 * 1024 * 1024
V7X_MXU_DIM = 256
VMEM_LIMIT = V7X_VMEM_BYTES * 3 // 4

ROW_TILE = 256


def _cparams(sem):
    return pltpu.CompilerParams(dimension_semantics=sem, vmem_limit_bytes=VMEM_LIMIT)


def _mm_body(a_ref, w_ref, o_ref):
    o_ref[...] = jnp.dot(a_ref[...], w_ref[...], preferred_element_type=F32).astype(o_ref.dtype)


def matmul(a, w, out_dtype=F32, tm=512, tn=512):
    m, k = a.shape
    _, n = w.shape
    tm = math.gcd(tm, m)
    tn = math.gcd(tn, n)
    return pl.pallas_call(
        _mm_body,
        grid=(n // tn, m // tm),
        in_specs=[
            pl.BlockSpec((tm, k), lambda j, i: (i, 0)),
            pl.BlockSpec((k, tn), lambda j, i: (0, j)),
        ],
        out_specs=pl.BlockSpec((tm, tn), lambda j, i: (i, j)),
        out_shape=jax.ShapeDtypeStruct((m, n), out_dtype),
        compiler_params=_cparams(("parallel", "parallel")),
        name="matmul",
    )(a, w)


def _adaln_mm_body(x_ref, g_ref, sh_ref, sc_ref, w_ref, o_ref, h_ref, *, ctx_len):
    tm = x_ref.shape[1]

    @pl.when(pl.program_id(2) == 0)
    def _():
        x = x_ref[0]
        y = x * lax.rsqrt(jnp.mean(x * x, axis=-1, keepdims=True) + EPS) * g_ref[...]
        row = pl.program_id(1) * tm + lax.broadcasted_iota(jnp.int32, (tm, 1), 0)
        is_ctx = row < ctx_len
        scale = jnp.where(is_ctx, sc_ref[0, 0], sc_ref[0, 1])
        shift = jnp.where(is_ctx, sh_ref[0, 0], sh_ref[0, 1])
        h_ref[...] = (y * (1.0 + scale) + shift).astype(BF16)

    o_ref[0] = jnp.dot(h_ref[...], w_ref[...], preferred_element_type=F32).astype(o_ref.dtype)


def adaln_matmul(x, g, shift, scale, w, ctx_len, n_row_tiles=4, tn=1024):
    b, s, d = x.shape
    n = w.shape[1]
    tm = s // n_row_tiles
    assert tm * n_row_tiles == s and tm % (2 * V7X_SUBLANES) == 0 and n % tn == 0
    mod = pl.BlockSpec((1, 2, 1, d), lambda bi, ti, j: (bi, 0, 0, 0))
    return pl.pallas_call(
        functools.partial(_adaln_mm_body, ctx_len=ctx_len),
        grid=(b, n_row_tiles, n // tn),
        in_specs=[
            pl.BlockSpec((1, tm, d), lambda bi, ti, j: (bi, ti, 0)),
            pl.BlockSpec((1, d), lambda bi, ti, j: (0, 0)),
            mod, mod,
            pl.BlockSpec((d, tn), lambda bi, ti, j: (0, j)),
        ],
        out_specs=pl.BlockSpec((1, tm, tn), lambda bi, ti, j: (bi, ti, j)),
        out_shape=jax.ShapeDtypeStruct((b, s, n), BF16),
        scratch_shapes=[pltpu.VMEM((tm, d), BF16)],
        compiler_params=_cparams(("parallel", "parallel", "arbitrary")),
        name="adaln_matmul",
    )(x, g, shift, scale, w)


def _mult(x, m):
    return x if isinstance(x, int) else pl.multiple_of(x, m)


def _gelu2(x):
    c = math.sqrt(2.0 / math.pi)
    return x * (1.0 + jnp.tanh(x * (c + (0.044715 * c) * (x * x))))


def _gelu(x):
    return 0.5 * _gelu2(x)


def _sigmoid(x):
    return 0.5 + 0.5 * jnp.tanh(0.5 * x)


TILE_ROWS = V7X_SUBLANES
HALO_ROWS = 2 * V7X_SUBLANES


def _lru_body(x_ref, xp_ref, xn_ref, cw_ref, cb_ref, wg_ref, bg_ref, nc_ref, *rest,
              reverse, n_chunks):
    if reverse:
        ya_ref, hf_ref, o_ref, carry_ref = rest
    else:
        o_ref, carry_ref = rest
    step = pl.program_id(1)
    if reverse:
        chunk = jnp.where(step == 0, 0, n_chunks - step)
    else:
        chunk = step
    rows = ROW_TILE
    c = x_ref.shape[-1]

    @pl.when(step == 0)
    def _():
        carry_ref[...] = jnp.zeros_like(carry_ref)

    x = x_ref[0].astype(F32)
    xx = jnp.concatenate([xp_ref[0].astype(F32), x, xn_ref[0].astype(F32)], axis=0)
    n_xx = rows + 2 * HALO_ROWS
    row = lax.broadcasted_iota(jnp.int32, (rows, c), 0)
    seg_start = jnp.logical_or(chunk == 0, chunk == 1)
    seg_end = jnp.logical_or(chunk == 0, chunk == n_chunks - 1)

    def shifted(d):
        return pltpu.roll(xx, (n_xx - d) % n_xx, axis=0)[HALO_ROWS:HALO_ROWS + rows]

    x_m1 = jnp.where(jnp.logical_and(seg_start, row == 0), 0.0, shifted(-1))
    x_p1 = jnp.where(jnp.logical_and(seg_end, row >= rows - 1), 0.0, shifted(1))
    x_p2 = jnp.where(jnp.logical_and(seg_end, row >= rows - 2), 0.0, shifted(2))
    cw = cw_ref[...]
    u = cw[0:1] * x_m1 + cw[1:2] * x + cw[2:3] * x_p1 + cw[3:4] * x_p2 + cb_ref[...]

    z = jnp.dot(u.astype(BF16), wg_ref[0], preferred_element_type=F32) + bg_ref[0]
    r = _sigmoid(z[:, :c])
    gi = _sigmoid(z[:, c:])
    a = jnp.exp(nc_ref[0] * r)
    bv = jnp.sqrt(1.0 - a * a) * (gi * u)

    sub = row % TILE_ROWS
    for dstep in (1, 2, 4):
        if reverse:
            sh = (rows - dstep) % rows
            keep = sub <= TILE_ROWS - 1 - dstep
        else:
            sh = dstep
            keep = sub >= dstep
        a_sh = jnp.where(keep, pltpu.roll(a, sh, axis=0), 1.0)
        b_sh = jnp.where(keep, pltpu.roll(bv, sh, axis=0), 0.0)
        bv = a * b_sh + bv
        a = a * a_sh

    carry = carry_ref[...]
    n_tiles = rows // TILE_ROWS
    order = range(n_tiles - 1, -1, -1) if reverse else range(n_tiles)
    hs = [None] * n_tiles
    for t in order:
        sl = slice(t * TILE_ROWS, (t + 1) * TILE_ROWS)
        h = bv[sl] + a[sl] * carry
        hs[t] = h
        edge = h[0:1] if reverse else h[TILE_ROWS - 1:TILE_ROWS]
        carry = jnp.broadcast_to(edge, (TILE_ROWS, c))
    carry_ref[...] = carry
    h_all = jnp.concatenate(hs, axis=0)
    if reverse:
        o_ref[0] = (_gelu(ya_ref[0].astype(F32)) * (hf_ref[0] + h_all)).astype(o_ref.dtype)
    else:
        o_ref[0] = h_all


def lru_scan(z, xa_col, ya_col, conv_w, conv_b, wg, bg, negc, hf, *, reverse):
    b, s, _ = z.shape
    c = D_RNN
    n_chunks = s // ROW_TILE
    per = ROW_TILE // HALO_ROWS
    n8 = s // HALO_ROWS
    if reverse:
        cidx = lambda st: jnp.where(st == 0, 0, n_chunks - st)
    else:
        cidx = lambda st: st
    in_specs = [
        pl.BlockSpec((1, ROW_TILE, c), lambda bi, st: (bi, cidx(st), xa_col)),
        pl.BlockSpec((1, HALO_ROWS, c), lambda bi, st: (bi, jnp.maximum(cidx(st) * per - 1, 0), xa_col)),
        pl.BlockSpec((1, HALO_ROWS, c), lambda bi, st: (bi, jnp.minimum((cidx(st) + 1) * per, n8 - 1), xa_col)),
        pl.BlockSpec((CONV_W, c), lambda bi, st: (0, 0)),
        pl.BlockSpec((1, c), lambda bi, st: (0, 0)),
        pl.BlockSpec((1, c, 2 * c), lambda bi, st: (0, 0, 0)),
        pl.BlockSpec((1, 1, 2 * c), lambda bi, st: (0, 0, 0)),
        pl.BlockSpec((1, 1, c), lambda bi, st: (0, 0, 0)),
    ]
    args = [z, z, z, conv_w, conv_b, wg, bg, negc]
    if reverse:
        in_specs += [
            pl.BlockSpec((1, ROW_TILE, c), lambda bi, st: (bi, cidx(st), ya_col)),
            pl.BlockSpec((1, ROW_TILE, c), lambda bi, st: (bi, cidx(st), 0)),
        ]
        args += [z, hf]
    return pl.pallas_call(
        functools.partial(_lru_body, reverse=reverse, n_chunks=n_chunks),
        grid=(b, n_chunks),
        in_specs=in_specs,
        out_specs=pl.BlockSpec((1, ROW_TILE, c), lambda bi, st: (bi, cidx(st), 0)),
        out_shape=jax.ShapeDtypeStruct((b, s, c), BF16 if reverse else F32),
        scratch_shapes=[pltpu.VMEM((TILE_ROWS, c), F32)],
        compiler_params=_cparams(("parallel", "arbitrary")),
        name="lru_rev" if reverse else "lru_fwd",
    )(*args)


def _tile_lanes(t, n):
    return jnp.concatenate([t] * n, axis=1) if n > 1 else t


def _rope(x, c, s1, s2, r):
    w = x.shape[-1]
    n = w // V7X_LANES
    return (x * _tile_lanes(c, n) + pltpu.roll(x, w - r, axis=1) * _tile_lanes(s1, n)
            + pltpu.roll(x, r, axis=1) * _tile_lanes(s2, n))


def _split_dot(x, m_ref):
    hi = x.astype(BF16)
    lo = (x - hi.astype(F32)).astype(BF16)
    m = m_ref[...]
    return jnp.dot(hi, m, preferred_element_type=F32) + jnp.dot(lo, m, preferred_element_type=F32)


def _with_ones(v):
    ones = jnp.ones((v.shape[0], V7X_LANES), v.dtype)
    parts = []
    for c in range(v.shape[1] // V7X_LANES):
        parts += [v[:, c * V7X_LANES:(c + 1) * V7X_LANES], ones]
    return jnp.concatenate(parts, axis=1)


def _rms(x, g):
    return x * lax.rsqrt(jnp.mean(x * x, axis=-1, keepdims=True) + EPS) * g


def _prep_body(qb_ref, kb_ref, vb_ref, qg_ref, tail_ref,
               c64_ref, s164_ref, s264_ref, cm_ref, s1m_ref, s2m_ref,
               avg_ref, gq_ref, gk_ref, gcq_ref, gckv_ref, wuq_ref, wuk_ref, wuv_ref,
               qd_ref, kd_ref, vd_ref, qq_ref, kq_ref, vq_ref, qm_ref, km_ref, vm_ref):
    c64, s164, s264 = c64_ref[...], s164_ref[...], s264_ref[...]
    cm, s1m, s2m = cm_ref[...], s1m_ref[...], s2m_ref[...]
    half64 = DIFF_DK // 2
    qd_ref[0] = (_rope(qb_ref[0].astype(F32), c64, s164, s264, half64) * (DIFF_DK ** -0.5)).astype(BF16)
    kd_ref[0] = _rope(kb_ref[0].astype(F32), c64, s164, s264, half64).astype(BF16)
    vd_ref[0] = _with_ones(vb_ref[0].astype(BF16))
    qg = qg_ref[0].astype(F32)
    ms = _split_dot(qg * qg, avg_ref)
    qn = qg * lax.rsqrt(ms + EPS) * gq_ref[...]
    qq_ref[0] = (_rope(qn, c64, s164, s264, GQA_DH // 2) * (GQA_DH ** -0.5)).astype(BF16)
    tail = tail_ref[0].astype(F32)
    kg = tail[:, 0:128]
    msk = _split_dot(kg * kg, avg_ref.at[0:128, 0:128])
    kn = _rope(kg * lax.rsqrt(msk + EPS) * gk_ref[...], c64, s164, s264, GQA_DH // 2).astype(BF16)
    vg = tail[:, 128:256].astype(BF16)
    kq_ref[0] = jnp.concatenate([kn[:, 0:64], kn[:, 0:64], kn[:, 64:128], kn[:, 64:128]], axis=1)
    vq_ref[0] = _with_ones(jnp.concatenate([vg[:, 0:64], vg[:, 0:64], vg[:, 64:128], vg[:, 64:128]], axis=1))
    cq = tail[:, 256:256 + MLA_Q_RANK]
    ckv = tail[:, 640:640 + MLA_KV_RANK]
    kr = tail[:, 896:1024]
    qf = jnp.dot(_rms(cq, gcq_ref[...]).astype(BF16), wuq_ref[...], preferred_element_type=F32)
    scale = (MLA_NOPE + MLA_ROPE) ** -0.5
    qm_ref[0] = (_rope(qf, cm, s1m, s2m, MLA_ROPE // 2) * scale).astype(BF16)
    ckvn = _rms(ckv, gckv_ref[...]).astype(BF16)
    kf = jnp.dot(ckvn, wuk_ref[...], preferred_element_type=F32)
    krr = _rope(kr, cm, s1m, s2m, MLA_ROPE // 2)
    km_ref[0] = (kf + _tile_lanes(krr, MLA_HEADS)).astype(BF16)
    vm_ref[0] = _with_ones(jnp.dot(ckvn, wuv_ref[...], preferred_element_type=F32).astype(BF16))


def attn_prep(z, tabs, avg, gq, gk, gcq, gckv, wuq, wuk, wuv):
    b, s, _ = z.shape
    nt = s // ROW_TILE
    zspec = lambda w, idx: pl.BlockSpec((1, ROW_TILE, w), lambda ti, bi: (bi, ti, idx))
    tab = pl.BlockSpec((ROW_TILE, V7X_LANES), lambda ti, bi: (ti, 0))
    full = lambda a: pl.BlockSpec(a.shape, lambda ti, bi: (0,) * a.ndim)
    ospec = lambda w: pl.BlockSpec((1, ROW_TILE, w), lambda ti, bi: (bi, ti, 0))
    oshape = lambda w: jax.ShapeDtypeStruct((b, s, w), BF16)
    widths = (512, 512, 1024, 512, 256, 512, 1024, 1024, 1024)
    return pl.pallas_call(
        _prep_body,
        grid=(nt, b),
        in_specs=[zspec(512, 10), zspec(512, 11), zspec(512, 12), zspec(512, 13), zspec(1024, 7)]
        + [tab] * 6 + [full(a) for a in (avg, gq, gk, gcq, gckv, wuq, wuk, wuv)],
        out_specs=[ospec(w) for w in widths],
        out_shape=[oshape(w) for w in widths],
        compiler_params=_cparams(("parallel", "parallel")),
        name="attn_prep",
    )(z, z, z, z, z, *tabs, avg, gq, gk, gcq, gckv, wuq, wuk, wuv)


ATTN_GROUPS = {"pair": 2, "diff": 2, "mla": 2}


def _attn_body(lam_ref, q_ref, k_ref, v_ref, g_ref, o_ref, *, mode, out_scale, t0, groups):
    ti = pl.program_id(2)
    lane = lax.broadcasted_iota(jnp.int32, (ROW_TILE, V7X_LANES), 1)
    lo_half = lane < (V7X_LANES // 2)

    ln = V7X_LANES

    def run(n_keys):
        for grp in range(groups):
            outs = []
            for j in range(2):
                if mode == "mla":
                    c0 = (2 * grp + j) * ln
                    qj = q_ref[0, :, c0:c0 + ln]
                    kj = k_ref[0, 0:n_keys, c0:c0 + ln]
                else:
                    q = q_ref[0, :, grp * ln:(grp + 1) * ln]
                    keep = lo_half if j == 0 else jnp.logical_not(lo_half)
                    qj = jnp.where(keep, q.astype(F32), 0.0).astype(BF16)
                    kc = (grp // 2 if mode == "pair" else grp) * ln
                    kj = k_ref[0, 0:n_keys, kc:kc + ln]
                vc = (grp // 2 if mode == "pair" else grp) * 2 * ln
                hk = (n_keys // (2 * ln)) * ln
                tn = (((1,), (1,)), ((), ()))
                s = jnp.concatenate(
                    [lax.dot_general(qj, kj[0:hk], tn, preferred_element_type=F32).astype(BF16),
                     lax.dot_general(qj, kj[hk:], tn, preferred_element_type=F32).astype(BF16)], axis=1)
                m = jnp.max(s, axis=-1, keepdims=True)
                p = jnp.exp(s - m)
                o = (jnp.dot(p[:, 0:hk], v_ref[0, 0:hk, vc:vc + 2 * ln], preferred_element_type=F32)
                     + jnp.dot(p[:, hk:], v_ref[0, hk:n_keys, vc:vc + 2 * ln], preferred_element_type=F32))
                outs.append(o[:, 0:ln] / o[:, ln:2 * ln])
            if mode == "diff":
                o = outs[0] - lam_ref[0] * outs[1]
                o = o * lax.rsqrt(jnp.mean(o * o, axis=-1, keepdims=True) + EPS) * g_ref[...] * out_scale
            else:
                o = jnp.where(lo_half, outs[0], outs[1])
            o_ref[0, :, grp * ln:(grp + 1) * ln] = o.astype(o_ref.dtype)

    if t0 == 0:
        @pl.when(ti == 0)
        def _():
            run(ROW_TILE)

        @pl.when(ti > 0)
        def _():
            run(k_ref.shape[1])
    else:
        run(k_ref.shape[1])


def attention(q, k, v, lam, g, *, mode, out_scale=1.0, t0=0):
    b, s, wq = q.shape
    groups = ATTN_GROUPS[mode]
    steps = 4 // groups
    qw = wq // steps
    kw = k.shape[-1] // steps
    vw = v.shape[-1] // steps
    ow = groups * V7X_LANES
    nt = s // ROW_TILE - t0
    return pl.pallas_call(
        functools.partial(_attn_body, mode=mode, out_scale=out_scale, t0=t0, groups=groups),
        grid=(b, steps, nt),
        in_specs=[
            pl.BlockSpec(memory_space=pltpu.SMEM),
            pl.BlockSpec((1, ROW_TILE, qw), lambda bi, gi, ti: (bi, ti + t0, gi)),
            pl.BlockSpec((1, s, kw), lambda bi, gi, ti: (bi, 0, gi)),
            pl.BlockSpec((1, s, vw), lambda bi, gi, ti: (bi, 0, gi)),
            pl.BlockSpec((1, V7X_LANES), lambda bi, gi, ti: (0, 0)),
        ],
        out_specs=pl.BlockSpec((1, ROW_TILE, ow), lambda bi, gi, ti: (bi, ti, gi)),
        out_shape=jax.ShapeDtypeStruct((b, nt * ROW_TILE, 4 * V7X_LANES), BF16),
        compiler_params=_cparams(("parallel", "parallel", "arbitrary")),
        name="attn_" + mode,
    )(lam, q, k, v, g)


def _merge_body(oa_ref, ob_ref, og_ref, om_ref, zg_ref, wb_ref, wo_ref, x_ref, gate_ref, o_ref):
    d = x_ref.shape[-1]
    merged = None
    for k, o_k in enumerate((oa_ref, ob_ref, og_ref, om_ref)):
        t = jnp.dot(o_k[0], wb_ref[k], preferred_element_type=F32)
        t = t * _sigmoid(zg_ref[0, :, k * d:(k + 1) * d].astype(F32))
        merged = t if merged is None else merged + t
    y = jnp.dot(merged.astype(BF16), wo_ref[...], preferred_element_type=F32)
    o_ref[0] = x_ref[0] + gate_ref[0, 0] * y


def merge(oa, ob, og, om, z, wb, wo, x, gate, t0=0):
    b, s, d = x.shape
    nt = s // ROW_TILE - t0
    full = lambda w: pl.BlockSpec((1, ROW_TILE, w), lambda bi, ti: (bi, ti + t0, 0))
    part = pl.BlockSpec((1, ROW_TILE, BRANCH_W), lambda bi, ti: (bi, ti, 0))
    return pl.pallas_call(
        _merge_body,
        grid=(b, nt),
        in_specs=[full(BRANCH_W), part, part, part,
                  full(N_BRANCH * d),
                  pl.BlockSpec(wb.shape, lambda bi, ti: (0, 0, 0)),
                  pl.BlockSpec(wo.shape, lambda bi, ti: (0, 0)),
                  full(d),
                  pl.BlockSpec((1, 1, 1, d), lambda bi, ti: (bi, jnp.minimum(ti + t0, 1), 0, 0))],
        out_specs=pl.BlockSpec((1, ROW_TILE, d), lambda bi, ti: (bi, ti, 0)),
        out_shape=jax.ShapeDtypeStruct((b, nt * ROW_TILE, d), F32),
        compiler_params=_cparams(("parallel", "parallel")),
        name="merge",
    )(oa, ob, og, om, z, wb, wo, x, gate)


NEG_BIG = float(jnp.finfo(jnp.float32).min)


def _oddeven_merge(lo, hi, r):
    step = r * 2
    if step < hi - lo:
        yield from _oddeven_merge(lo, hi, step)
        yield from _oddeven_merge(lo + r, hi, step)
        yield from [(i, i + r) for i in range(lo + r, hi - r, step)]
    else:
        yield (lo, lo + r)


def _oddeven_sort(lo, hi):
    if hi - lo >= 1:
        mid = lo + (hi - lo) // 2
        yield from _oddeven_sort(lo, mid)
        yield from _oddeven_sort(mid + 1, hi)
        yield from _oddeven_merge(lo, hi, 1)


def _bitonic_merge(n):
    pairs, s = [], n // 2
    while s >= 1:
        pairs += [(i, i + s) for i in range(n) if (i // s) % 2 == 0]
        s //= 2
    return pairs


_SORT16 = tuple(_oddeven_sort(0, PEER_TOPK - 1))
_BITONIC16 = tuple(_bitonic_merge(PEER_TOPK))


def _route_body(x_ref, g_ref, sh_ref, sc_ref, wq_ref, kb_ref,
                h_ref, a_ref, b_ref, e1_ref, e2_ref, st_ref):
    x = x_ref[0]
    ms = jnp.mean(x * x, axis=-1, keepdims=True)
    hf = x * lax.rsqrt(ms + EPS) * g_ref[...] * (1.0 + sc_ref[0, 0]) + sh_ref[0, 0]
    h_ref[0] = hf.T.astype(BF16)
    h = hf.astype(BF16)
    q = jnp.dot(h, wq_ref[...], preferred_element_type=F32).astype(BF16)
    nk = PEER_NKEYS
    for hd in range(PEER_HEADS):
        st_ref[hd * 2 * nk:(hd + 1) * 2 * nk, :] = lax.dot_general(
            kb_ref[hd], q[:, hd * PEER_DK:(hd + 1) * PEER_DK], (((1,), (1,)), ((), ())),
            preferred_element_type=F32)

    def exchange(t, pairs):
        for i, j in pairs:
            t[i], t[j] = jnp.maximum(t[i], t[j]), jnp.minimum(t[i], t[j])

    def merge_top(a, b):
        t = [jnp.maximum(a[k], b[PEER_TOPK - 1 - k]) for k in range(PEER_TOPK)]
        exchange(t, _BITONIC16)
        return t

    def across_sublanes(t):
        for sh in (4, 2, 1):
            t = merge_top(t, [pltpu.roll(x, sh, axis=0) for x in t])
        return t

    def top16(s):
        t = [s[k * TILE_ROWS:(k + 1) * TILE_ROWS] for k in range(nk // TILE_ROWS)]
        exchange(t, _SORT16)
        return across_sublanes(t)

    row8 = lax.broadcasted_iota(jnp.int32, (TILE_ROWS, V7X_LANES), 0)
    n_q = jnp.zeros((TILE_ROWS, V7X_LANES), jnp.int32)
    for p in range(TILE_ROWS):
        n_q = jnp.where(row8 == p, PEER_TOPK // (p + 1), n_q)

    def rank_rows(t, lo):
        out = t[lo]
        for r in range(1, TILE_ROWS):
            out = jnp.where(row8 == r, t[lo + r], out)
        return out

    def head(it, carry):
        hd = it // (ROW_TILE // V7X_LANES)
        lt = it % (ROW_TILE // V7X_LANES)
        base = pl.multiple_of(hd * (2 * nk), 2 * nk)
        tl = pl.ds(pl.multiple_of(lt * V7X_LANES, V7X_LANES), V7X_LANES)
        s1 = st_ref[pl.ds(base, nk), tl]
        s2 = st_ref[pl.ds(base + nk, nk), tl]
        ta = top16(s1)
        tb = top16(s2)
        lo = rank_rows(ta, 0)
        cand = [jnp.where(n_q > q, lo + tb[q], NEG_BIG) for q in range(PEER_TOPK)]
        cand[PEER_TOPK - 1] = jnp.maximum(cand[PEER_TOPK - 1], rank_rows(ta, TILE_ROWS) + tb[0])
        exchange(cand, _BITONIC16)
        cand = across_sublanes(cand)
        top = cand[0][0:1]
        tau = cand[PEER_TOPK - 1][0:1]
        zsum = jnp.zeros_like(top)
        for k in range(PEER_TOPK):
            zsum = zsum + jnp.exp(cand[k][0:1] - top)
        a_ref[hd, :, tl] = tau - s1
        e1_ref[hd, :, tl] = jnp.exp(s1 - ta[0][0:1]) * (0.5 / zsum)
        b_ref[hd, lt] = s2
        e2_ref[hd, lt] = jnp.exp(s2 - tb[0][0:1])
        return carry

    lax.fori_loop(0, PEER_HEADS * (ROW_TILE // V7X_LANES), head, 0)


def peer_route(x, g, shift, scale, wq, kbig, t0=0):
    b, s, d = x.shape
    nt = s // ROW_TILE
    m = b * s
    seg = lambda bi, ti: (bi, jnp.minimum(ti + t0, 1), 0, 0)
    flat = lambda bi, ti: (0, 0, bi * nt + ti)
    kspec = pl.BlockSpec((PEER_HEADS, PEER_NKEYS, ROW_TILE), flat)
    kshape = jax.ShapeDtypeStruct((PEER_HEADS, PEER_NKEYS, m), F32)
    lt_per = ROW_TILE // V7X_LANES
    tspec = pl.BlockSpec((PEER_HEADS, lt_per, PEER_NKEYS, V7X_LANES), lambda bi, ti: (0, bi * nt + ti, 0, 0))
    tshape = jax.ShapeDtypeStruct((PEER_HEADS, m // V7X_LANES, PEER_NKEYS, V7X_LANES), F32)
    return pl.pallas_call(
        _route_body,
        grid=(b, nt),
        in_specs=[
            pl.BlockSpec((1, ROW_TILE, d), lambda bi, ti: (bi, ti, 0)),
            pl.BlockSpec((1, d), lambda bi, ti: (0, 0)),
            pl.BlockSpec((1, 1, 1, d), seg),
            pl.BlockSpec((1, 1, 1, d), seg),
            pl.BlockSpec(wq.shape, lambda bi, ti: (0, 0)),
            pl.BlockSpec(kbig.shape, lambda bi, ti: (0, 0, 0)),
        ],
        out_specs=[pl.BlockSpec((1, d, ROW_TILE), lambda bi, ti: (bi * nt + ti, 0, 0)),
                   kspec, tspec, kspec, tspec],
        out_shape=[jax.ShapeDtypeStruct((m // ROW_TILE, d, ROW_TILE), BF16), kshape, tshape, kshape, tshape],
        scratch_shapes=[pltpu.VMEM((2 * PEER_HEADS * PEER_NKEYS, ROW_TILE), F32)],
        compiler_params=_cparams(("parallel", "parallel")),
        name="peer_route",
    )(x, g, shift, scale, wq, kbig)


PEER_TOK = 2 * ROW_TILE
PEER_ECH = 1024


def _expert_body(h_ref, *refs, final, n_ch, n_kt):
    u0_ref, u_ref = refs[:n_kt], refs[n_kt:2 * n_kt]
    (vt_ref, a_ref, e1_ref, b_ref, e2_ref, x_ref, g0_ref, g1_ref, fg_ref, o_ref, acc_ref,
     w0_ref, w1_ref, uf0_ref, uf1_ref, thb_ref, e1b_ref) = refs[2 * n_kt:]
    e = pl.program_id(1)
    nk = PEER_NKEYS
    n_il = PEER_ECH // nk
    sub = TILE_ROWS
    ib, jb = 8, nk // 4
    n_ib, n_jb = n_il // ib, nk // jb
    mq = V7X_MXU_DIM
    uf_refs = (uf0_ref, uf1_ref)
    w_refs = (w0_ref, w1_ref)

    def gate_block(g, cur):
        uf_ref, w_ref = uf_refs[cur], w_refs[cur]
        lt = g // (n_ib * n_jb)
        i0 = ((g // n_jb) % n_ib) * ib
        j0 = (g % n_jb) * jb
        gates = [None] * ib
        for hd in range(PEER_HEADS):
            bj = b_ref[hd, lt, j0:j0 + jb, :].reshape(jb // sub, sub, V7X_LANES)
            e2j = e2_ref[hd, lt, j0:j0 + jb, :].reshape(jb // sub, sub, V7X_LANES)
            for k in range(ib):
                th = thb_ref[lt, hd * n_il + i0 + k][None]
                e1 = e1b_ref[lt, hd * n_il + i0 + k][None]
                t = jnp.where(bj >= th, e1 * e2j, 0.0)
                gates[k] = t if gates[k] is None else gates[k] + t
        lpm = mq // V7X_LANES
        wl = pl.ds(_mult((lt % lpm) * V7X_LANES, V7X_LANES), V7X_LANES)
        for k in range(ib):
            rows = pl.ds((i0 + k) * nk + j0, jb)
            act = _gelu2(uf_ref[lt, rows, :].astype(BF16)) * gates[k].reshape(jb, V7X_LANES).astype(BF16)
            w_ref[lt // lpm, rows, wl] = act

    def k_major(ref, r0):
        tiles = ref if isinstance(ref, tuple) else [ref.at[k] for k in range(ref.shape[0])]
        return jnp.concatenate([t[pl.ds(r0, mq), :] for t in tiles], axis=1)

    def next_u(it, oth):
        r0 = _mult((it // 2) * mq, mq)
        res = jnp.dot(k_major(u_ref, r0), h_ref[it % 2], preferred_element_type=F32)
        for k in range(mq // V7X_LANES):
            uf_refs[oth][(it % 2) * (mq // V7X_LANES) + k, pl.ds(r0, mq), :] = (
                res[:, k * V7X_LANES:(k + 1) * V7X_LANES])

    def prev_out(it, oth):
        r0 = _mult((it // 2) * mq, mq)
        acc_ref[it % 2, pl.ds(r0, mq), :] += jnp.dot(
            k_major(vt_ref, r0), w_refs[oth][it % 2], preferred_element_type=F32)

    n_it = (PEER_ECH // mq) * (PEER_TOK // mq)
    per_it = (PEER_TOK // V7X_LANES) * n_ib * n_jb // n_it

    def run_chunk(with_prev, cur):
        for lt in range(PEER_TOK // V7X_LANES):
            tl = slice(lt * V7X_LANES, (lt + 1) * V7X_LANES)
            for hd in range(PEER_HEADS):
                for il in range(n_il):
                    thb_ref[lt, hd * n_il + il] = jnp.broadcast_to(a_ref[hd, il:il + 1, tl], (sub, V7X_LANES))
                    e1b_ref[lt, hd * n_il + il] = jnp.broadcast_to(e1_ref[hd, il:il + 1, tl], (sub, V7X_LANES))

        lpm = mq // V7X_LANES

        def step(it, carry):
            for k in range(per_it):
                gate_block(it * per_it + k, cur)
            next_u(it, 1 - cur)
            if with_prev:
                prev_out(it, 1 - cur)
            elif it < PEER_ECH // mq:
                r0 = it * mq
                res = jnp.dot(k_major(u0_ref, r0), h_ref[1], preferred_element_type=F32)
                for k in range(lpm):
                    uf0_ref[lpm + k, pl.ds(r0, mq), :] = res[:, k * V7X_LANES:(k + 1) * V7X_LANES]
            return carry

        for it in range(n_it):
            step(it, 0)

    @pl.when(e == 0)
    def _():
        acc_ref[...] = jnp.zeros_like(acc_ref)
        u_all = jnp.concatenate([t[...] for t in u0_ref], axis=1)
        u_first = jnp.dot(u_all, h_ref[0], preferred_element_type=F32)
        for k in range(mq // V7X_LANES):
            uf0_ref[k] = u_first[:, k * V7X_LANES:(k + 1) * V7X_LANES]
        run_chunk(False, 0)

    for par in range(2):
        @pl.when(jnp.logical_and(jnp.logical_and(e > 0, e < n_ch), e % 2 == par))
        def _():
            run_chunk(True, par)

    @pl.when(e == n_ch)
    def _():
        vt_all = jnp.concatenate([vt_ref[k] for k in range(vt_ref.shape[0])], axis=1)
        for half, gref in enumerate((g0_ref, g1_ref)):
            y = acc_ref[half] + jnp.dot(vt_all, w_refs[(n_ch - 1) % 2][half], preferred_element_type=F32)
            sl = slice(half * ROW_TILE, (half + 1) * ROW_TILE)
            xn = x_ref[sl, :] + gref[0, 0] * y.T
            if final:
                xn = xn * lax.rsqrt(jnp.mean(xn * xn, axis=-1, keepdims=True) + EPS) * fg_ref[...]
            o_ref[sl, :] = xn


def peer_experts(h, u_tab, vt_tab, a, e1, bm, e2, x, gate, fg, *, final, t0=0):
    b, s, d = x.shape
    nt = s // ROW_TILE
    m = b * s
    n_exp = u_tab.shape[0]
    i_per = PEER_ECH // PEER_NKEYS

    def gidx(half):
        def f(i, e):
            t = 2 * i + half
            return (t // nt, jnp.minimum(t % nt + t0, 1), 0, 0)
        return f

    n_ch = n_exp // PEER_ECH
    last = n_ch - 1
    mq = V7X_MXU_DIM
    n_kt = d // mq
    n_lt = PEER_TOK // V7X_LANES
    assert mq == ROW_TILE and PEER_TOK == 2 * mq
    out = pl.pallas_call(
        functools.partial(_expert_body, final=final, n_ch=n_ch, n_kt=n_kt),
        grid=(m // PEER_TOK, n_ch + 1),
        in_specs=[pl.BlockSpec((PEER_TOK // mq, d, mq), lambda i, e: (i, 0, 0))]
        + [pl.BlockSpec((PEER_ECH, mq), functools.partial(lambda i, e, k: (0, k), k=k)) for k in range(n_kt)]
        + [pl.BlockSpec((PEER_ECH, mq), functools.partial(lambda i, e, k: (jnp.minimum(e + 1, last), k), k=k))
           for k in range(n_kt)]
        + [
            pl.BlockSpec((PEER_ECH // mq, d, mq), lambda i, e: (jnp.maximum(e - 1, 0), 0, 0)),
            pl.BlockSpec((PEER_HEADS, i_per, PEER_TOK), lambda i, e: (0, jnp.minimum(e, last), i)),
            pl.BlockSpec((PEER_HEADS, i_per, PEER_TOK), lambda i, e: (0, jnp.minimum(e, last), i)),
            pl.BlockSpec((PEER_HEADS, PEER_TOK // V7X_LANES, PEER_NKEYS, V7X_LANES), lambda i, e: (0, i, 0, 0)),
            pl.BlockSpec((PEER_HEADS, PEER_TOK // V7X_LANES, PEER_NKEYS, V7X_LANES), lambda i, e: (0, i, 0, 0)),
            pl.BlockSpec((PEER_TOK, d), lambda i, e: (i, 0)),
            pl.BlockSpec((1, 1, 1, d), gidx(0)),
            pl.BlockSpec((1, 1, 1, d), gidx(1)),
            pl.BlockSpec((1, d), lambda i, e: (0, 0)),
        ],
        out_specs=pl.BlockSpec((PEER_TOK, d), lambda i, e: (i, 0)),
        out_shape=jax.ShapeDtypeStruct((m, d), F32),
        scratch_shapes=[pltpu.VMEM((PEER_TOK // mq, d, mq), F32),
                        pltpu.VMEM((PEER_TOK // mq, PEER_ECH, mq), BF16),
                        pltpu.VMEM((PEER_TOK // mq, PEER_ECH, mq), BF16),
                        pltpu.VMEM((n_lt, PEER_ECH, V7X_LANES), F32),
                        pltpu.VMEM((n_lt, PEER_ECH, V7X_LANES), F32),
                        pltpu.VMEM((n_lt, PEER_HEADS * i_per, TILE_ROWS, V7X_LANES), F32),
                        pltpu.VMEM((n_lt, PEER_HEADS * i_per, TILE_ROWS, V7X_LANES), F32)],
        compiler_params=_cparams(("parallel", "arbitrary")),
        name="peer_experts",
    )(h, *([u_tab] * (2 * n_kt)), vt_tab, a, e1, bm, e2, x.reshape(m, d), gate, gate, fg)
    return out.reshape(b, s, d)


def _axial_tables(rows, rope_dim, ctx_len, lane0):
    n_freq = rope_dim // 4
    half = rope_dim // 2
    inv_freq = ROPE_THETA ** (-jnp.arange(n_freq, dtype=F32) / n_freq)
    r = jnp.repeat(jnp.arange(rows, dtype=F32), GRID_W)
    col = jnp.tile(jnp.arange(GRID_W, dtype=F32), rows)
    ang = jnp.concatenate([r[:, None] * inv_freq, col[:, None] * inv_freq], axis=-1)
    cos, sin = jnp.cos(ang), jnp.sin(ang)
    n = cos.shape[0]
    period = 64 if rope_dim == 64 else V7X_LANES
    c_blk = jnp.ones((n, period), F32).at[:, lane0:lane0 + rope_dim].set(jnp.concatenate([cos, cos], -1))
    s1_blk = jnp.zeros((n, period), F32).at[:, lane0:lane0 + half].set(-sin)
    s2_blk = jnp.zeros((n, period), F32).at[:, lane0 + half:lane0 + rope_dim].set(sin)
    reps = V7X_LANES // period
    out = []
    for blk, fill in ((c_blk, 1.0), (s1_blk, 0.0), (s2_blk, 0.0)):
        t = jnp.tile(blk, (1, reps))
        out.append(jnp.concatenate([jnp.full((ctx_len, V7X_LANES), fill, F32), t], axis=0))
    return out


def _block_diag(w):
    n, a, b = w.shape
    eye = jnp.eye(n, dtype=w.dtype)
    return (eye[:, None, :, None] * w[:, :, None, :]).reshape(n * a, n * b)


def kernel(x, c, ctx, c_ctx, w_mod, b_mod, norm1_g, norm2_g, w_in, conv_w, conv_b, lru_wa, lru_ba,
           lru_wi, lru_bi, lru_lambda, diff_lam, diff_subln_g, gqa_qnorm_g, gqa_knorm_g, mla_qnorm_g,
           mla_w_uq, mla_kvnorm_g, mla_w_ukv, w_branch, w_out, peer_wq, peer_keys, peer_u, peer_v,
           final_norm_g):
    bsz, seq, d = x.shape
    ctx_len = ctx.shape[1]
    depth = w_in.shape[0]
    assert ctx_len == ROW_TILE and seq % PEER_TOK == 0 and seq % GRID_W == 0
    rows = seq // GRID_W
    s_all = ctx_len + seq
    xs = jnp.concatenate([ctx, x], axis=1)

    tabs = _axial_tables(rows, DIFF_DK, ctx_len, 0) + _axial_tables(rows, MLA_ROPE, ctx_len, MLA_NOPE)
    grp = jnp.arange(BRANCH_W) // GQA_DH
    avg = (grp[:, None] == grp[None, :]).astype(BF16) * (1.0 / GQA_DH)
    sc_in = jnp.zeros((16, d), F32).at[:bsz].set(jax.nn.silu(c)).at[bsz].set(jax.nn.silu(c_ctx)).astype(BF16)
    zero1 = jnp.zeros((1,), F32)
    ones_g = jnp.ones((1, V7X_LANES), F32)

    for l in range(depth):
        lam_init = 0.8 - 0.6 * math.exp(-0.3 * l)
        mod_all = matmul(sc_in, w_mod[l].astype(BF16)) + b_mod[l]
        mod_b = mod_all[:bsz].reshape(bsz, N_MOD, d)
        mod_c = jnp.broadcast_to(mod_all[bsz].reshape(1, N_MOD, d), (bsz, N_MOD, d))
        mods = [jnp.stack([mod_c[:, k], mod_b[:, k]], axis=1)[:, :, None, :] for k in range(N_MOD)]

        wl = w_in[l]
        o = 0
        parts = []
        for w in (512, 512, 512, 512, 512, 512, 128, 128, 384, 256, 32, 4096):
            parts.append(wl[:, o:o + w])
            o += w
        xa, ya, qb, kb, vb, qg, kg, vg, cq, ckv, kr, zg = parts
        kr_blk = jnp.zeros((d, V7X_LANES), F32).at[:, MLA_NOPE:MLA_NOPE + MLA_ROPE].set(kr)
        w_cat = jnp.concatenate([zg, xa, ya, qb, kb, vb, qg, kg, vg, cq, ckv, kr_blk], axis=1).astype(BF16)

        z = adaln_matmul(xs, norm1_g[l][None], mods[0], mods[1], w_cat, ctx_len)

        negc = (-LRU_C * jax.nn.softplus(-lru_lambda[l]))[:, None, None, :]
        wg = jnp.stack([jnp.concatenate([_block_diag(lru_wa[l, dd]), _block_diag(lru_wi[l, dd])], axis=1)
                        for dd in range(2)]).astype(BF16)[:, None]
        bg = jnp.stack([jnp.concatenate([lru_ba[l, dd], lru_bi[l, dd]]) for dd in range(2)])[:, None, None, :]
        cb = conv_b[l][None]
        hf = lru_scan(z, 8, 9, conv_w[l], cb, wg[0], bg[0], negc[0], None, reverse=False)
        oa = lru_scan(z, 8, 9, conv_w[l], cb, wg[1], bg[1], negc[1], hf, reverse=True)

        uq = mla_w_uq[l].reshape(MLA_Q_RANK, MLA_HEADS, MLA_NOPE + MLA_ROPE)
        wuq = jnp.pad(uq, ((0, 0), (0, 0), (0, V7X_LANES - MLA_NOPE - MLA_ROPE))).reshape(MLA_Q_RANK, -1)
        ukv = mla_w_ukv[l].reshape(MLA_KV_RANK, MLA_HEADS, MLA_NOPE + MLA_DV)
        wuk = jnp.pad(ukv[:, :, :MLA_NOPE], ((0, 0), (0, 0), (0, V7X_LANES - MLA_NOPE))).reshape(MLA_KV_RANK, -1)
        wuv = ukv[:, :, MLA_NOPE:].reshape(MLA_KV_RANK, -1)
        qd, kd, vd, qq, kq, vq, qm, km, vm = attn_prep(
            z, tabs, avg, jnp.tile(gqa_qnorm_g[l], GQA_HEADS)[None], jnp.tile(gqa_knorm_g[l], GQA_KV_HEADS)[None],
            mla_qnorm_g[l][None], mla_kvnorm_g[l][None], wuq.astype(BF16), wuk.astype(BF16), wuv.astype(BF16))
        lv = diff_lam[l]
        lam = (jnp.exp(jnp.sum(lv[0] * lv[1])) - jnp.exp(jnp.sum(lv[2] * lv[3])) + lam_init).reshape(1)
        t0 = 1 if l == depth - 1 else 0
        ob = attention(qd, kd, vd, lam, diff_subln_g[l][None], mode="diff", out_scale=1.0 - lam_init, t0=t0)
        og = attention(qq, kq, vq, zero1, ones_g, mode="pair", t0=t0)
        om = attention(qm, km, vm, zero1, ones_g, mode="mla", t0=t0)

        xs = merge(oa, ob, og, om, z, w_branch[l].astype(BF16), w_out[l].astype(BF16), xs, mods[2], t0=t0)

        kbig = jnp.stack([_block_diag(peer_keys[l, hd]) for hd in range(PEER_HEADS)]).astype(BF16)
        h2, pa, pb, pe1, pe2 = peer_route(xs, norm2_g[l][None], mods[3], mods[4], peer_wq[l].astype(BF16), kbig,
                                          t0=t0)
        n_exp = peer_u.shape[1]
        vt_km = peer_v[l].astype(BF16).reshape(n_exp // V7X_MXU_DIM, V7X_MXU_DIM, d).transpose(0, 2, 1)
        xs = peer_experts(h2, peer_u[l].astype(BF16), vt_km, pa, pe1, pb, pe2, xs, mods[5],
                          final_norm_g[None], final=(l == depth - 1), t0=t0)
    return xs
```
